```python
import jax, jax.numpy as jnp
from jax import lax
import numpy as np

D_MODEL = 2048
BATCH = 16
SEQ = 2048
DEPTH = 1
DEC_BATCH = 16
DEC_SEQ = 32
PAST_LEN = 1024

CHUNK = 64
CONV_WIDTH = 3
D_CONV = D_MODEL
N_CONV_GROUPS = 32
N_RET_HEADS = 8
RET_DK = D_MODEL // 16
RET_DV = D_MODEL // 8
D_RET_QK = N_RET_HEADS * RET_DK
D_RET_V = N_RET_HEADS * RET_DV
D_BRANCH = D_MODEL
ROPE_BASE = 10000.0
EPS = 1e-6
IN_SIZES = (D_CONV, D_CONV, D_CONV, D_CONV, D_RET_QK, D_RET_QK, D_RET_V, D_RET_V, D_MODEL, D_MODEL)
D_IN_TOTAL = sum(IN_SIZES)
SPLIT_POINTS = tuple(int(s) for s in np.cumsum(IN_SIZES)[:-1])

kernel_name = "hybrid_stream_conv_retention_step"


def rms_norm(x, g):
    xf = x.astype(jnp.float32)
    y = xf * lax.rsqrt(jnp.mean(xf * xf, axis=-1, keepdims=True) + EPS)
    return (y * g.astype(jnp.float32)).astype(x.dtype)


def rope(x, pos):
    d = x.shape[-1]
    inv = ROPE_BASE ** (-jnp.arange(0, d, 2, dtype=jnp.float32) / d)
    ang = pos.astype(jnp.float32)[:, None] * inv[None, :]
    cos = jnp.cos(ang)[None, :, None, :]
    sin = jnp.sin(ang)[None, :, None, :]
    x1, x2 = x[..., : d // 2], x[..., d // 2:]
    return jnp.concatenate([x1 * cos - x2 * sin, x1 * sin + x2 * cos], axis=-1)


def causal_conv(u, hist, w, b):
    T = u.shape[1]
    full = jnp.concatenate([hist.astype(u.dtype), u], axis=1)
    y = b + sum(w[j] * full[:, j:j + T] for j in range(CONV_WIDTH))
    return y, full[:, -(CONV_WIDTH - 1):]


def retention_block(state, qb, kb, vb, lg):
    L = qb.shape[2]
    idx = jnp.arange(L, dtype=jnp.float32)
    intra = jnp.exp(lg[:, None, None] * jnp.abs(idx[:, None] - idx[None, :]))
    q_dec = jnp.exp(lg[:, None] * (idx + 1.0))[..., None]
    k_dec = jnp.exp(lg[:, None] * (L - 1.0 - idx))[..., None]
    blk_dec = jnp.exp(lg * L)[:, None, None]
    scores = jnp.einsum('bhid,bhjd->bhij', qb, kb) * intra
    o = (jnp.einsum('bhij,bhje->bhie', scores, vb)
         + q_dec * jnp.einsum('bhid,bhde->bhie', qb, state))
    new_state = blk_dec * state + jnp.einsum('bhjd,bhje->bhde', kb * k_dec, vb)
    return new_state, o


def retention(q, k, v, state, lg):
    B, H, T, _ = q.shape
    L = min(CHUNK, T)
    n = T // L

    def to_blocks(a):
        return a.reshape(B, H, n, L, a.shape[-1]).transpose(2, 0, 1, 3, 4)

    final, o = lax.scan(lambda s, blk: retention_block(s, blk[0], blk[1], blk[2], lg),
                        state, (to_blocks(q), to_blocks(k), to_blocks(v)))
    o = o.transpose(1, 2, 0, 3, 4).reshape(B, H, T, v.shape[-1])
    return o, final


def mixer_layer(x, c, conv_hist, ret_state, pos, ada_w, ada_b, g_pre, g_post,
                w_in, conv_w, conv_b, w_branch, w_out):
    B, T, _ = x.shape
    f32 = jnp.float32
    mod = jax.nn.silu(c) @ ada_w + ada_b
    shift, scale, gate = jnp.split(mod, 3, axis=-1)
    h = rms_norm(x, g_pre) * (1.0 + scale[:, None]) + shift[:, None]
    proj = h @ w_in
    gb, gc, gu, zc, q, k, v, zr, ma, mb = jnp.split(proj, SPLIT_POINTS, axis=-1)

    conv_out, new_hist = causal_conv(gc * gu, conv_hist, conv_w, conv_b)
    y_conv = gb * conv_out * jax.nn.silu(zc)

    lg = jnp.log(1.0 - 2.0 ** (-5.0 - jnp.arange(N_RET_HEADS, dtype=f32)))
    qh = rope(q.reshape(B, T, N_RET_HEADS, RET_DK).astype(f32), pos).transpose(0, 2, 1, 3)
    kh = (rope(k.reshape(B, T, N_RET_HEADS, RET_DK).astype(f32), pos) * (RET_DK ** -0.5)).transpose(0, 2, 1, 3)
    vh = v.reshape(B, T, N_RET_HEADS, RET_DV).astype(f32).transpose(0, 2, 1, 3)
    o, new_state = retention(qh, kh, vh, ret_state.astype(f32), lg)
    o = o * lax.rsqrt(jnp.mean(o * o, axis=-1, keepdims=True) + EPS)
    y_ret = o.transpose(0, 2, 1, 3).reshape(B, T, D_RET_V).astype(x.dtype) * jax.nn.silu(zr)

    merged = (jax.nn.sigmoid(ma) * (y_conv @ w_branch[0])
              + jax.nn.sigmoid(mb) * (y_ret @ w_branch[1]))
    out = merged @ w_out
    x = x + gate[:, None] * rms_norm(out, g_post)
    return x, new_hist, new_state.astype(x.dtype)


def setup_inputs(seed: int = 0) -> dict:
    key = jax.random.key(seed)
    ks = jax.random.split(key, 16)
    f32 = jnp.float32
    nrm = lambda k, shape, s: jax.random.normal(k, shape, f32) * s
    return {
        "x_prompt": nrm(ks[0], (BATCH, SEQ, D_MODEL), 1.0),
        "x_sample": nrm(ks[1], (DEC_BATCH, DEC_SEQ, D_MODEL), 1.0),
        "c_prompt": nrm(ks[2], (BATCH, D_MODEL), 1.0),
        "c_sample": nrm(ks[3], (DEC_BATCH, D_MODEL), 1.0),
        "state_conv": nrm(ks[4], (DEPTH, DEC_BATCH, CONV_WIDTH - 1, D_CONV), 1.0),
        "state_ret": nrm(ks[5], (DEPTH, DEC_BATCH, N_RET_HEADS, RET_DK, RET_DV), 1.0),
        "ada_w": nrm(ks[6], (DEPTH, D_MODEL, 3 * D_MODEL), 0.5 * D_MODEL ** -0.5),
        "ada_b": nrm(ks[7], (DEPTH, 3 * D_MODEL), 0.01),
        "norm_pre": 1.0 + nrm(ks[8], (DEPTH, D_MODEL), 0.02),
        "norm_post": 1.0 + nrm(ks[9], (DEPTH, D_MODEL), 0.02),
        "w_in": nrm(ks[10], (DEPTH, D_MODEL, D_IN_TOTAL), D_MODEL ** -0.5),
        "conv_w": nrm(ks[11], (DEPTH, CONV_WIDTH, D_CONV), CONV_WIDTH ** -0.5),
        "conv_b": nrm(ks[12], (DEPTH, D_CONV), 0.01),
        "w_branch": nrm(ks[13], (DEPTH, 2, D_BRANCH, D_MODEL), D_BRANCH ** -0.5),
        "w_out": nrm(ks[14], (DEPTH, D_MODEL, D_MODEL), D_MODEL ** -0.5),
    }


def reference(x_prompt, x_sample, c_prompt, c_sample, state_conv, state_ret,
              ada_w, ada_b, norm_pre, norm_post, w_in, conv_w, conv_b, w_branch, w_out):
    bp, tp = x_prompt.shape[0], x_prompt.shape[1]
    ts = x_sample.shape[1]
    pos_p = jnp.arange(tp)
    pos_s = PAST_LEN + jnp.arange(ts)
    hp, hs = x_prompt, x_sample
    conv_p_l, ret_p_l, conv_s_l, ret_s_l = [], [], [], []
    for l in range(DEPTH):
        hist0 = jnp.zeros((bp, CONV_WIDTH - 1, D_CONV), x_prompt.dtype)
        ret0 = jnp.zeros((bp, N_RET_HEADS, RET_DK, RET_DV), jnp.float32)
        hp, cp, rp = mixer_layer(hp, c_prompt, hist0, ret0, pos_p, ada_w[l], ada_b[l],
                                 norm_pre[l], norm_post[l], w_in[l], conv_w[l], conv_b[l],
                                 w_branch[l], w_out[l])
        hs, cs, rs = mixer_layer(hs, c_sample, state_conv[l], state_ret[l], pos_s, ada_w[l], ada_b[l],
                                 norm_pre[l], norm_post[l], w_in[l], conv_w[l], conv_b[l],
                                 w_branch[l], w_out[l])
        conv_p_l.append(cp)
        ret_p_l.append(rp)
        conv_s_l.append(cs)
        ret_s_l.append(rs)
    new_conv_prompt = jnp.stack(conv_p_l)
    new_ret_prompt = jnp.stack(ret_p_l)
    new_conv_sample = jnp.stack(conv_s_l)
    new_ret_sample = jnp.stack(ret_s_l)
    return (hp, hs, new_conv_prompt, new_ret_prompt, new_conv_sample, new_ret_sample)
```

```python
import functools

import jax
import jax.numpy as jnp
from jax import lax
from jax.experimental import pallas as pl
from jax.experimental.pallas import tpu as pltpu

F32 = jnp.float32
BF16 = jnp.bfloat16

EPS = 1e-6
ROPE_BASE = 10000.0
REF_CHUNK = 64
N_HEADS = 8
CONV_WIDTH = 3

LANES = 128
SUBLANES = 8
COL_BLOCK = 256
ROW_CHUNK = 256
RET_TILE = 256
VMEM_LIMIT = 56 * 1024 * 1024


def _dot(a, b):
    return jnp.dot(a, b, preferred_element_type=F32)


def _sigmoid(x):
    return 1.0 / (1.0 + jnp.exp(-x))


def _silu(x):
    return x * _sigmoid(x)


def _params(semantics):
    return pltpu.CompilerParams(dimension_semantics=semantics, vmem_limit_bytes=VMEM_LIMIT)


def _mod_body(c_ref, w_ref, b_ref, o_ref):
    a = _silu(c_ref[...]).astype(BF16)
    o_ref[...] = _dot(a, w_ref[...].astype(BF16)) + b_ref[...]


def _modulation(c_all, ada_w, ada_b):
    nseq, d = c_all.shape
    n_out = ada_w.shape[1]
    nb = 768
    return pl.pallas_call(
        _mod_body,
        grid=(n_out // nb,),
        in_specs=[pl.BlockSpec((nseq, d), lambda j: (0, 0)),
                  pl.BlockSpec((d, nb), lambda j: (0, j)),
                  pl.BlockSpec((1, nb), lambda j: (0, j))],
        out_specs=pl.BlockSpec((nseq, nb), lambda j: (0, j)),
        out_shape=jax.ShapeDtypeStruct((nseq, n_out), F32),
        compiler_params=_params(("arbitrary",)),
        name="modulation",
    )(c_all, ada_w, ada_b.reshape(1, n_out))


def _prenorm_body(x_ref, mod_ref, g_ref, h_ref):
    x = x_ref[...]
    ms = jnp.mean(x * x, axis=-1, keepdims=True)
    y = x * lax.rsqrt(ms + EPS) * g_ref[...]
    shift = mod_ref[:, 0:1, :]
    scale = mod_ref[:, 1:2, :]
    h_ref[...] = (y * (1.0 + scale) + shift).astype(BF16)


def _prenorm(x, mod3, g_pre, sb, tr):
    b, t, d = x.shape
    return pl.pallas_call(
        _prenorm_body,
        grid=(b // sb, t // tr),
        in_specs=[pl.BlockSpec((sb, tr, d), lambda i, j: (i, j, 0)),
                  pl.BlockSpec((sb, 3, d), lambda i, j: (i, 0, 0)),
                  pl.BlockSpec((1, d), lambda i, j: (0, 0))],
        out_specs=pl.BlockSpec((sb, tr, d), lambda i, j: (i, j, 0)),
        out_shape=jax.ShapeDtypeStruct((b, t, d), BF16),
        compiler_params=_params(("arbitrary", "arbitrary")),
        name="prenorm",
    )(x, mod3, g_pre.reshape(1, d))


def _conv_body(*refs, sb, ts, mc, has_hist):
    if has_hist:
        (h_ref, wb_ref, wc_ref, wu_ref, wz_ref, cw_ref, cb_ref, hist_ref,
         ya_ref, nh_ref, ubuf, pbuf) = refs
    else:
        (h_ref, wb_ref, wc_ref, wu_ref, wz_ref, cw_ref, cb_ref,
         ya_ref, nh_ref, ubuf, pbuf) = refs
        hist_ref = None
    ec = min(mc, ts)
    w0 = cw_ref[0:1, :]
    w1 = cw_ref[1:2, :]
    w2 = cw_ref[2:3, :]
    cb = cb_ref[...]
    top = SUBLANES

    def epilogue(gb, gc, gu, zc):
        u = gc * gu
        ubuf[top:top + ec, :] = u
        f0 = ubuf[top - 2:top - 2 + ec, :]
        f1 = ubuf[top - 1:top - 1 + ec, :]
        conv = cb + w0 * f0 + w1 * f1 + w2 * u
        ubuf[top - 2:top, :] = ubuf[top + ec - 2:top + ec, :]
        return (gb * conv * _silu(zc)).astype(BF16)

    def projections(hr):
        return (_dot(hr, wb_ref[...]), _dot(hr, wc_ref[...]),
                _dot(hr, wu_ref[...]), _dot(hr, wz_ref[...]))

    def start_sequence(s):
        if has_hist:
            ubuf[top - 2:top, :] = hist_ref[s]
        else:
            ubuf[top - 2:top, :] = jnp.zeros((2, ubuf.shape[1]), F32)

    if mc <= ts:
        for s in range(sb):
            start_sequence(s)

            def chunk(i, carry, s=s):
                r0 = pl.multiple_of(i * mc, mc)
                ya_ref[s, pl.ds(r0, mc), :] = epilogue(*projections(h_ref[s, pl.ds(r0, mc), :]))
                return carry

            lax.fori_loop(0, ts // mc, chunk, 0)
            nh_ref[s] = ubuf[top - 2:top, :]
    else:
        hr = h_ref[...].reshape(sb * ts, h_ref.shape[2])
        for k, p in enumerate(projections(hr)):
            pbuf[k] = p
        for s in range(sb):
            start_sequence(s)
            rows = slice(s * ts, (s + 1) * ts)
            ya_ref[s] = epilogue(pbuf[0, rows, :], pbuf[1, rows, :], pbuf[2, rows, :], pbuf[3, rows, :])
            nh_ref[s] = ubuf[top - 2:top, :]


def _conv_branch(h, w_in, conv_w, conv_b, hist, sb, mc):
    b, t, d = h.shape
    cw = COL_BLOCK
    ng = d // cw
    has_hist = hist is not None
    ec = min(mc, t)
    in_specs = [pl.BlockSpec((sb, t, d), lambda i, g: (i, 0, 0))]
    in_specs += [pl.BlockSpec((d, cw), functools.partial(lambda i, g, k: (0, k * ng + g), k=k))
                 for k in range(4)]
    in_specs += [pl.BlockSpec((CONV_WIDTH, cw), lambda i, g: (0, g)),
                 pl.BlockSpec((1, cw), lambda i, g: (0, g))]
    args = [h, w_in, w_in, w_in, w_in, conv_w, conv_b.reshape(1, d)]
    if has_hist:
        in_specs.append(pl.BlockSpec((sb, CONV_WIDTH - 1, cw), lambda i, g: (i, 0, g)))
        args.append(hist)
    pbuf_rows = sb * t if mc > t else SUBLANES
    return pl.pallas_call(
        functools.partial(_conv_body, sb=sb, ts=t, mc=mc, has_hist=has_hist),
        grid=(b // sb, ng),
        in_specs=in_specs,
        out_specs=[pl.BlockSpec((sb, t, cw), lambda i, g: (i, 0, g)),
                   pl.BlockSpec((sb, CONV_WIDTH - 1, cw), lambda i, g: (i, 0, g))],
        out_shape=[jax.ShapeDtypeStruct((b, t, d), BF16),
                   jax.ShapeDtypeStruct((b, CONV_WIDTH - 1, d), F32)],
        scratch_shapes=[pltpu.VMEM((ec + SUBLANES, cw), F32),
                        pltpu.VMEM((4, pbuf_rows, cw), F32)],
        compiler_params=_params(("arbitrary", "arbitrary")),
        name="conv_branch",
    )(*args)


def _ret_body(*refs, sb, ts, mc, tt, has_state):
    if has_state:
        (h_ref, wq_ref, wk_ref, wv_ref, wz_ref, rq_ref, rk_ref, mask_ref, qd_ref, kd_ref, bd_ref,
         st_ref, yb_ref, ns_ref, pbuf) = refs
    else:
        (h_ref, wq_ref, wk_ref, wv_ref, wz_ref, rq_ref, rk_ref, mask_ref, qd_ref, kd_ref, bd_ref,
         yb_ref, ns_ref, pbuf) = refs
        st_ref = None
    dk = LANES
    dv = 2 * LANES
    hpb = 2

    def tile(s, r0, q2, k2, v2, z2, cq, sq, ck, sk):
        for hh in range(hpb):
            q = q2[:, hh * dk:(hh + 1) * dk]
            k = k2[:, hh * dk:(hh + 1) * dk]
            v = v2[:, hh * dv:(hh + 1) * dv].astype(BF16)
            z = z2[:, hh * dv:(hh + 1) * dv]
            qr = q * cq + pltpu.roll(q, dk // 2, axis=1) * sq
            kr = k * ck + pltpu.roll(k, dk // 2, axis=1) * sk
            state = ns_ref[s, hh]
            scores = lax.dot_general(qr.astype(BF16), kr.astype(BF16),
                                     (((1,), (1,)), ((), ())), preferred_element_type=F32)
            p = (scores * mask_ref[hh]).astype(BF16)
            o = _dot(p, v) + _dot((qr * qd_ref[hh]).astype(BF16), state.astype(BF16))
            kv = lax.dot_general((kr * kd_ref[hh]).astype(BF16), v,
                                 (((0,), (0,)), ((), ())), preferred_element_type=F32)
            ns_ref[s, hh] = bd_ref[hh] * state + kv
            on = o * lax.rsqrt(jnp.mean(o * o, axis=-1, keepdims=True) + EPS)
            yb_ref[s, pl.ds(r0, tt), hh * dv:(hh + 1) * dv] = (on * _silu(z)).astype(BF16)

    def projections(hr):
        return (_dot(hr, wq_ref[...]), _dot(hr, wk_ref[...]),
                _dot(hr, wv_ref[...]), _dot(hr, wz_ref[...]))

    if has_state:
        ns_ref[...] = st_ref[...]
    else:
        ns_ref[...] = jnp.zeros(ns_ref.shape, F32)

    if mc <= ts:
        assert mc == tt
        for s in range(sb):
            def chunk(i, carry, s=s):
                r0 = pl.multiple_of(i * mc, mc)
                q2, k2, v2, z2 = projections(h_ref[s, pl.ds(r0, mc), :])
                rows = pl.ds(r0, mc)
                tile(s, r0, q2, k2, v2, z2, rq_ref[0, rows, :], rq_ref[1, rows, :],
                     rk_ref[0, rows, :], rk_ref[1, rows, :])
                return carry

            lax.fori_loop(0, ts // mc, chunk, 0)
    else:
        assert tt == ts
        hr = h_ref[...].reshape(sb * ts, h_ref.shape[2])
        for k, p in enumerate(projections(hr)):
            pbuf[k, :, 0:p.shape[1]] = p
        for s in range(sb):
            rows = slice(s * ts, (s + 1) * ts)
            tile(s, 0, pbuf[0, rows, 0:hpb * dk], pbuf[1, rows, 0:hpb * dk],
                 pbuf[2, rows, :], pbuf[3, rows, :],
                 rq_ref[0], rq_ref[1], rk_ref[0], rk_ref[1])


def _retention_tables(t, tt, pos0, chunk):
    dk = LANES
    inv = ROPE_BASE ** (-jnp.arange(0, dk, 2, dtype=F32) / dk)
    ang = (pos0 + jnp.arange(t)).astype(F32)[:, None] * inv[None, :]
    cos, sin = jnp.cos(ang), jnp.sin(ang)
    rope_q = jnp.stack([jnp.concatenate([cos, cos], -1), jnp.concatenate([-sin, sin], -1)])
    rope_k = rope_q * (dk ** -0.5)
    lg = jnp.log(1.0 - 2.0 ** (-5.0 - jnp.arange(N_HEADS, dtype=F32)))
    idx = jnp.arange(tt, dtype=F32)
    visible = (jnp.arange(tt)[None, :] // chunk) <= (jnp.arange(tt)[:, None] // chunk)
    mask = jnp.exp(lg[:, None, None] * jnp.abs(idx[:, None] - idx[None, :])) * visible.astype(F32)
    q_dec = jnp.broadcast_to(jnp.exp(lg[:, None] * (idx + 1.0))[..., None], (N_HEADS, tt, dk))
    k_dec = jnp.broadcast_to(jnp.exp(lg[:, None] * (tt - 1.0 - idx))[..., None], (N_HEADS, tt, dk))
    b_dec = jnp.broadcast_to(jnp.exp(lg * tt)[:, None, None], (N_HEADS, 1, 2 * dk))
    return rope_q, rope_k, mask, q_dec, k_dec, b_dec


def _ret_branch(h, w_in, state, pos0, sb, mc):
    b, t, d = h.shape
    dk, dv, hpb = LANES, 2 * LANES, 2
    nhp = N_HEADS // hpb
    tt = min(RET_TILE, t)
    chunk = min(REF_CHUNK, t)
    has_state = state is not None
    rope_q, rope_k, mask, q_dec, k_dec, b_dec = _retention_tables(t, tt, pos0, chunk)
    q_off = 4 * d // (hpb * dk)
    k_off = q_off + nhp
    v_off = (4 * d + 2 * N_HEADS * dk) // (hpb * dv)
    z_off = v_off + nhp
    in_specs = [pl.BlockSpec((sb, t, d), lambda i, g: (i, 0, 0)),
                pl.BlockSpec((d, hpb * dk), lambda i, g: (0, q_off + g)),
                pl.BlockSpec((d, hpb * dk), lambda i, g: (0, k_off + g)),
                pl.BlockSpec((d, hpb * dv), lambda i, g: (0, v_off + g)),
                pl.BlockSpec((d, hpb * dv), lambda i, g: (0, z_off + g)),
                pl.BlockSpec((2, t, dk), lambda i, g: (0, 0, 0)),
                pl.BlockSpec((2, t, dk), lambda i, g: (0, 0, 0)),
                pl.BlockSpec((hpb, tt, tt), lambda i, g: (g, 0, 0)),
                pl.BlockSpec((hpb, tt, dk), lambda i, g: (g, 0, 0)),
                pl.BlockSpec((hpb, tt, dk), lambda i, g: (g, 0, 0)),
                pl.BlockSpec((hpb, 1, dv), lambda i, g: (g, 0, 0))]
    args = [h, w_in, w_in, w_in, w_in, rope_q, rope_k, mask, q_dec, k_dec, b_dec]
    if has_state:
        in_specs.append(pl.BlockSpec((sb, hpb, dk, dv), lambda i, g: (i, g, 0, 0)))
        args.append(state)
    pbuf_rows = sb * t if mc > t else SUBLANES
    return pl.pallas_call(
        functools.partial(_ret_body, sb=sb, ts=t, mc=mc, tt=tt, has_state=has_state),
        grid=(b // sb, nhp),
        in_specs=in_specs,
        out_specs=[pl.BlockSpec((sb, t, hpb * dv), lambda i, g: (i, 0, g)),
                   pl.BlockSpec((sb, hpb, dk, dv), lambda i, g: (i, g, 0, 0))],
        out_shape=[jax.ShapeDtypeStruct((b, t, N_HEADS * dv), BF16),
                   jax.ShapeDtypeStruct((b, N_HEADS, dk, dv), F32)],
        scratch_shapes=[pltpu.VMEM((4, pbuf_rows, hpb * dv), F32)],
        compiler_params=_params(("arbitrary", "arbitrary")),
        name="ret_branch",
    )(*args)


def _merge_body(h_ref, ya_ref, yb_ref, wma_ref, wmb_ref, wa_ref, wb_ref, o_ref, *, mc):
    def chunk(i, carry):
        r0 = pl.multiple_of(i * mc, mc)
        rows = pl.ds(r0, mc)
        hr = h_ref[rows, :]
        ga = _sigmoid(_dot(hr, wma_ref[...]))
        gb = _sigmoid(_dot(hr, wmb_ref[...]))
        a = _dot(ya_ref[rows, :], wa_ref[0])
        b = _dot(yb_ref[rows, :], wb_ref[0])
        o_ref[rows, :] = (ga * a + gb * b).astype(BF16)
        return carry

    lax.fori_loop(0, h_ref.shape[0] // mc, chunk, 0)


def _merge(h2, ya2, yb2, w_in, w_branch, tm, mc):
    n, d = h2.shape
    nb = 512
    nblk = d // nb
    ma_off = (w_in.shape[1] - 2 * d) // nb
    mb_off = ma_off + nblk
    act = pl.BlockSpec((tm, d), lambda i, j: (i, 0))
    return pl.pallas_call(
        functools.partial(_merge_body, mc=mc),
        grid=(n // tm, nblk),
        in_specs=[act, act, act,
                  pl.BlockSpec((d, nb), lambda i, j: (0, ma_off + j)),
                  pl.BlockSpec((d, nb), lambda i, j: (0, mb_off + j)),
                  pl.BlockSpec((1, d, nb), lambda i, j: (0, 0, j)),
                  pl.BlockSpec((1, d, nb), lambda i, j: (1, 0, j))],
        out_specs=pl.BlockSpec((tm, nb), lambda i, j: (i, j)),
        out_shape=jax.ShapeDtypeStruct((n, d), BF16),
        compiler_params=_params(("arbitrary", "arbitrary")),
        name="merge",
    )(h2, ya2, yb2, w_in, w_in, w_branch, w_branch)


def _out_body(m_ref, w_ref, x_ref, mod_ref, g_ref, y_ref, obuf, *, sb, ts, mc):
    g = g_ref[...]

    def finish(o, x, gate):
        ms = jnp.mean(o * o, axis=-1, keepdims=True)
        return x + gate * (o * lax.rsqrt(ms + EPS) * g)

    if mc <= ts:
        for s in range(sb):
            gate = mod_ref[s, 2:3, :]

            def chunk(i, carry, s=s, gate=gate):
                rows = pl.ds(pl.multiple_of(i * mc, mc), mc)
                o = _dot(m_ref[s, rows, :], w_ref[...])
                y_ref[s, rows, :] = finish(o, x_ref[s, rows, :], gate)
                return carry

            lax.fori_loop(0, ts // mc, chunk, 0)
    else:
        obuf[...] = _dot(m_ref[...].reshape(sb * ts, m_ref.shape[2]), w_ref[...])
        for s in range(sb):
            y_ref[s] = finish(obuf[s * ts:(s + 1) * ts, :], x_ref[s], mod_ref[s, 2:3, :])


def _out_proj(merged, w_out, x, mod3, g_post, sb, tr, mc):
    b, t, d = x.shape
    obuf_rows = sb * tr if mc > tr else SUBLANES
    act = lambda i, j: (i, j, 0)
    return pl.pallas_call(
        functools.partial(_out_body, sb=sb, ts=tr, mc=mc),
        grid=(b // sb, t // tr),
        in_specs=[pl.BlockSpec((sb, tr, d), act),
                  pl.BlockSpec((d, d), lambda i, j: (0, 0)),
                  pl.BlockSpec((sb, tr, d), act),
                  pl.BlockSpec((sb, 3, d), lambda i, j: (i, 0, 0)),
                  pl.BlockSpec((1, d), lambda i, j: (0, 0))],
        out_specs=pl.BlockSpec((sb, tr, d), act),
        out_shape=jax.ShapeDtypeStruct((b, t, d), F32),
        scratch_shapes=[pltpu.VMEM((obuf_rows, d), F32)],
        compiler_params=_params(("arbitrary", "arbitrary")),
        name="out_proj",
    )(merged, w_out, x, mod3, g_post.reshape(1, d))


def _mixer_layer(x, mod3, hist, state, pos0, g_pre, g_post, w_in, conv_w, conv_b, w_branch, w_out):
    b, t, d = x.shape
    long_seq = t >= ROW_CHUNK
    sb = 1 if long_seq else b
    tr = 512 if long_seq else t
    mc = ROW_CHUNK if long_seq else b * t
    h = _prenorm(x, mod3, g_pre, sb, tr)
    ya, new_hist = _conv_branch(h, w_in, conv_w, conv_b, hist, sb, mc)
    yb, new_state = _ret_branch(h, w_in, state, pos0, sb, mc)
    n = b * t
    tm = 1024 if long_seq else n
    merged = _merge(h.reshape(n, d), ya.reshape(n, d), yb.reshape(n, d), w_in, w_branch, tm, ROW_CHUNK)
    y = _out_proj(merged.reshape(b, t, d), w_out, x, mod3, g_post, sb, tr, mc)
    return y, new_hist, new_state


def kernel(x_prompt, x_sample, c_prompt, c_sample, state_conv, state_ret, ada_w, ada_b, norm_pre,
           norm_post, w_in, conv_w, conv_b, w_branch, w_out):
    depth = w_in.shape[0]
    bp = x_prompt.shape[0]
    d = x_prompt.shape[2]
    past_len = 1024
    hp, hs = x_prompt, x_sample
    conv_p, ret_p, conv_s, ret_s = [], [], [], []
    c_all = jnp.concatenate([c_prompt, c_sample], axis=0)
    for l in range(depth):
        mod3 = _modulation(c_all, ada_w[l], ada_b[l]).reshape(c_all.shape[0], 3, d)
        w_in_l = w_in[l].astype(BF16)
        w_branch_l = w_branch[l].astype(BF16)
        w_out_l = w_out[l].astype(BF16)
        shared = (norm_pre[l], norm_post[l], w_in_l, conv_w[l], conv_b[l], w_branch_l, w_out_l)
        hp, cp, rp = _mixer_layer(hp, mod3[:bp], None, None, 0, *shared)
        hs, cs, rs = _mixer_layer(hs, mod3[bp:], state_conv[l], state_ret[l], past_len, *shared)
        conv_p.append(cp)
        ret_p.append(rp)
        conv_s.append(cs)
        ret_s.append(rs)
    return (hp, hs, jnp.stack(conv_p), jnp.stack(ret_p), jnp.stack(conv_s), jnp.stack(ret_s))
```

```python
import functools

import jax
import jax.numpy as jnp
from jax import lax
from jax.experimental import pallas as pl
from jax.experimental.pallas import tpu as pltpu

F32 = jnp.float32
BF16 = jnp.bfloat16

EPS = 1e-6
ROPE_BASE = 10000.0
REF_CHUNK = 64
N_HEADS = 8
CONV_WIDTH = 3

LANES = 128
SUBLANES = 8
COL_BLOCK = 256
ROW_CHUNK = 256
RET_TILE = 256
CHUNK_UNROLL = True
PAST_LEN = 1024
VMEM_LIMIT = 56 * 1024 * 1024


def _dot(a, b):
    return jnp.dot(a, b, preferred_element_type=F32)


def _sigmoid(x):
    return 1.0 / (1.0 + jnp.exp(-x))


def _silu(x):
    return x * _sigmoid(x)


def _params(semantics):
    return pltpu.CompilerParams(dimension_semantics=semantics, vmem_limit_bytes=VMEM_LIMIT)


def _mod_body(c_ref, w_ref, b_ref, o_ref):
    a = _silu(c_ref[...]).astype(BF16)
    o_ref[...] = _dot(a, w_ref[...].astype(BF16)) + b_ref[...]


def _modulation(c_all, ada_w, ada_b):
    nseq, d = c_all.shape
    n_out = ada_w.shape[1]
    nb = 768
    return pl.pallas_call(
        _mod_body,
        grid=(n_out // nb,),
        in_specs=[pl.BlockSpec((nseq, d), lambda j: (0, 0)),
                  pl.BlockSpec((d, nb), lambda j: (0, j)),
                  pl.BlockSpec((1, nb), lambda j: (0, j))],
        out_specs=pl.BlockSpec((nseq, nb), lambda j: (0, j)),
        out_shape=jax.ShapeDtypeStruct((nseq, n_out), F32),
        compiler_params=_params(("arbitrary",)),
        name="modulation",
    )(c_all, ada_w, ada_b.reshape(1, n_out))


def _prenorm_body(x_ref, mod_ref, g_ref, h_ref):
    x = x_ref[...]
    ms = jnp.mean(x * x, axis=-1, keepdims=True)
    y = x * lax.rsqrt(ms + EPS) * g_ref[...]
    shift = mod_ref[:, 0:1, :]
    scale = mod_ref[:, 1:2, :]
    h_ref[...] = (y * (1.0 + scale) + shift).astype(BF16)


def _prenorm(x, mod3, g_pre, sb, tr):
    b, t, d = x.shape
    return pl.pallas_call(
        _prenorm_body,
        grid=(b // sb, t // tr),
        in_specs=[pl.BlockSpec((sb, tr, d), lambda i, j: (i, j, 0)),
                  pl.BlockSpec((sb, 3, d), lambda i, j: (i, 0, 0)),
                  pl.BlockSpec((1, d), lambda i, j: (0, 0))],
        out_specs=pl.BlockSpec((sb, tr, d), lambda i, j: (i, j, 0)),
        out_shape=jax.ShapeDtypeStruct((b, t, d), BF16),
        compiler_params=_params(("arbitrary", "arbitrary")),
        name="prenorm",
    )(x, mod3, g_pre.reshape(1, d))


def _conv_body(*refs, sb, ts, mc, has_hist):
    if has_hist:
        (h_ref, wb_ref, wc_ref, wu_ref, wz_ref, cw_ref, cb_ref, hist_ref,
         ya_ref, nh_ref, ubuf, pbuf) = refs
    else:
        (h_ref, wb_ref, wc_ref, wu_ref, wz_ref, cw_ref, cb_ref,
         ya_ref, nh_ref, ubuf, pbuf) = refs
        hist_ref = None
    ec = min(mc, ts)
    w0 = cw_ref[0:1, :]
    w1 = cw_ref[1:2, :]
    w2 = cw_ref[2:3, :]
    cb = cb_ref[...]
    top = SUBLANES

    def epilogue(gb, gc, gu, zc):
        u = gc * gu
        ubuf[top:top + ec, :] = u
        f0 = ubuf[top - 2:top - 2 + ec, :]
        f1 = ubuf[top - 1:top - 1 + ec, :]
        conv = cb + w0 * f0 + w1 * f1 + w2 * u
        ubuf[top - 2:top, :] = ubuf[top + ec - 2:top + ec, :]
        return (gb * conv * _silu(zc)).astype(BF16)

    def projections(hr):
        return (_dot(hr, wb_ref[...]), _dot(hr, wc_ref[...]),
                _dot(hr, wu_ref[...]), _dot(hr, wz_ref[...]))

    def start_sequence(s):
        if has_hist:
            ubuf[top - 2:top, :] = hist_ref[s]
        else:
            ubuf[top - 2:top, :] = jnp.zeros((2, ubuf.shape[1]), F32)

    if mc <= ts:
        for s in range(sb):
            start_sequence(s)

            def chunk(i, carry, s=s):
                r0 = pl.multiple_of(i * mc, mc)
                ya_ref[s, pl.ds(r0, mc), :] = epilogue(*projections(h_ref[s, pl.ds(r0, mc), :]))
                return carry

            lax.fori_loop(0, ts // mc, chunk, 0, unroll=CHUNK_UNROLL)
            nh_ref[s] = ubuf[top - 2:top, :]
    else:
        hr = h_ref[...].reshape(sb * ts, h_ref.shape[2])
        for k, p in enumerate(projections(hr)):
            pbuf[k] = p
        for s in range(sb):
            start_sequence(s)
            rows = slice(s * ts, (s + 1) * ts)
            ya_ref[s] = epilogue(pbuf[0, rows, :], pbuf[1, rows, :], pbuf[2, rows, :], pbuf[3, rows, :])
            nh_ref[s] = ubuf[top - 2:top, :]


def _conv_branch(h, w_in, conv_w, conv_b, hist, sb, mc):
    b, t, d = h.shape
    cw = COL_BLOCK
    ng = d // cw
    has_hist = hist is not None
    ec = min(mc, t)
    in_specs = [pl.BlockSpec((sb, t, d), lambda i, g: (i, 0, 0))]
    in_specs += [pl.BlockSpec((d, cw), functools.partial(lambda i, g, k: (0, k * ng + g), k=k))
                 for k in range(4)]
    in_specs += [pl.BlockSpec((CONV_WIDTH, cw), lambda i, g: (0, g)),
                 pl.BlockSpec((1, cw), lambda i, g: (0, g))]
    args = [h, w_in, w_in, w_in, w_in, conv_w, conv_b.reshape(1, d)]
    if has_hist:
        in_specs.append(pl.BlockSpec((sb, CONV_WIDTH - 1, cw), lambda i, g: (i, 0, g)))
        args.append(hist)
    pbuf_rows = sb * t if mc > t else SUBLANES
    return pl.pallas_call(
        functools.partial(_conv_body, sb=sb, ts=t, mc=mc, has_hist=has_hist),
        grid=(b // sb, ng),
        in_specs=in_specs,
        out_specs=[pl.BlockSpec((sb, t, cw), lambda i, g: (i, 0, g)),
                   pl.BlockSpec((sb, CONV_WIDTH - 1, cw), lambda i, g: (i, 0, g))],
        out_shape=[jax.ShapeDtypeStruct((b, t, d), BF16),
                   jax.ShapeDtypeStruct((b, CONV_WIDTH - 1, d), F32)],
        scratch_shapes=[pltpu.VMEM((ec + SUBLANES, cw), F32),
                        pltpu.VMEM((4, pbuf_rows, cw), F32)],
        compiler_params=_params(("arbitrary", "arbitrary")),
        name="conv_branch",
    )(*args)


def _ret_body(*refs, sb, ts, mc, tt, has_state):
    if has_state:
        (h_ref, wq_ref, wk_ref, wv_ref, wz_ref, rq_ref, rk_ref, mask_ref, qd_ref, kd_ref, bd_ref,
         st_ref, yb_ref, ns_ref, pbuf) = refs
    else:
        (h_ref, wq_ref, wk_ref, wv_ref, wz_ref, rq_ref, rk_ref, mask_ref, qd_ref, kd_ref, bd_ref,
         yb_ref, ns_ref, pbuf) = refs
        st_ref = None
    dk = LANES
    dv = 2 * LANES
    hpb = 2

    def tile(s, r0, q2, k2, v2, z2, cq, sq, ck, sk):
        for hh in range(hpb):
            q = q2[:, hh * dk:(hh + 1) * dk]
            k = k2[:, hh * dk:(hh + 1) * dk]
            v = v2[:, hh * dv:(hh + 1) * dv].astype(BF16)
            z = z2[:, hh * dv:(hh + 1) * dv]
            qr = q * cq + pltpu.roll(q, dk // 2, axis=1) * sq
            kr = k * ck + pltpu.roll(k, dk // 2, axis=1) * sk
            state = ns_ref[s, hh]
            scores = lax.dot_general(qr.astype(BF16), kr.astype(BF16),
                                     (((1,), (1,)), ((), ())), preferred_element_type=F32)
            p = (scores * mask_ref[hh]).astype(BF16)
            o = _dot(p, v) + _dot((qr * qd_ref[hh]).astype(BF16), state.astype(BF16))
            kv = lax.dot_general((kr * kd_ref[hh]).astype(BF16), v,
                                 (((0,), (0,)), ((), ())), preferred_element_type=F32)
            ns_ref[s, hh] = bd_ref[hh] * state + kv
            on = o * lax.rsqrt(jnp.mean(o * o, axis=-1, keepdims=True) + EPS)
            yb_ref[s, pl.ds(r0, tt), hh * dv:(hh + 1) * dv] = (on * _silu(z)).astype(BF16)

    def projections(hr):
        return (_dot(hr, wq_ref[...]), _dot(hr, wk_ref[...]),
                _dot(hr, wv_ref[...]), _dot(hr, wz_ref[...]))

    if has_state:
        ns_ref[...] = st_ref[...]
    else:
        ns_ref[...] = jnp.zeros(ns_ref.shape, F32)

    if mc <= ts:
        assert mc == tt
        for s in range(sb):
            def chunk(i, carry, s=s):
                r0 = pl.multiple_of(i * mc, mc)
                q2, k2, v2, z2 = projections(h_ref[s, pl.ds(r0, mc), :])
                rows = pl.ds(r0, mc)
                tile(s, r0, q2, k2, v2, z2, rq_ref[0, rows, :], rq_ref[1, rows, :],
                     rk_ref[0, rows, :], rk_ref[1, rows, :])
                return carry

            lax.fori_loop(0, ts // mc, chunk, 0, unroll=CHUNK_UNROLL)
    else:
        assert tt == ts
        hr = h_ref[...].reshape(sb * ts, h_ref.shape[2])
        for k, p in enumerate(projections(hr)):
            pbuf[k, :, 0:p.shape[1]] = p
        for s in range(sb):
            rows = slice(s * ts, (s + 1) * ts)
            tile(s, 0, pbuf[0, rows, 0:hpb * dk], pbuf[1, rows, 0:hpb * dk],
                 pbuf[2, rows, :], pbuf[3, rows, :],
                 rq_ref[0], rq_ref[1], rk_ref[0], rk_ref[1])


def _retention_tables(t, tt, pos0, chunk):
    dk = LANES
    inv = ROPE_BASE ** (-jnp.arange(0, dk, 2, dtype=F32) / dk)
    ang = (pos0 + jnp.arange(t)).astype(F32)[:, None] * inv[None, :]
    cos, sin = jnp.cos(ang), jnp.sin(ang)
    rope_q = jnp.stack([jnp.concatenate([cos, cos], -1), jnp.concatenate([-sin, sin], -1)])
    rope_k = rope_q * (dk ** -0.5)
    lg = jnp.log(1.0 - 2.0 ** (-5.0 - jnp.arange(N_HEADS, dtype=F32)))
    idx = jnp.arange(tt, dtype=F32)
    visible = (jnp.arange(tt)[None, :] // chunk) <= (jnp.arange(tt)[:, None] // chunk)
    mask = jnp.exp(lg[:, None, None] * jnp.abs(idx[:, None] - idx[None, :])) * visible.astype(F32)
    q_dec = jnp.broadcast_to(jnp.exp(lg[:, None] * (idx + 1.0))[..., None], (N_HEADS, tt, dk))
    k_dec = jnp.broadcast_to(jnp.exp(lg[:, None] * (tt - 1.0 - idx))[..., None], (N_HEADS, tt, dk))
    b_dec = jnp.broadcast_to(jnp.exp(lg * tt)[:, None, None], (N_HEADS, 1, 2 * dk))
    return rope_q, rope_k, mask, q_dec, k_dec, b_dec


def _ret_branch(h, w_in, state, pos0, sb, mc):
    b, t, d = h.shape
    dk, dv, hpb = LANES, 2 * LANES, 2
    nhp = N_HEADS // hpb
    tt = min(RET_TILE, t)
    chunk = min(REF_CHUNK, t)
    has_state = state is not None
    rope_q, rope_k, mask, q_dec, k_dec, b_dec = _retention_tables(t, tt, pos0, chunk)
    q_off = 4 * d // (hpb * dk)
    k_off = q_off + nhp
    v_off = (4 * d + 2 * N_HEADS * dk) // (hpb * dv)
    z_off = v_off + nhp
    in_specs = [pl.BlockSpec((sb, t, d), lambda i, g: (i, 0, 0)),
                pl.BlockSpec((d, hpb * dk), lambda i, g: (0, q_off + g)),
                pl.BlockSpec((d, hpb * dk), lambda i, g: (0, k_off + g)),
                pl.BlockSpec((d, hpb * dv), lambda i, g: (0, v_off + g)),
                pl.BlockSpec((d, hpb * dv), lambda i, g: (0, z_off + g)),
                pl.BlockSpec((2, t, dk), lambda i, g: (0, 0, 0)),
                pl.BlockSpec((2, t, dk), lambda i, g: (0, 0, 0)),
                pl.BlockSpec((hpb, tt, tt), lambda i, g: (g, 0, 0)),
                pl.BlockSpec((hpb, tt, dk), lambda i, g: (g, 0, 0)),
                pl.BlockSpec((hpb, tt, dk), lambda i, g: (g, 0, 0)),
                pl.BlockSpec((hpb, 1, dv), lambda i, g: (g, 0, 0))]
    args = [h, w_in, w_in, w_in, w_in, rope_q, rope_k, mask, q_dec, k_dec, b_dec]
    if has_state:
        in_specs.append(pl.BlockSpec((sb, hpb, dk, dv), lambda i, g: (i, g, 0, 0)))
        args.append(state)
    pbuf_rows = sb * t if mc > t else SUBLANES
    return pl.pallas_call(
        functools.partial(_ret_body, sb=sb, ts=t, mc=mc, tt=tt, has_state=has_state),
        grid=(b // sb, nhp),
        in_specs=in_specs,
        out_specs=[pl.BlockSpec((sb, t, hpb * dv), lambda i, g: (i, 0, g)),
                   pl.BlockSpec((sb, hpb, dk, dv), lambda i, g: (i, g, 0, 0))],
        out_shape=[jax.ShapeDtypeStruct((b, t, N_HEADS * dv), BF16),
                   jax.ShapeDtypeStruct((b, N_HEADS, dk, dv), F32)],
        scratch_shapes=[pltpu.VMEM((4, pbuf_rows, hpb * dv), F32)],
        compiler_params=_params(("arbitrary", "arbitrary")),
        name="ret_branch",
    )(*args)


def _merge_body(h_ref, ya_ref, yb_ref, wma_ref, wmb_ref, wa_ref, wb_ref, o_ref, *, mc):
    def chunk(i, carry):
        r0 = pl.multiple_of(i * mc, mc)
        rows = pl.ds(r0, mc)
        hr = h_ref[rows, :]
        ga = _sigmoid(_dot(hr, wma_ref[...]))
        gb = _sigmoid(_dot(hr, wmb_ref[...]))
        a = _dot(ya_ref[rows, :], wa_ref[0])
        b = _dot(yb_ref[rows, :], wb_ref[0])
        o_ref[rows, :] = (ga * a + gb * b).astype(BF16)
        return carry

    lax.fori_loop(0, h_ref.shape[0] // mc, chunk, 0, unroll=CHUNK_UNROLL)


def _merge(h2, ya2, yb2, w_in, w_branch, tm, mc):
    n, d = h2.shape
    nb = 512
    nblk = d // nb
    ma_off = (w_in.shape[1] - 2 * d) // nb
    mb_off = ma_off + nblk
    act = pl.BlockSpec((tm, d), lambda i, j: (i, 0))
    return pl.pallas_call(
        functools.partial(_merge_body, mc=mc),
        grid=(n // tm, nblk),
        in_specs=[act, act, act,
                  pl.BlockSpec((d, nb), lambda i, j: (0, ma_off + j)),
                  pl.BlockSpec((d, nb), lambda i, j: (0, mb_off + j)),
                  pl.BlockSpec((1, d, nb), lambda i, j: (0, 0, j)),
                  pl.BlockSpec((1, d, nb), lambda i, j: (1, 0, j))],
        out_specs=pl.BlockSpec((tm, nb), lambda i, j: (i, j)),
        out_shape=jax.ShapeDtypeStruct((n, d), BF16),
        compiler_params=_params(("arbitrary", "arbitrary")),
        name="merge",
    )(h2, ya2, yb2, w_in, w_in, w_branch, w_branch)


def _out_body(m_ref, w_ref, x_ref, mod_ref, g_ref, y_ref, obuf, *, sb, ts, mc):
    g = g_ref[...]

    def finish(o, x, gate):
        ms = jnp.mean(o * o, axis=-1, keepdims=True)
        return x + gate * (o * lax.rsqrt(ms + EPS) * g)

    if mc <= ts:
        for s in range(sb):
            gate = mod_ref[s, 2:3, :]

            def chunk(i, carry, s=s, gate=gate):
                rows = pl.ds(pl.multiple_of(i * mc, mc), mc)
                o = _dot(m_ref[s, rows, :], w_ref[...])
                y_ref[s, rows, :] = finish(o, x_ref[s, rows, :], gate)
                return carry

            lax.fori_loop(0, ts // mc, chunk, 0, unroll=CHUNK_UNROLL)
    else:
        obuf[...] = _dot(m_ref[...].reshape(sb * ts, m_ref.shape[2]), w_ref[...])
        for s in range(sb):
            y_ref[s] = finish(obuf[s * ts:(s + 1) * ts, :], x_ref[s], mod_ref[s, 2:3, :])


def _out_proj(merged, w_out, x, mod3, g_post, sb, tr, mc):
    b, t, d = x.shape
    obuf_rows = sb * tr if mc > tr else SUBLANES
    act = lambda i, j: (i, j, 0)
    return pl.pallas_call(
        functools.partial(_out_body, sb=sb, ts=tr, mc=mc),
        grid=(b // sb, t // tr),
        in_specs=[pl.BlockSpec((sb, tr, d), act),
                  pl.BlockSpec((d, d), lambda i, j: (0, 0)),
                  pl.BlockSpec((sb, tr, d), act),
                  pl.BlockSpec((sb, 3, d), lambda i, j: (i, 0, 0)),
                  pl.BlockSpec((1, d), lambda i, j: (0, 0))],
        out_specs=pl.BlockSpec((sb, tr, d), act),
        out_shape=jax.ShapeDtypeStruct((b, t, d), F32),
        scratch_shapes=[pltpu.VMEM((obuf_rows, d), F32)],
        compiler_params=_params(("arbitrary", "arbitrary")),
        name="out_proj",
    )(merged, w_out, x, mod3, g_post.reshape(1, d))


def _mixer_layer(x, mod3, hist, state, pos0, g_pre, g_post, w_in, conv_w, conv_b, w_branch, w_out):
    b, t, d = x.shape
    long_seq = t >= ROW_CHUNK
    sb = 1 if long_seq else b
    tr = 512 if long_seq else t
    mc = ROW_CHUNK if long_seq else b * t
    h = _prenorm(x, mod3, g_pre, sb, tr)
    ya, new_hist = _conv_branch(h, w_in, conv_w, conv_b, hist, sb, mc)
    yb, new_state = _ret_branch(h, w_in, state, pos0, sb, mc)
    n = b * t
    tm = 1024 if long_seq else n
    merged = _merge(h.reshape(n, d), ya.reshape(n, d), yb.reshape(n, d), w_in, w_branch, tm, ROW_CHUNK)
    y = _out_proj(merged.reshape(b, t, d), w_out, x, mod3, g_post, sb, tr, mc)
    return y, new_hist, new_state


def kernel(x_prompt, x_sample, c_prompt, c_sample, state_conv, state_ret, ada_w, ada_b, norm_pre,
           norm_post, w_in, conv_w, conv_b, w_branch, w_out):
    depth = w_in.shape[0]
    bp = x_prompt.shape[0]
    d = x_prompt.shape[2]
    hp, hs = x_prompt, x_sample
    conv_p, ret_p, conv_s, ret_s = [], [], [], []
    c_all = jnp.concatenate([c_prompt, c_sample], axis=0)
    for l in range(depth):
        mod3 = _modulation(c_all, ada_w[l], ada_b[l]).reshape(c_all.shape[0], 3, d)
        w_in_l = w_in[l].astype(BF16)
        w_branch_l = w_branch[l].astype(BF16)
        w_out_l = w_out[l].astype(BF16)
        shared = (norm_pre[l], norm_post[l], w_in_l, conv_w[l], conv_b[l], w_branch_l, w_out_l)
        hp, cp, rp = _mixer_layer(hp, mod3[:bp], None, None, 0, *shared)
        hs, cs, rs = _mixer_layer(hs, mod3[bp:], state_conv[l], state_ret[l], PAST_LEN, *shared)
        conv_p.append(cp)
        ret_p.append(rp)
        conv_s.append(cs)
        ret_s.append(rs)
    return (hp, hs, jnp.stack(conv_p), jnp.stack(ret_p), jnp.stack(conv_s), jnp.stack(ret_s))
```

```python
import functools

import jax
import jax.numpy as jnp
from jax import lax
from jax.experimental import pallas as pl
from jax.experimental.pallas import tpu as pltpu

F32 = jnp.float32
BF16 = jnp.bfloat16

EPS = 1e-6
ROPE_BASE = 10000.0
REF_CHUNK = 64
N_HEADS = 8
CONV_WIDTH = 3

LANES = 128
SUBLANES = 8
COL_BLOCK = 256
ROW_CHUNK = 256
RET_TILE = 256
CHUNK_UNROLL = True
PAST_LEN = 1024
VMEM_LIMIT = 56 * 1024 * 1024
CONV_ROWS_VMEM_LIMIT = 60 * 1024 * 1024


def _dot(a, b):
    return jnp.dot(a, b, preferred_element_type=F32)


def _sigmoid(x):
    return 1.0 / (1.0 + jnp.exp(-x))


def _silu(x):
    return x * _sigmoid(x)


def _params(semantics):
    return pltpu.CompilerParams(dimension_semantics=semantics, vmem_limit_bytes=VMEM_LIMIT)


def _mod_body(c_ref, w_ref, b_ref, o_ref):
    a = _silu(c_ref[...]).astype(BF16)
    o_ref[...] = _dot(a, w_ref[...].astype(BF16)) + b_ref[...]


def _modulation(c_all, ada_w, ada_b):
    nseq, d = c_all.shape
    n_out = ada_w.shape[1]
    nb = 768
    return pl.pallas_call(
        _mod_body,
        grid=(n_out // nb,),
        in_specs=[pl.BlockSpec((nseq, d), lambda j: (0, 0)),
                  pl.BlockSpec((d, nb), lambda j: (0, j)),
                  pl.BlockSpec((1, nb), lambda j: (0, j))],
        out_specs=pl.BlockSpec((nseq, nb), lambda j: (0, j)),
        out_shape=jax.ShapeDtypeStruct((nseq, n_out), F32),
        compiler_params=_params(("arbitrary",)),
        name="modulation",
    )(c_all, ada_w, ada_b.reshape(1, n_out))


def _prenorm_body(x_ref, mod_ref, g_ref, h_ref):
    x = x_ref[...]
    ms = jnp.mean(x * x, axis=-1, keepdims=True)
    y = x * lax.rsqrt(ms + EPS) * g_ref[...]
    shift = mod_ref[:, 0:1, :]
    scale = mod_ref[:, 1:2, :]
    h_ref[...] = (y * (1.0 + scale) + shift).astype(BF16)


def _prenorm(x, mod3, g_pre, sb, tr):
    b, t, d = x.shape
    return pl.pallas_call(
        _prenorm_body,
        grid=(b // sb, t // tr),
        in_specs=[pl.BlockSpec((sb, tr, d), lambda i, j: (i, j, 0)),
                  pl.BlockSpec((sb, 3, d), lambda i, j: (i, 0, 0)),
                  pl.BlockSpec((1, d), lambda i, j: (0, 0))],
        out_specs=pl.BlockSpec((sb, tr, d), lambda i, j: (i, j, 0)),
        out_shape=jax.ShapeDtypeStruct((b, t, d), BF16),
        compiler_params=_params(("arbitrary", "arbitrary")),
        name="prenorm",
    )(x, mod3, g_pre.reshape(1, d))


def _conv_epilogue(ubuf, ec, taps, gb, gc, gu, zc):
    w0, w1, w2, cb = taps
    top = SUBLANES
    u = gc * gu
    ubuf[top:top + ec, :] = u
    f0 = ubuf[top - 2:top - 2 + ec, :]
    f1 = ubuf[top - 1:top - 1 + ec, :]
    conv = cb + w0 * f0 + w1 * f1 + w2 * u
    ubuf[top - 2:top, :] = ubuf[top + ec - 2:top + ec, :]
    return (gb * conv * _silu(zc)).astype(BF16)


def _conv_rows_body(*refs, mc, has_hist):
    if has_hist:
        (x_ref, mod_ref, g_ref, wb_ref, wc_ref, wu_ref, wz_ref, cw_ref, cb_ref, hist_ref,
         h_ref, ya_ref, nh_ref, ubuf) = refs
    else:
        (x_ref, mod_ref, g_ref, wb_ref, wc_ref, wu_ref, wz_ref, cw_ref, cb_ref,
         h_ref, ya_ref, nh_ref, ubuf) = refs
    tr, d = x_ref.shape[1], x_ref.shape[2]
    cw = ubuf.shape[2]
    ng = d // cw
    top = SUBLANES
    r = pl.program_id(1)

    @pl.when(r == 0)
    def _():
        for gi in range(ng):
            cols = slice(gi * cw, (gi + 1) * cw)
            ubuf[gi, top - 2:top, :] = hist_ref[0, :, cols] if has_hist else jnp.zeros((2, cw), F32)

    g_pre = g_ref[...]
    shift = mod_ref[0, 0:1, :]
    scale1 = 1.0 + mod_ref[0, 1:2, :]
    for sub in range(tr // mc):
        rows = slice(sub * mc, (sub + 1) * mc)
        x = x_ref[0, rows, :]
        ms = jnp.mean(x * x, axis=-1, keepdims=True)
        h_ref[0, rows, :] = (x * lax.rsqrt(ms + EPS) * g_pre * scale1 + shift).astype(BF16)
        hr = h_ref[0, rows, :]
        for gi in range(ng):
            cols = slice(gi * cw, (gi + 1) * cw)
            taps = (cw_ref[0:1, cols], cw_ref[1:2, cols], cw_ref[2:3, cols], cb_ref[:, cols])
            ya_ref[0, rows, cols] = _conv_epilogue(
                ubuf.at[gi], mc, taps, _dot(hr, wb_ref[:, cols]), _dot(hr, wc_ref[:, cols]),
                _dot(hr, wu_ref[:, cols]), _dot(hr, wz_ref[:, cols]))

    @pl.when(r == pl.num_programs(1) - 1)
    def _():
        for gi in range(ng):
            nh_ref[0, :, gi * cw:(gi + 1) * cw] = ubuf[gi, top - 2:top, :]


def _conv_rows(x, mod3, g_pre, w_in, conv_w, conv_b, hist, tr, mc):
    b, t, d = x.shape
    cw = COL_BLOCK
    has_hist = hist is not None
    row_block = lambda i, r: (i, r, 0)
    per_seq = lambda i, r: (i, 0, 0)
    whole = lambda i, r: (0, 0)
    in_specs = [pl.BlockSpec((1, tr, d), row_block),
                pl.BlockSpec((1, 3, d), per_seq),
                pl.BlockSpec((1, d), whole)]
    in_specs += [pl.BlockSpec((d, d), functools.partial(lambda i, r, k: (0, k), k=k),
                              pipeline_mode=pl.Buffered(1)) for k in range(4)]
    in_specs += [pl.BlockSpec((CONV_WIDTH, d), whole), pl.BlockSpec((1, d), whole)]
    args = [x, mod3, g_pre.reshape(1, d), w_in, w_in, w_in, w_in, conv_w, conv_b.reshape(1, d)]
    if has_hist:
        in_specs.append(pl.BlockSpec((1, CONV_WIDTH - 1, d), per_seq))
        args.append(hist)
    return pl.pallas_call(
        functools.partial(_conv_rows_body, mc=mc, has_hist=has_hist),
        grid=(b, t // tr),
        in_specs=in_specs,
        out_specs=[pl.BlockSpec((1, tr, d), row_block),
                   pl.BlockSpec((1, tr, d), row_block),
                   pl.BlockSpec((1, CONV_WIDTH - 1, d), per_seq)],
        out_shape=[jax.ShapeDtypeStruct((b, t, d), BF16),
                   jax.ShapeDtypeStruct((b, t, d), BF16),
                   jax.ShapeDtypeStruct((b, CONV_WIDTH - 1, d), F32)],
        scratch_shapes=[pltpu.VMEM((d // cw, mc + SUBLANES, cw), F32)],
        compiler_params=pltpu.CompilerParams(dimension_semantics=("arbitrary", "arbitrary"),
                                             vmem_limit_bytes=CONV_ROWS_VMEM_LIMIT),
        name="conv_rows",
    )(*args)


def _conv_seqs_body(*refs, sb, ts, has_hist):
    if has_hist:
        (h_ref, wb_ref, wc_ref, wu_ref, wz_ref, cw_ref, cb_ref, hist_ref,
         ya_ref, nh_ref, ubuf, pbuf) = refs
    else:
        (h_ref, wb_ref, wc_ref, wu_ref, wz_ref, cw_ref, cb_ref,
         ya_ref, nh_ref, ubuf, pbuf) = refs
    taps = (cw_ref[0:1, :], cw_ref[1:2, :], cw_ref[2:3, :], cb_ref[...])
    top = SUBLANES
    hr = h_ref[...].reshape(sb * ts, h_ref.shape[2])
    for k, w_ref in enumerate((wb_ref, wc_ref, wu_ref, wz_ref)):
        pbuf[k] = _dot(hr, w_ref[...])
    for s in range(sb):
        ubuf[top - 2:top, :] = hist_ref[s] if has_hist else jnp.zeros((2, ubuf.shape[1]), F32)
        rows = slice(s * ts, (s + 1) * ts)
        ya_ref[s] = _conv_epilogue(ubuf, ts, taps, pbuf[0, rows, :], pbuf[1, rows, :],
                                   pbuf[2, rows, :], pbuf[3, rows, :])
        nh_ref[s] = ubuf[top - 2:top, :]


def _conv_seqs(h, w_in, conv_w, conv_b, hist, sb):
    b, t, d = h.shape
    cw = COL_BLOCK
    ng = d // cw
    has_hist = hist is not None
    in_specs = [pl.BlockSpec((sb, t, d), lambda i, g: (i, 0, 0))]
    in_specs += [pl.BlockSpec((d, cw), functools.partial(lambda i, g, k: (0, k * ng + g), k=k))
                 for k in range(4)]
    in_specs += [pl.BlockSpec((CONV_WIDTH, cw), lambda i, g: (0, g)),
                 pl.BlockSpec((1, cw), lambda i, g: (0, g))]
    args = [h, w_in, w_in, w_in, w_in, conv_w, conv_b.reshape(1, d)]
    if has_hist:
        in_specs.append(pl.BlockSpec((sb, CONV_WIDTH - 1, cw), lambda i, g: (i, 0, g)))
        args.append(hist)
    return pl.pallas_call(
        functools.partial(_conv_seqs_body, sb=sb, ts=t, has_hist=has_hist),
        grid=(b // sb, ng),
        in_specs=in_specs,
        out_specs=[pl.BlockSpec((sb, t, cw), lambda i, g: (i, 0, g)),
                   pl.BlockSpec((sb, CONV_WIDTH - 1, cw), lambda i, g: (i, 0, g))],
        out_shape=[jax.ShapeDtypeStruct((b, t, d), BF16),
                   jax.ShapeDtypeStruct((b, CONV_WIDTH - 1, d), F32)],
        scratch_shapes=[pltpu.VMEM((t + SUBLANES, cw), F32),
                        pltpu.VMEM((4, sb * t, cw), F32)],
        compiler_params=_params(("arbitrary", "arbitrary")),
        name="conv_seqs",
    )(*args)


def _ret_body(*refs, sb, ts, mc, tt, has_state):
    if has_state:
        (h_ref, wq_ref, wk_ref, wv_ref, wz_ref, rq_ref, rk_ref, mask_ref, qd_ref, kd_ref, bd_ref,
         st_ref, yb_ref, ns_ref, pbuf) = refs
    else:
        (h_ref, wq_ref, wk_ref, wv_ref, wz_ref, rq_ref, rk_ref, mask_ref, qd_ref, kd_ref, bd_ref,
         yb_ref, ns_ref, pbuf) = refs
        st_ref = None
    dk = LANES
    dv = 2 * LANES
    hpb = 2

    def tile(s, r0, q2, k2, v2, z2, cq, sq, ck, sk):
        for hh in range(hpb):
            q = q2[:, hh * dk:(hh + 1) * dk]
            k = k2[:, hh * dk:(hh + 1) * dk]
            v = v2[:, hh * dv:(hh + 1) * dv].astype(BF16)
            z = z2[:, hh * dv:(hh + 1) * dv]
            qr = q * cq + pltpu.roll(q, dk // 2, axis=1) * sq
            kr = k * ck + pltpu.roll(k, dk // 2, axis=1) * sk
            state = ns_ref[s, hh]
            scores = lax.dot_general(qr.astype(BF16), kr.astype(BF16),
                                     (((1,), (1,)), ((), ())), preferred_element_type=F32)
            p = (scores * mask_ref[hh]).astype(BF16)
            o = _dot(p, v) + _dot((qr * qd_ref[hh]).astype(BF16), state.astype(BF16))
            kv = lax.dot_general((kr * kd_ref[hh]).astype(BF16), v,
                                 (((0,), (0,)), ((), ())), preferred_element_type=F32)
            ns_ref[s, hh] = bd_ref[hh] * state + kv
            on = o * lax.rsqrt(jnp.mean(o * o, axis=-1, keepdims=True) + EPS)
            yb_ref[s, pl.ds(r0, tt), hh * dv:(hh + 1) * dv] = (on * _silu(z)).astype(BF16)

    def projections(hr):
        return (_dot(hr, wq_ref[...]), _dot(hr, wk_ref[...]),
                _dot(hr, wv_ref[...]), _dot(hr, wz_ref[...]))

    if has_state:
        ns_ref[...] = st_ref[...]
    else:
        ns_ref[...] = jnp.zeros(ns_ref.shape, F32)

    if mc <= ts:
        assert mc == tt
        for s in range(sb):
            def chunk(i, carry, s=s):
                r0 = pl.multiple_of(i * mc, mc)
                q2, k2, v2, z2 = projections(h_ref[s, pl.ds(r0, mc), :])
                rows = pl.ds(r0, mc)
                tile(s, r0, q2, k2, v2, z2, rq_ref[0, rows, :], rq_ref[1, rows, :],
                     rk_ref[0, rows, :], rk_ref[1, rows, :])
                return carry

            lax.fori_loop(0, ts // mc, chunk, 0, unroll=CHUNK_UNROLL)
    else:
        assert tt == ts
        hr = h_ref[...].reshape(sb * ts, h_ref.shape[2])
        for k, p in enumerate(projections(hr)):
            pbuf[k, :, 0:p.shape[1]] = p
        for s in range(sb):
            rows = slice(s * ts, (s + 1) * ts)
            tile(s, 0, pbuf[0, rows, 0:hpb * dk], pbuf[1, rows, 0:hpb * dk],
                 pbuf[2, rows, :], pbuf[3, rows, :],
                 rq_ref[0], rq_ref[1], rk_ref[0], rk_ref[1])


def _retention_tables(t, tt, pos0, chunk):
    dk = LANES
    inv = ROPE_BASE ** (-jnp.arange(0, dk, 2, dtype=F32) / dk)
    ang = (pos0 + jnp.arange(t)).astype(F32)[:, None] * inv[None, :]
    cos, sin = jnp.cos(ang), jnp.sin(ang)
    rope_q = jnp.stack([jnp.concatenate([cos, cos], -1), jnp.concatenate([-sin, sin], -1)])
    rope_k = rope_q * (dk ** -0.5)
    lg = jnp.log(1.0 - 2.0 ** (-5.0 - jnp.arange(N_HEADS, dtype=F32)))
    idx = jnp.arange(tt, dtype=F32)
    visible = (jnp.arange(tt)[None, :] // chunk) <= (jnp.arange(tt)[:, None] // chunk)
    mask = jnp.exp(lg[:, None, None] * jnp.abs(idx[:, None] - idx[None, :])) * visible.astype(F32)
    q_dec = jnp.broadcast_to(jnp.exp(lg[:, None] * (idx + 1.0))[..., None], (N_HEADS, tt, dk))
    k_dec = jnp.broadcast_to(jnp.exp(lg[:, None] * (tt - 1.0 - idx))[..., None], (N_HEADS, tt, dk))
    b_dec = jnp.broadcast_to(jnp.exp(lg * tt)[:, None, None], (N_HEADS, 1, 2 * dk))
    return rope_q, rope_k, mask, q_dec, k_dec, b_dec


def _ret_branch(h, w_in, state, pos0, sb, mc):
    b, t, d = h.shape
    dk, dv, hpb = LANES, 2 * LANES, 2
    nhp = N_HEADS // hpb
    tt = min(RET_TILE, t)
    chunk = min(REF_CHUNK, t)
    has_state = state is not None
    rope_q, rope_k, mask, q_dec, k_dec, b_dec = _retention_tables(t, tt, pos0, chunk)
    q_off = 4 * d // (hpb * dk)
    k_off = q_off + nhp
    v_off = (4 * d + 2 * N_HEADS * dk) // (hpb * dv)
    z_off = v_off + nhp
    in_specs = [pl.BlockSpec((sb, t, d), lambda i, g: (i, 0, 0)),
                pl.BlockSpec((d, hpb * dk), lambda i, g: (0, q_off + g)),
                pl.BlockSpec((d, hpb * dk), lambda i, g: (0, k_off + g)),
                pl.BlockSpec((d, hpb * dv), lambda i, g: (0, v_off + g)),
                pl.BlockSpec((d, hpb * dv), lambda i, g: (0, z_off + g)),
                pl.BlockSpec((2, t, dk), lambda i, g: (0, 0, 0)),
                pl.BlockSpec((2, t, dk), lambda i, g: (0, 0, 0)),
                pl.BlockSpec((hpb, tt, tt), lambda i, g: (g, 0, 0)),
                pl.BlockSpec((hpb, tt, dk), lambda i, g: (g, 0, 0)),
                pl.BlockSpec((hpb, tt, dk), lambda i, g: (g, 0, 0)),
                pl.BlockSpec((hpb, 1, dv), lambda i, g: (g, 0, 0))]
    args = [h, w_in, w_in, w_in, w_in, rope_q, rope_k, mask, q_dec, k_dec, b_dec]
    if has_state:
        in_specs.append(pl.BlockSpec((sb, hpb, dk, dv), lambda i, g: (i, g, 0, 0)))
        args.append(state)
    pbuf_rows = sb * t if mc > t else SUBLANES
    return pl.pallas_call(
        functools.partial(_ret_body, sb=sb, ts=t, mc=mc, tt=tt, has_state=has_state),
        grid=(b // sb, nhp),
        in_specs=in_specs,
        out_specs=[pl.BlockSpec((sb, t, hpb * dv), lambda i, g: (i, 0, g)),
                   pl.BlockSpec((sb, hpb, dk, dv), lambda i, g: (i, g, 0, 0))],
        out_shape=[jax.ShapeDtypeStruct((b, t, N_HEADS * dv), BF16),
                   jax.ShapeDtypeStruct((b, N_HEADS, dk, dv), F32)],
        scratch_shapes=[pltpu.VMEM((4, pbuf_rows, hpb * dv), F32)],
        compiler_params=_params(("arbitrary", "arbitrary")),
        name="ret_branch",
    )(*args)


def _merge_body(h_ref, ya_ref, yb_ref, wma_ref, wmb_ref, wa_ref, wb_ref, o_ref, *, mc):
    def chunk(i, carry):
        r0 = pl.multiple_of(i * mc, mc)
        rows = pl.ds(r0, mc)
        hr = h_ref[rows, :]
        ga = _sigmoid(_dot(hr, wma_ref[...]))
        gb = _sigmoid(_dot(hr, wmb_ref[...]))
        a = _dot(ya_ref[rows, :], wa_ref[0])
        b = _dot(yb_ref[rows, :], wb_ref[0])
        o_ref[rows, :] = (ga * a + gb * b).astype(BF16)
        return carry

    lax.fori_loop(0, h_ref.shape[0] // mc, chunk, 0, unroll=CHUNK_UNROLL)


def _merge(h2, ya2, yb2, w_in, w_branch, tm, mc):
    n, d = h2.shape
    nb = 512
    nblk = d // nb
    ma_off = (w_in.shape[1] - 2 * d) // nb
    mb_off = ma_off + nblk
    act = pl.BlockSpec((tm, d), lambda i, j: (i, 0))
    return pl.pallas_call(
        functools.partial(_merge_body, mc=mc),
        grid=(n // tm, nblk),
        in_specs=[act, act, act,
                  pl.BlockSpec((d, nb), lambda i, j: (0, ma_off + j)),
                  pl.BlockSpec((d, nb), lambda i, j: (0, mb_off + j)),
                  pl.BlockSpec((1, d, nb), lambda i, j: (0, 0, j)),
                  pl.BlockSpec((1, d, nb), lambda i, j: (1, 0, j))],
        out_specs=pl.BlockSpec((tm, nb), lambda i, j: (i, j)),
        out_shape=jax.ShapeDtypeStruct((n, d), BF16),
        compiler_params=_params(("arbitrary", "arbitrary")),
        name="merge",
    )(h2, ya2, yb2, w_in, w_in, w_branch, w_branch)


def _out_body(m_ref, w_ref, x_ref, mod_ref, g_ref, y_ref, obuf, *, sb, ts, mc):
    g = g_ref[...]

    def finish(o, x, gate):
        ms = jnp.mean(o * o, axis=-1, keepdims=True)
        return x + gate * (o * lax.rsqrt(ms + EPS) * g)

    if mc <= ts:
        for s in range(sb):
            gate = mod_ref[s, 2:3, :]

            def chunk(i, carry, s=s, gate=gate):
                rows = pl.ds(pl.multiple_of(i * mc, mc), mc)
                o = _dot(m_ref[s, rows, :], w_ref[...])
                y_ref[s, rows, :] = finish(o, x_ref[s, rows, :], gate)
                return carry

            lax.fori_loop(0, ts // mc, chunk, 0, unroll=CHUNK_UNROLL)
    else:
        obuf[...] = _dot(m_ref[...].reshape(sb * ts, m_ref.shape[2]), w_ref[...])
        for s in range(sb):
            y_ref[s] = finish(obuf[s * ts:(s + 1) * ts, :], x_ref[s], mod_ref[s, 2:3, :])


def _out_proj(merged, w_out, x, mod3, g_post, sb, tr, mc):
    b, t, d = x.shape
    obuf_rows = sb * tr if mc > tr else SUBLANES
    act = lambda i, j: (i, j, 0)
    return pl.pallas_call(
        functools.partial(_out_body, sb=sb, ts=tr, mc=mc),
        grid=(b // sb, t // tr),
        in_specs=[pl.BlockSpec((sb, tr, d), act),
                  pl.BlockSpec((d, d), lambda i, j: (0, 0)),
                  pl.BlockSpec((sb, tr, d), act),
                  pl.BlockSpec((sb, 3, d), lambda i, j: (i, 0, 0)),
                  pl.BlockSpec((1, d), lambda i, j: (0, 0))],
        out_specs=pl.BlockSpec((sb, tr, d), act),
        out_shape=jax.ShapeDtypeStruct((b, t, d), F32),
        scratch_shapes=[pltpu.VMEM((obuf_rows, d), F32)],
        compiler_params=_params(("arbitrary", "arbitrary")),
        name="out_proj",
    )(merged, w_out, x, mod3, g_post.reshape(1, d))


def _mixer_layer(x, mod3, hist, state, pos0, g_pre, g_post, w_in, conv_w, conv_b, w_branch, w_out):
    b, t, d = x.shape
    long_seq = t >= ROW_CHUNK
    sb = 1 if long_seq else b
    tr = 512 if long_seq else t
    mc = ROW_CHUNK if long_seq else b * t
    if long_seq:
        h, ya, new_hist = _conv_rows(x, mod3, g_pre, w_in, conv_w, conv_b, hist, tr, ROW_CHUNK)
    else:
        h = _prenorm(x, mod3, g_pre, sb, tr)
        ya, new_hist = _conv_seqs(h, w_in, conv_w, conv_b, hist, sb)
    yb, new_state = _ret_branch(h, w_in, state, pos0, sb, mc)
    n = b * t
    tm = 1024 if long_seq else n
    merged = _merge(h.reshape(n, d), ya.reshape(n, d), yb.reshape(n, d), w_in, w_branch, tm, ROW_CHUNK)
    y = _out_proj(merged.reshape(b, t, d), w_out, x, mod3, g_post, sb, tr, mc)
    return y, new_hist, new_state


def kernel(x_prompt, x_sample, c_prompt, c_sample, state_conv, state_ret, ada_w, ada_b, norm_pre,
           norm_post, w_in, conv_w, conv_b, w_branch, w_out):
    depth = w_in.shape[0]
    bp = x_prompt.shape[0]
    d = x_prompt.shape[2]
    hp, hs = x_prompt, x_sample
    conv_p, ret_p, conv_s, ret_s = [], [], [], []
    c_all = jnp.concatenate([c_prompt, c_sample], axis=0)
    for l in range(depth):
        mod3 = _modulation(c_all, ada_w[l], ada_b[l]).reshape(c_all.shape[0], 3, d)
        w_in_l = w_in[l].astype(BF16)
        w_branch_l = w_branch[l].astype(BF16)
        w_out_l = w_out[l].astype(BF16)
        shared = (norm_pre[l], norm_post[l], w_in_l, conv_w[l], conv_b[l], w_branch_l, w_out_l)
        hp, cp, rp = _mixer_layer(hp, mod3[:bp], None, None, 0, *shared)
        hs, cs, rs = _mixer_layer(hs, mod3[bp:], state_conv[l], state_ret[l], PAST_LEN, *shared)
        conv_p.append(cp)
        ret_p.append(rp)
        conv_s.append(cs)
        ret_s.append(rs)
    return (hp, hs, jnp.stack(conv_p), jnp.stack(ret_p), jnp.stack(conv_s), jnp.stack(ret_s))
```

```python
import functools

import jax
import jax.numpy as jnp
from jax import lax
from jax.experimental import pallas as pl
from jax.experimental.pallas import tpu as pltpu

F32 = jnp.float32
BF16 = jnp.bfloat16

EPS = 1e-6
ROPE_BASE = 10000.0
REF_CHUNK = 64
N_HEADS = 8
CONV_WIDTH = 3

LANES = 128
SUBLANES = 8
COL_BLOCK = 256
ROW_CHUNK = 256
RET_TILE = 256
CHUNK_UNROLL = True
PAST_LEN = 1024
VMEM_LIMIT = 56 * 1024 * 1024
CONV_ROWS_VMEM_LIMIT = 60 * 1024 * 1024


def _dot(a, b):
    return jnp.dot(a, b, preferred_element_type=F32)


def _sigmoid(x):
    return 1.0 / (1.0 + jnp.exp(-x))


def _silu(x):
    return x * _sigmoid(x)


def _params(semantics):
    return pltpu.CompilerParams(dimension_semantics=semantics, vmem_limit_bytes=VMEM_LIMIT)


def _mod_body(c_ref, w_ref, b_ref, o_ref):
    a = _silu(c_ref[...]).astype(BF16)
    o_ref[...] = _dot(a, w_ref[...].astype(BF16)) + b_ref[...]


def _modulation(c_all, ada_w, ada_b):
    nseq, d = c_all.shape
    n_out = ada_w.shape[1]
    nb = 768
    return pl.pallas_call(
        _mod_body,
        grid=(n_out // nb,),
        in_specs=[pl.BlockSpec((nseq, d), lambda j: (0, 0)),
                  pl.BlockSpec((d, nb), lambda j: (0, j)),
                  pl.BlockSpec((1, nb), lambda j: (0, j))],
        out_specs=pl.BlockSpec((nseq, nb), lambda j: (0, j)),
        out_shape=jax.ShapeDtypeStruct((nseq, n_out), F32),
        compiler_params=_params(("arbitrary",)),
        name="modulation",
    )(c_all, ada_w, ada_b.reshape(1, n_out))


def _prenorm_body(x_ref, mod_ref, g_ref, h_ref):
    x = x_ref[...]
    ms = jnp.mean(x * x, axis=-1, keepdims=True)
    y = x * lax.rsqrt(ms + EPS) * g_ref[...]
    shift = mod_ref[:, 0:1, :]
    scale = mod_ref[:, 1:2, :]
    h_ref[...] = (y * (1.0 + scale) + shift).astype(BF16)


def _prenorm(x, mod3, g_pre, sb, tr):
    b, t, d = x.shape
    return pl.pallas_call(
        _prenorm_body,
        grid=(b // sb, t // tr),
        in_specs=[pl.BlockSpec((sb, tr, d), lambda i, j: (i, j, 0)),
                  pl.BlockSpec((sb, 3, d), lambda i, j: (i, 0, 0)),
                  pl.BlockSpec((1, d), lambda i, j: (0, 0))],
        out_specs=pl.BlockSpec((sb, tr, d), lambda i, j: (i, j, 0)),
        out_shape=jax.ShapeDtypeStruct((b, t, d), BF16),
        compiler_params=_params(("arbitrary", "arbitrary")),
        name="prenorm",
    )(x, mod3, g_pre.reshape(1, d))


def _conv_epilogue(ubuf, ec, taps, gb, gc, gu, zc):
    w0, w1, w2, cb = taps
    top = SUBLANES
    u = gc * gu
    ubuf[top:top + ec, :] = u
    f0 = ubuf[top - 2:top - 2 + ec, :]
    f1 = ubuf[top - 1:top - 1 + ec, :]
    conv = cb + w0 * f0 + w1 * f1 + w2 * u
    ubuf[top - 2:top, :] = ubuf[top + ec - 2:top + ec, :]
    return (gb * conv * _silu(zc)).astype(BF16)


def _conv_rows_body(*refs, mc, has_hist):
    if has_hist:
        (x_ref, mod_ref, g_ref, wb_ref, wc_ref, wu_ref, wz_ref, cw_ref, cb_ref, hist_ref,
         h_ref, ya_ref, nh_ref, ubuf) = refs
    else:
        (x_ref, mod_ref, g_ref, wb_ref, wc_ref, wu_ref, wz_ref, cw_ref, cb_ref,
         h_ref, ya_ref, nh_ref, ubuf) = refs
    tr, d = x_ref.shape[1], x_ref.shape[2]
    cw = ubuf.shape[2]
    ng = d // cw
    top = SUBLANES
    r = pl.program_id(1)

    @pl.when(r == 0)
    def _():
        for gi in range(ng):
            cols = slice(gi * cw, (gi + 1) * cw)
            ubuf[gi, top - 2:top, :] = hist_ref[0, :, cols] if has_hist else jnp.zeros((2, cw), F32)

    g_pre = g_ref[...]
    shift = mod_ref[0, 0:1, :]
    scale1 = 1.0 + mod_ref[0, 1:2, :]
    for sub in range(tr // mc):
        rows = slice(sub * mc, (sub + 1) * mc)
        x = x_ref[0, rows, :]
        ms = jnp.mean(x * x, axis=-1, keepdims=True)
        h_ref[0, rows, :] = (x * lax.rsqrt(ms + EPS) * g_pre * scale1 + shift).astype(BF16)
        hr = h_ref[0, rows, :]
        for gi in range(ng):
            cols = slice(gi * cw, (gi + 1) * cw)
            taps = (cw_ref[0:1, cols], cw_ref[1:2, cols], cw_ref[2:3, cols], cb_ref[:, cols])
            ya_ref[0, rows, cols] = _conv_epilogue(
                ubuf.at[gi], mc, taps, gc=_dot(hr, wc_ref[:, cols]), gu=_dot(hr, wu_ref[:, cols]),
                zc=_dot(hr, wz_ref[:, cols]), gb=_dot(hr, wb_ref[:, cols]))

    @pl.when(r == pl.num_programs(1) - 1)
    def _():
        for gi in range(ng):
            nh_ref[0, :, gi * cw:(gi + 1) * cw] = ubuf[gi, top - 2:top, :]


def _conv_rows(x, mod3, g_pre, w_in, conv_w, conv_b, hist, tr, mc):
    b, t, d = x.shape
    cw = COL_BLOCK
    has_hist = hist is not None
    row_block = lambda i, r: (i, r, 0)
    per_seq = lambda i, r: (i, 0, 0)
    whole = lambda i, r: (0, 0)
    in_specs = [pl.BlockSpec((1, tr, d), row_block),
                pl.BlockSpec((1, 3, d), per_seq),
                pl.BlockSpec((1, d), whole)]
    in_specs += [pl.BlockSpec((d, d), functools.partial(lambda i, r, k: (0, k), k=k),
                              pipeline_mode=pl.Buffered(1)) for k in range(4)]
    in_specs += [pl.BlockSpec((CONV_WIDTH, d), whole), pl.BlockSpec((1, d), whole)]
    args = [x, mod3, g_pre.reshape(1, d), w_in, w_in, w_in, w_in, conv_w, conv_b.reshape(1, d)]
    if has_hist:
        in_specs.append(pl.BlockSpec((1, CONV_WIDTH - 1, d), per_seq))
        args.append(hist)
    return pl.pallas_call(
        functools.partial(_conv_rows_body, mc=mc, has_hist=has_hist),
        grid=(b, t // tr),
        in_specs=in_specs,
        out_specs=[pl.BlockSpec((1, tr, d), row_block),
                   pl.BlockSpec((1, tr, d), row_block),
                   pl.BlockSpec((1, CONV_WIDTH - 1, d), per_seq)],
        out_shape=[jax.ShapeDtypeStruct((b, t, d), BF16),
                   jax.ShapeDtypeStruct((b, t, d), BF16),
                   jax.ShapeDtypeStruct((b, CONV_WIDTH - 1, d), F32)],
        scratch_shapes=[pltpu.VMEM((d // cw, mc + SUBLANES, cw), F32)],
        compiler_params=pltpu.CompilerParams(dimension_semantics=("arbitrary", "arbitrary"),
                                             vmem_limit_bytes=CONV_ROWS_VMEM_LIMIT),
        name="conv_rows",
    )(*args)


def _conv_seqs_body(*refs, sb, ts, has_hist):
    if has_hist:
        (h_ref, wb_ref, wc_ref, wu_ref, wz_ref, cw_ref, cb_ref, hist_ref,
         ya_ref, nh_ref, ubuf, pbuf) = refs
    else:
        (h_ref, wb_ref, wc_ref, wu_ref, wz_ref, cw_ref, cb_ref,
         ya_ref, nh_ref, ubuf, pbuf) = refs
    taps = (cw_ref[0:1, :], cw_ref[1:2, :], cw_ref[2:3, :], cb_ref[...])
    top = SUBLANES
    hr = h_ref[...].reshape(sb * ts, h_ref.shape[2])
    for k, w_ref in enumerate((wb_ref, wc_ref, wu_ref, wz_ref)):
        pbuf[k] = _dot(hr, w_ref[...])
    for s in range(sb):
        ubuf[top - 2:top, :] = hist_ref[s] if has_hist else jnp.zeros((2, ubuf.shape[1]), F32)
        rows = slice(s * ts, (s + 1) * ts)
        ya_ref[s] = _conv_epilogue(ubuf, ts, taps, pbuf[0, rows, :], pbuf[1, rows, :],
                                   pbuf[2, rows, :], pbuf[3, rows, :])
        nh_ref[s] = ubuf[top - 2:top, :]


def _conv_seqs(h, w_in, conv_w, conv_b, hist, sb):
    b, t, d = h.shape
    cw = COL_BLOCK
    ng = d // cw
    has_hist = hist is not None
    in_specs = [pl.BlockSpec((sb, t, d), lambda i, g: (i, 0, 0))]
    in_specs += [pl.BlockSpec((d, cw), functools.partial(lambda i, g, k: (0, k * ng + g), k=k))
                 for k in range(4)]
    in_specs += [pl.BlockSpec((CONV_WIDTH, cw), lambda i, g: (0, g)),
                 pl.BlockSpec((1, cw), lambda i, g: (0, g))]
    args = [h, w_in, w_in, w_in, w_in, conv_w, conv_b.reshape(1, d)]
    if has_hist:
        in_specs.append(pl.BlockSpec((sb, CONV_WIDTH - 1, cw), lambda i, g: (i, 0, g)))
        args.append(hist)
    return pl.pallas_call(
        functools.partial(_conv_seqs_body, sb=sb, ts=t, has_hist=has_hist),
        grid=(b // sb, ng),
        in_specs=in_specs,
        out_specs=[pl.BlockSpec((sb, t, cw), lambda i, g: (i, 0, g)),
                   pl.BlockSpec((sb, CONV_WIDTH - 1, cw), lambda i, g: (i, 0, g))],
        out_shape=[jax.ShapeDtypeStruct((b, t, d), BF16),
                   jax.ShapeDtypeStruct((b, CONV_WIDTH - 1, d), F32)],
        scratch_shapes=[pltpu.VMEM((t + SUBLANES, cw), F32),
                        pltpu.VMEM((4, sb * t, cw), F32)],
        compiler_params=_params(("arbitrary", "arbitrary")),
        name="conv_seqs",
    )(*args)


def _ret_body(*refs, sb, ts, mc, tt, has_state):
    if has_state:
        (h_ref, wq_ref, wk_ref, wv_ref, wz_ref, rq_ref, rk_ref, mask_ref, qd_ref, kd_ref, bd_ref,
         st_ref, yb_ref, ns_ref, pbuf) = refs
    else:
        (h_ref, wq_ref, wk_ref, wv_ref, wz_ref, rq_ref, rk_ref, mask_ref, qd_ref, kd_ref, bd_ref,
         yb_ref, ns_ref, pbuf) = refs
        st_ref = None
    dk = LANES
    dv = 2 * LANES
    hpb = 2

    def tile(s, r0, q2, k2, v2, z2, cq, sq, ck, sk):
        for hh in range(hpb):
            q = q2[:, hh * dk:(hh + 1) * dk]
            k = k2[:, hh * dk:(hh + 1) * dk]
            v = v2[:, hh * dv:(hh + 1) * dv].astype(BF16)
            z = z2[:, hh * dv:(hh + 1) * dv]
            qr = q * cq + pltpu.roll(q, dk // 2, axis=1) * sq
            kr = k * ck + pltpu.roll(k, dk // 2, axis=1) * sk
            state = ns_ref[s, hh]
            scores = lax.dot_general(qr.astype(BF16), kr.astype(BF16),
                                     (((1,), (1,)), ((), ())), preferred_element_type=F32)
            p = (scores * mask_ref[hh]).astype(BF16)
            o = _dot(p, v) + _dot((qr * qd_ref[hh]).astype(BF16), state.astype(BF16))
            kv = lax.dot_general((kr * kd_ref[hh]).astype(BF16), v,
                                 (((0,), (0,)), ((), ())), preferred_element_type=F32)
            ns_ref[s, hh] = bd_ref[hh] * state + kv
            on = o * lax.rsqrt(jnp.mean(o * o, axis=-1, keepdims=True) + EPS)
            yb_ref[s, pl.ds(r0, tt), hh * dv:(hh + 1) * dv] = (on * _silu(z)).astype(BF16)

    def projections(hr):
        return (_dot(hr, wq_ref[...]), _dot(hr, wk_ref[...]),
                _dot(hr, wv_ref[...]), _dot(hr, wz_ref[...]))

    if has_state:
        ns_ref[...] = st_ref[...]
    else:
        ns_ref[...] = jnp.zeros(ns_ref.shape, F32)

    if mc <= ts:
        assert mc == tt
        for s in range(sb):
            def chunk(i, carry, s=s):
                r0 = pl.multiple_of(i * mc, mc)
                q2, k2, v2, z2 = projections(h_ref[s, pl.ds(r0, mc), :])
                rows = pl.ds(r0, mc)
                tile(s, r0, q2, k2, v2, z2, rq_ref[0, rows, :], rq_ref[1, rows, :],
                     rk_ref[0, rows, :], rk_ref[1, rows, :])
                return carry

            lax.fori_loop(0, ts // mc, chunk, 0, unroll=CHUNK_UNROLL)
    else:
        assert tt == ts
        hr = h_ref[...].reshape(sb * ts, h_ref.shape[2])
        for k, p in enumerate(projections(hr)):
            pbuf[k, :, 0:p.shape[1]] = p
        for s in range(sb):
            rows = slice(s * ts, (s + 1) * ts)
            tile(s, 0, pbuf[0, rows, 0:hpb * dk], pbuf[1, rows, 0:hpb * dk],
                 pbuf[2, rows, :], pbuf[3, rows, :],
                 rq_ref[0], rq_ref[1], rk_ref[0], rk_ref[1])


def _retention_tables(t, tt, pos0, chunk):
    dk = LANES
    inv = ROPE_BASE ** (-jnp.arange(0, dk, 2, dtype=F32) / dk)
    ang = (pos0 + jnp.arange(t)).astype(F32)[:, None] * inv[None, :]
    cos, sin = jnp.cos(ang), jnp.sin(ang)
    rope_q = jnp.stack([jnp.concatenate([cos, cos], -1), jnp.concatenate([-sin, sin], -1)])
    rope_k = rope_q * (dk ** -0.5)
    lg = jnp.log(1.0 - 2.0 ** (-5.0 - jnp.arange(N_HEADS, dtype=F32)))
    idx = jnp.arange(tt, dtype=F32)
    visible = (jnp.arange(tt)[None, :] // chunk) <= (jnp.arange(tt)[:, None] // chunk)
    mask = jnp.exp(lg[:, None, None] * jnp.abs(idx[:, None] - idx[None, :])) * visible.astype(F32)
    q_dec = jnp.broadcast_to(jnp.exp(lg[:, None] * (idx + 1.0))[..., None], (N_HEADS, tt, dk))
    k_dec = jnp.broadcast_to(jnp.exp(lg[:, None] * (tt - 1.0 - idx))[..., None], (N_HEADS, tt, dk))
    b_dec = jnp.broadcast_to(jnp.exp(lg * tt)[:, None, None], (N_HEADS, 1, 2 * dk))
    return rope_q, rope_k, mask, q_dec, k_dec, b_dec


def _ret_branch(h, w_in, state, pos0, sb, mc):
    b, t, d = h.shape
    dk, dv, hpb = LANES, 2 * LANES, 2
    nhp = N_HEADS // hpb
    tt = min(RET_TILE, t)
    chunk = min(REF_CHUNK, t)
    has_state = state is not None
    rope_q, rope_k, mask, q_dec, k_dec, b_dec = _retention_tables(t, tt, pos0, chunk)
    q_off = 4 * d // (hpb * dk)
    k_off = q_off + nhp
    v_off = (4 * d + 2 * N_HEADS * dk) // (hpb * dv)
    z_off = v_off + nhp
    in_specs = [pl.BlockSpec((sb, t, d), lambda i, g: (i, 0, 0)),
                pl.BlockSpec((d, hpb * dk), lambda i, g: (0, q_off + g)),
                pl.BlockSpec((d, hpb * dk), lambda i, g: (0, k_off + g)),
                pl.BlockSpec((d, hpb * dv), lambda i, g: (0, v_off + g)),
                pl.BlockSpec((d, hpb * dv), lambda i, g: (0, z_off + g)),
                pl.BlockSpec((2, t, dk), lambda i, g: (0, 0, 0)),
                pl.BlockSpec((2, t, dk), lambda i, g: (0, 0, 0)),
                pl.BlockSpec((hpb, tt, tt), lambda i, g: (g, 0, 0)),
                pl.BlockSpec((hpb, tt, dk), lambda i, g: (g, 0, 0)),
                pl.BlockSpec((hpb, tt, dk), lambda i, g: (g, 0, 0)),
                pl.BlockSpec((hpb, 1, dv), lambda i, g: (g, 0, 0))]
    args = [h, w_in, w_in, w_in, w_in, rope_q, rope_k, mask, q_dec, k_dec, b_dec]
    if has_state:
        in_specs.append(pl.BlockSpec((sb, hpb, dk, dv), lambda i, g: (i, g, 0, 0)))
        args.append(state)
    pbuf_rows = sb * t if mc > t else SUBLANES
    return pl.pallas_call(
        functools.partial(_ret_body, sb=sb, ts=t, mc=mc, tt=tt, has_state=has_state),
        grid=(b // sb, nhp),
        in_specs=in_specs,
        out_specs=[pl.BlockSpec((sb, t, hpb * dv), lambda i, g: (i, 0, g)),
                   pl.BlockSpec((sb, hpb, dk, dv), lambda i, g: (i, g, 0, 0))],
        out_shape=[jax.ShapeDtypeStruct((b, t, N_HEADS * dv), BF16),
                   jax.ShapeDtypeStruct((b, N_HEADS, dk, dv), F32)],
        scratch_shapes=[pltpu.VMEM((4, pbuf_rows, hpb * dv), F32)],
        compiler_params=_params(("arbitrary", "arbitrary")),
        name="ret_branch",
    )(*args)


def _merge_body(h_ref, ya_ref, yb_ref, wma_ref, wmb_ref, wa_ref, wb_ref, o_ref, *, mc):
    def chunk(i, carry):
        r0 = pl.multiple_of(i * mc, mc)
        rows = pl.ds(r0, mc)
        hr = h_ref[rows, :]
        ma = _sigmoid(_dot(hr, wma_ref[...])) * _dot(ya_ref[rows, :], wa_ref[0])
        mb = _sigmoid(_dot(hr, wmb_ref[...])) * _dot(yb_ref[rows, :], wb_ref[0])
        o_ref[rows, :] = (ma + mb).astype(BF16)
        return carry

    lax.fori_loop(0, h_ref.shape[0] // mc, chunk, 0, unroll=CHUNK_UNROLL)


def _merge(h2, ya2, yb2, w_in, w_branch, tm, mc):
    n, d = h2.shape
    nb = 512
    nblk = d // nb
    ma_off = (w_in.shape[1] - 2 * d) // nb
    mb_off = ma_off + nblk
    act = pl.BlockSpec((tm, d), lambda i, j: (i, 0))
    return pl.pallas_call(
        functools.partial(_merge_body, mc=mc),
        grid=(n // tm, nblk),
        in_specs=[act, act, act,
                  pl.BlockSpec((d, nb), lambda i, j: (0, ma_off + j)),
                  pl.BlockSpec((d, nb), lambda i, j: (0, mb_off + j)),
                  pl.BlockSpec((1, d, nb), lambda i, j: (0, 0, j)),
                  pl.BlockSpec((1, d, nb), lambda i, j: (1, 0, j))],
        out_specs=pl.BlockSpec((tm, nb), lambda i, j: (i, j)),
        out_shape=jax.ShapeDtypeStruct((n, d), BF16),
        compiler_params=_params(("arbitrary", "arbitrary")),
        name="merge",
    )(h2, ya2, yb2, w_in, w_in, w_branch, w_branch)


def _out_body(m_ref, w_ref, x_ref, mod_ref, g_ref, y_ref, obuf, *, sb, ts, mc):
    g = g_ref[...]

    def finish(o, x, gate):
        ms = jnp.mean(o * o, axis=-1, keepdims=True)
        return x + gate * (o * lax.rsqrt(ms + EPS) * g)

    if mc <= ts:
        for s in range(sb):
            gate = mod_ref[s, 2:3, :]

            def chunk(i, carry, s=s, gate=gate):
                rows = pl.ds(pl.multiple_of(i * mc, mc), mc)
                o = _dot(m_ref[s, rows, :], w_ref[...])
                y_ref[s, rows, :] = finish(o, x_ref[s, rows, :], gate)
                return carry

            lax.fori_loop(0, ts // mc, chunk, 0, unroll=CHUNK_UNROLL)
    else:
        obuf[...] = _dot(m_ref[...].reshape(sb * ts, m_ref.shape[2]), w_ref[...])
        for s in range(sb):
            y_ref[s] = finish(obuf[s * ts:(s + 1) * ts, :], x_ref[s], mod_ref[s, 2:3, :])


def _out_proj(merged, w_out, x, mod3, g_post, sb, tr, mc):
    b, t, d = x.shape
    obuf_rows = sb * tr if mc > tr else SUBLANES
    act = lambda i, j: (i, j, 0)
    return pl.pallas_call(
        functools.partial(_out_body, sb=sb, ts=tr, mc=mc),
        grid=(b // sb, t // tr),
        in_specs=[pl.BlockSpec((sb, tr, d), act),
                  pl.BlockSpec((d, d), lambda i, j: (0, 0), pipeline_mode=pl.Buffered(1)),
                  pl.BlockSpec((sb, tr, d), act),
                  pl.BlockSpec((sb, 3, d), lambda i, j: (i, 0, 0)),
                  pl.BlockSpec((1, d), lambda i, j: (0, 0))],
        out_specs=pl.BlockSpec((sb, tr, d), act),
        out_shape=jax.ShapeDtypeStruct((b, t, d), F32),
        scratch_shapes=[pltpu.VMEM((obuf_rows, d), F32)],
        compiler_params=_params(("arbitrary", "arbitrary")),
        name="out_proj",
    )(merged, w_out, x, mod3, g_post.reshape(1, d))


def _mixer_layer(x, mod3, hist, state, pos0, g_pre, g_post, w_in, conv_w, conv_b, w_branch, w_out):
    b, t, d = x.shape
    long_seq = t >= ROW_CHUNK
    sb = 1 if long_seq else b
    tr = 512 if long_seq else t
    mc = ROW_CHUNK if long_seq else b * t
    if long_seq:
        h, ya, new_hist = _conv_rows(x, mod3, g_pre, w_in, conv_w, conv_b, hist, tr, ROW_CHUNK)
    else:
        h = _prenorm(x, mod3, g_pre, sb, tr)
        ya, new_hist = _conv_seqs(h, w_in, conv_w, conv_b, hist, sb)
    yb, new_state = _ret_branch(h, w_in, state, pos0, sb, mc)
    n = b * t
    tm = 1024 if long_seq else n
    merged = _merge(h.reshape(n, d), ya.reshape(n, d), yb.reshape(n, d), w_in, w_branch, tm, ROW_CHUNK)
    y = _out_proj(merged.reshape(b, t, d), w_out, x, mod3, g_post, sb, 2 * tr if long_seq else tr, mc)
    return y, new_hist, new_state


def kernel(x_prompt, x_sample, c_prompt, c_sample, state_conv, state_ret, ada_w, ada_b, norm_pre,
           norm_post, w_in, conv_w, conv_b, w_branch, w_out):
    depth = w_in.shape[0]
    bp = x_prompt.shape[0]
    d = x_prompt.shape[2]
    hp, hs = x_prompt, x_sample
    conv_p, ret_p, conv_s, ret_s = [], [], [], []
    c_all = jnp.concatenate([c_prompt, c_sample], axis=0)
    for l in range(depth):
        mod3 = _modulation(c_all, ada_w[l], ada_b[l]).reshape(c_all.shape[0], 3, d)
        w_in_l = w_in[l].astype(BF16)
        w_branch_l = w_branch[l].astype(BF16)
        w_out_l = w_out[l].astype(BF16)
        shared = (norm_pre[l], norm_post[l], w_in_l, conv_w[l], conv_b[l], w_branch_l, w_out_l)
        hp, cp, rp = _mixer_layer(hp, mod3[:bp], None, None, 0, *shared)
        hs, cs, rs = _mixer_layer(hs, mod3[bp:], state_conv[l], state_ret[l], PAST_LEN, *shared)
        conv_p.append(cp)
        ret_p.append(rp)
        conv_s.append(cs)
        ret_s.append(rs)
    return (hp, hs, jnp.stack(conv_p), jnp.stack(ret_p), jnp.stack(conv_s), jnp.stack(ret_s))
```

```python
import functools

import jax
import jax.numpy as jnp
from jax import lax
from jax.experimental import pallas as pl
from jax.experimental.pallas import tpu as pltpu

F32 = jnp.float32
BF16 = jnp.bfloat16

EPS = 1e-6
ROPE_BASE = 10000.0
REF_CHUNK = 64
N_HEADS = 8
CONV_WIDTH = 3

LANES = 128
SUBLANES = 8
COL_BLOCK = 256
ROW_CHUNK = 256
RET_TILE = 256
CHUNK_UNROLL = True
PAST_LEN = 1024
VMEM_LIMIT = 56 * 1024 * 1024
CONV_ROWS_VMEM_LIMIT = 60 * 1024 * 1024


def _dot(a, b):
    return jnp.dot(a, b, preferred_element_type=F32)


def _sigmoid(x):
    return 1.0 / (1.0 + jnp.exp(-x))


def _silu(x):
    return x * _sigmoid(x)


def _params(semantics):
    return pltpu.CompilerParams(dimension_semantics=semantics, vmem_limit_bytes=VMEM_LIMIT)


def _cast_specs(w_f32, col0, cols, n_steps, step_of):
    m, d, _ = w_f32.shape
    per = cols // COL_BLOCK
    assert cols % COL_BLOCK == 0 and col0 % COL_BLOCK == 0 and m * per <= n_steps

    def block(*ids):
        return jnp.minimum(step_of(*ids), m * per - 1)

    in_spec = pl.BlockSpec((1, d, COL_BLOCK),
                           lambda *ids: (block(*ids) // per, 0, col0 // COL_BLOCK + block(*ids) % per))
    out_spec = pl.BlockSpec((1, d, COL_BLOCK), lambda *ids: (block(*ids) // per, 0, block(*ids) % per))
    return in_spec, out_spec, jax.ShapeDtypeStruct((m, d, cols), BF16), m * per


def _side_cast(wf_ref, wr_ref, n_blocks):
    step = pl.program_id(0) * pl.num_programs(1) + pl.program_id(1)

    @pl.when(step < n_blocks)
    def _():
        wr_ref[...] = wf_ref[...].astype(BF16)


def _mod_body(c_ref, w_ref, b_ref, o_ref):
    a = _silu(c_ref[...]).astype(BF16)
    o_ref[...] = _dot(a, w_ref[...].astype(BF16)) + b_ref[...]


def _modulation(c_all, ada_w, ada_b):
    nseq, d = c_all.shape
    n_out = ada_w.shape[1]
    nb = 768
    return pl.pallas_call(
        _mod_body,
        grid=(n_out // nb,),
        in_specs=[pl.BlockSpec((nseq, d), lambda j: (0, 0)),
                  pl.BlockSpec((d, nb), lambda j: (0, j)),
                  pl.BlockSpec((1, nb), lambda j: (0, j))],
        out_specs=pl.BlockSpec((nseq, nb), lambda j: (0, j)),
        out_shape=jax.ShapeDtypeStruct((nseq, n_out), F32),
        compiler_params=_params(("arbitrary",)),
        name="modulation",
    )(c_all, ada_w, ada_b.reshape(1, n_out))


def _prenorm_body(x_ref, mod_ref, g_ref, h_ref):
    x = x_ref[...]
    ms = jnp.mean(x * x, axis=-1, keepdims=True)
    y = x * lax.rsqrt(ms + EPS) * g_ref[...]
    shift = mod_ref[:, 0:1, :]
    scale = mod_ref[:, 1:2, :]
    h_ref[...] = (y * (1.0 + scale) + shift).astype(BF16)


def _prenorm(x, mod3, g_pre, sb, tr):
    b, t, d = x.shape
    return pl.pallas_call(
        _prenorm_body,
        grid=(b // sb, t // tr),
        in_specs=[pl.BlockSpec((sb, tr, d), lambda i, j: (i, j, 0)),
                  pl.BlockSpec((sb, 3, d), lambda i, j: (i, 0, 0)),
                  pl.BlockSpec((1, d), lambda i, j: (0, 0))],
        out_specs=pl.BlockSpec((sb, tr, d), lambda i, j: (i, j, 0)),
        out_shape=jax.ShapeDtypeStruct((b, t, d), BF16),
        compiler_params=_params(("arbitrary", "arbitrary")),
        name="prenorm",
    )(x, mod3, g_pre.reshape(1, d))


def _conv_epilogue(ubuf, ec, taps, gb, gc, gu, zc):
    w0, w1, w2, cb = taps
    top = SUBLANES
    u = gc * gu
    ubuf[top:top + ec, :] = u
    f0 = ubuf[top - 2:top - 2 + ec, :]
    f1 = ubuf[top - 1:top - 1 + ec, :]
    conv = cb + w0 * f0 + w1 * f1 + w2 * u
    ubuf[top - 2:top, :] = ubuf[top + ec - 2:top + ec, :]
    return (gb * conv * _silu(zc)).astype(BF16)


def _conv_rows_body(*refs, mc, has_hist, cast_blocks):
    it = iter(refs)
    x_ref, mod_ref, g_ref, wb_ref, wc_ref, wu_ref, wz_ref, cw_ref, cb_ref = (next(it) for _ in range(9))
    hist_ref = next(it) if has_hist else None
    wf_ref = next(it) if cast_blocks else None
    h_ref, ya_ref, nh_ref = next(it), next(it), next(it)
    wr_ref = next(it) if cast_blocks else None
    ubuf = next(it)
    if cast_blocks:
        _side_cast(wf_ref, wr_ref, cast_blocks)
    tr, d = x_ref.shape[1], x_ref.shape[2]
    cw = ubuf.shape[2]
    ng = d // cw
    top = SUBLANES
    r = pl.program_id(1)

    @pl.when(r == 0)
    def _():
        for gi in range(ng):
            cols = slice(gi * cw, (gi + 1) * cw)
            ubuf[gi, top - 2:top, :] = hist_ref[0, :, cols] if has_hist else jnp.zeros((2, cw), F32)

    g_pre = g_ref[...]
    shift = mod_ref[0, 0:1, :]
    scale1 = 1.0 + mod_ref[0, 1:2, :]
    for sub in range(tr // mc):
        rows = slice(sub * mc, (sub + 1) * mc)
        x = x_ref[0, rows, :]
        ms = jnp.mean(x * x, axis=-1, keepdims=True)
        h_ref[0, rows, :] = (x * lax.rsqrt(ms + EPS) * g_pre * scale1 + shift).astype(BF16)
        hr = h_ref[0, rows, :]
        for gi in range(ng):
            cols = slice(gi * cw, (gi + 1) * cw)
            taps = (cw_ref[0:1, cols], cw_ref[1:2, cols], cw_ref[2:3, cols], cb_ref[:, cols])
            ya_ref[0, rows, cols] = _conv_epilogue(
                ubuf.at[gi], mc, taps, gc=_dot(hr, wc_ref[:, cols]), gu=_dot(hr, wu_ref[:, cols]),
                zc=_dot(hr, wz_ref[:, cols]), gb=_dot(hr, wb_ref[:, cols]))

    @pl.when(r == pl.num_programs(1) - 1)
    def _():
        for gi in range(ng):
            nh_ref[0, :, gi * cw:(gi + 1) * cw] = ubuf[gi, top - 2:top, :]


def _conv_rows(x, mod3, g_pre, w_conv, conv_w, conv_b, hist, tr, mc, cast_f32=None, cast_col0=0):
    b, t, d = x.shape
    cw = COL_BLOCK
    nr = t // tr
    has_hist = hist is not None
    row_block = lambda i, r: (i, r, 0)
    per_seq = lambda i, r: (i, 0, 0)
    whole = lambda i, r: (0, 0)
    in_specs = [pl.BlockSpec((1, tr, d), row_block),
                pl.BlockSpec((1, 3, d), per_seq),
                pl.BlockSpec((1, d), whole)]
    in_specs += [pl.BlockSpec((d, d), functools.partial(lambda i, r, k: (0, k), k=k),
                              pipeline_mode=pl.Buffered(1)) for k in range(4)]
    in_specs += [pl.BlockSpec((CONV_WIDTH, d), whole), pl.BlockSpec((1, d), whole)]
    args = [x, mod3, g_pre.reshape(1, d), w_conv, w_conv, w_conv, w_conv, conv_w, conv_b.reshape(1, d)]
    if has_hist:
        in_specs.append(pl.BlockSpec((1, CONV_WIDTH - 1, d), per_seq))
        args.append(hist)
    out_specs = [pl.BlockSpec((1, tr, d), row_block),
                 pl.BlockSpec((1, tr, d), row_block),
                 pl.BlockSpec((1, CONV_WIDTH - 1, d), per_seq)]
    out_shape = [jax.ShapeDtypeStruct((b, t, d), BF16),
                 jax.ShapeDtypeStruct((b, t, d), BF16),
                 jax.ShapeDtypeStruct((b, CONV_WIDTH - 1, d), F32)]
    cast_blocks = 0
    if cast_f32 is not None:
        c_in, c_out, c_shape, cast_blocks = _cast_specs(
            cast_f32, cast_col0, cast_f32.shape[2] - cast_col0, b * nr, lambda i, r: i * nr + r)
        in_specs.append(c_in)
        args.append(cast_f32)
        out_specs.append(c_out)
        out_shape.append(c_shape)
    return pl.pallas_call(
        functools.partial(_conv_rows_body, mc=mc, has_hist=has_hist, cast_blocks=cast_blocks),
        grid=(b, nr),
        in_specs=in_specs,
        out_specs=out_specs,
        out_shape=out_shape,
        scratch_shapes=[pltpu.VMEM((d // cw, mc + SUBLANES, cw), F32)],
        compiler_params=pltpu.CompilerParams(dimension_semantics=("arbitrary", "arbitrary"),
                                             vmem_limit_bytes=CONV_ROWS_VMEM_LIMIT),
        name="conv_rows",
    )(*args)


def _conv_seqs_body(*refs, sb, ts, has_hist):
    if has_hist:
        (h_ref, wb_ref, wc_ref, wu_ref, wz_ref, cw_ref, cb_ref, hist_ref,
         ya_ref, nh_ref, ubuf, pbuf) = refs
    else:
        (h_ref, wb_ref, wc_ref, wu_ref, wz_ref, cw_ref, cb_ref,
         ya_ref, nh_ref, ubuf, pbuf) = refs
    taps = (cw_ref[0:1, :], cw_ref[1:2, :], cw_ref[2:3, :], cb_ref[...])
    top = SUBLANES
    hr = h_ref[...].reshape(sb * ts, h_ref.shape[2])
    for k, w_ref in enumerate((wb_ref, wc_ref, wu_ref, wz_ref)):
        pbuf[k] = _dot(hr, w_ref[...])
    for s in range(sb):
        ubuf[top - 2:top, :] = hist_ref[s] if has_hist else jnp.zeros((2, ubuf.shape[1]), F32)
        rows = slice(s * ts, (s + 1) * ts)
        ya_ref[s] = _conv_epilogue(ubuf, ts, taps, pbuf[0, rows, :], pbuf[1, rows, :],
                                   pbuf[2, rows, :], pbuf[3, rows, :])
        nh_ref[s] = ubuf[top - 2:top, :]


def _conv_seqs(h, w_in, conv_w, conv_b, hist, sb):
    b, t, d = h.shape
    cw = COL_BLOCK
    ng = d // cw
    has_hist = hist is not None
    in_specs = [pl.BlockSpec((sb, t, d), lambda i, g: (i, 0, 0))]
    in_specs += [pl.BlockSpec((d, cw), functools.partial(lambda i, g, k: (0, k * ng + g), k=k))
                 for k in range(4)]
    in_specs += [pl.BlockSpec((CONV_WIDTH, cw), lambda i, g: (0, g)),
                 pl.BlockSpec((1, cw), lambda i, g: (0, g))]
    args = [h, w_in, w_in, w_in, w_in, conv_w, conv_b.reshape(1, d)]
    if has_hist:
        in_specs.append(pl.BlockSpec((sb, CONV_WIDTH - 1, cw), lambda i, g: (i, 0, g)))
        args.append(hist)
    return pl.pallas_call(
        functools.partial(_conv_seqs_body, sb=sb, ts=t, has_hist=has_hist),
        grid=(b // sb, ng),
        in_specs=in_specs,
        out_specs=[pl.BlockSpec((sb, t, cw), lambda i, g: (i, 0, g)),
                   pl.BlockSpec((sb, CONV_WIDTH - 1, cw), lambda i, g: (i, 0, g))],
        out_shape=[jax.ShapeDtypeStruct((b, t, d), BF16),
                   jax.ShapeDtypeStruct((b, CONV_WIDTH - 1, d), F32)],
        scratch_shapes=[pltpu.VMEM((t + SUBLANES, cw), F32),
                        pltpu.VMEM((4, sb * t, cw), F32)],
        compiler_params=_params(("arbitrary", "arbitrary")),
        name="conv_seqs",
    )(*args)


def _ret_body(*refs, sb, ts, mc, tt, has_state, cast_blocks):
    it = iter(refs)
    (h_ref, wq_ref, wk_ref, wv_ref, wz_ref, rq_ref, rk_ref, mask_ref, qd_ref, kd_ref,
     bd_ref) = (next(it) for _ in range(11))
    st_ref = next(it) if has_state else None
    wf_ref = next(it) if cast_blocks else None
    yb_ref, ns_ref = next(it), next(it)
    wr_ref = next(it) if cast_blocks else None
    pbuf = next(it)
    if cast_blocks:
        _side_cast(wf_ref, wr_ref, cast_blocks)
    dk = LANES
    dv = 2 * LANES
    hpb = 2

    def tile(s, r0, q2, k2, v2, z2, cq, sq, ck, sk):
        for hh in range(hpb):
            q = q2[:, hh * dk:(hh + 1) * dk]
            k = k2[:, hh * dk:(hh + 1) * dk]
            v = v2[:, hh * dv:(hh + 1) * dv].astype(BF16)
            z = z2[:, hh * dv:(hh + 1) * dv]
            qr = q * cq + pltpu.roll(q, dk // 2, axis=1) * sq
            kr = k * ck + pltpu.roll(k, dk // 2, axis=1) * sk
            state = ns_ref[s, hh]
            scores = lax.dot_general(qr.astype(BF16), kr.astype(BF16),
                                     (((1,), (1,)), ((), ())), preferred_element_type=F32)
            p = (scores * mask_ref[hh]).astype(BF16)
            o = _dot(p, v) + _dot((qr * qd_ref[hh]).astype(BF16), state.astype(BF16))
            kv = lax.dot_general((kr * kd_ref[hh]).astype(BF16), v,
                                 (((0,), (0,)), ((), ())), preferred_element_type=F32)
            ns_ref[s, hh] = bd_ref[hh] * state + kv
            on = o * lax.rsqrt(jnp.mean(o * o, axis=-1, keepdims=True) + EPS)
            yb_ref[s, pl.ds(r0, tt), hh * dv:(hh + 1) * dv] = (on * _silu(z)).astype(BF16)

    def projections(hr):
        return (_dot(hr, wq_ref[...]), _dot(hr, wk_ref[...]),
                _dot(hr, wv_ref[...]), _dot(hr, wz_ref[...]))

    if has_state:
        ns_ref[...] = st_ref[...]
    else:
        ns_ref[...] = jnp.zeros(ns_ref.shape, F32)

    if mc <= ts:
        assert mc == tt
        for s in range(sb):
            def chunk(i, carry, s=s):
                r0 = pl.multiple_of(i * mc, mc)
                q2, k2, v2, z2 = projections(h_ref[s, pl.ds(r0, mc), :])
                rows = pl.ds(r0, mc)
                tile(s, r0, q2, k2, v2, z2, rq_ref[0, rows, :], rq_ref[1, rows, :],
                     rk_ref[0, rows, :], rk_ref[1, rows, :])
                return carry

            lax.fori_loop(0, ts // mc, chunk, 0, unroll=CHUNK_UNROLL)
    else:
        assert tt == ts
        hr = h_ref[...].reshape(sb * ts, h_ref.shape[2])
        for k, p in enumerate(projections(hr)):
            pbuf[k, :, 0:p.shape[1]] = p
        for s in range(sb):
            rows = slice(s * ts, (s + 1) * ts)
            tile(s, 0, pbuf[0, rows, 0:hpb * dk], pbuf[1, rows, 0:hpb * dk],
                 pbuf[2, rows, :], pbuf[3, rows, :],
                 rq_ref[0], rq_ref[1], rk_ref[0], rk_ref[1])


def _retention_tables(t, tt, pos0, chunk):
    dk = LANES
    inv = ROPE_BASE ** (-jnp.arange(0, dk, 2, dtype=F32) / dk)
    ang = (pos0 + jnp.arange(t)).astype(F32)[:, None] * inv[None, :]
    cos, sin = jnp.cos(ang), jnp.sin(ang)
    rope_q = jnp.stack([jnp.concatenate([cos, cos], -1), jnp.concatenate([-sin, sin], -1)])
    rope_k = rope_q * (dk ** -0.5)
    lg = jnp.log(1.0 - 2.0 ** (-5.0 - jnp.arange(N_HEADS, dtype=F32)))
    idx = jnp.arange(tt, dtype=F32)
    visible = (jnp.arange(tt)[None, :] // chunk) <= (jnp.arange(tt)[:, None] // chunk)
    mask = jnp.exp(lg[:, None, None] * jnp.abs(idx[:, None] - idx[None, :])) * visible.astype(F32)
    q_dec = jnp.broadcast_to(jnp.exp(lg[:, None] * (idx + 1.0))[..., None], (N_HEADS, tt, dk))
    k_dec = jnp.broadcast_to(jnp.exp(lg[:, None] * (tt - 1.0 - idx))[..., None], (N_HEADS, tt, dk))
    b_dec = jnp.broadcast_to(jnp.exp(lg * tt)[:, None, None], (N_HEADS, 1, 2 * dk))
    return rope_q, rope_k, mask, q_dec, k_dec, b_dec


def _ret_branch(h, w_in, state, pos0, sb, mc, cast_f32=None):
    b, t, d = h.shape
    dk, dv, hpb = LANES, 2 * LANES, 2
    nhp = N_HEADS // hpb
    tt = min(RET_TILE, t)
    chunk = min(REF_CHUNK, t)
    has_state = state is not None
    rope_q, rope_k, mask, q_dec, k_dec, b_dec = _retention_tables(t, tt, pos0, chunk)
    q_off = 0
    k_off = q_off + nhp
    v_off = 2 * N_HEADS * dk // (hpb * dv)
    z_off = v_off + nhp
    in_specs = [pl.BlockSpec((sb, t, d), lambda i, g: (i, 0, 0)),
                pl.BlockSpec((d, hpb * dk), lambda i, g: (0, q_off + g)),
                pl.BlockSpec((d, hpb * dk), lambda i, g: (0, k_off + g)),
                pl.BlockSpec((d, hpb * dv), lambda i, g: (0, v_off + g)),
                pl.BlockSpec((d, hpb * dv), lambda i, g: (0, z_off + g)),
                pl.BlockSpec((2, t, dk), lambda i, g: (0, 0, 0), pipeline_mode=pl.Buffered(1)),
                pl.BlockSpec((2, t, dk), lambda i, g: (0, 0, 0), pipeline_mode=pl.Buffered(1)),
                pl.BlockSpec((hpb, tt, tt), lambda i, g: (g, 0, 0)),
                pl.BlockSpec((hpb, tt, dk), lambda i, g: (g, 0, 0)),
                pl.BlockSpec((hpb, tt, dk), lambda i, g: (g, 0, 0)),
                pl.BlockSpec((hpb, 1, dv), lambda i, g: (g, 0, 0))]
    args = [h, w_in, w_in, w_in, w_in, rope_q, rope_k, mask, q_dec, k_dec, b_dec]
    if has_state:
        in_specs.append(pl.BlockSpec((sb, hpb, dk, dv), lambda i, g: (i, g, 0, 0)))
        args.append(state)
    out_specs = [pl.BlockSpec((sb, t, hpb * dv), lambda i, g: (i, 0, g)),
                 pl.BlockSpec((sb, hpb, dk, dv), lambda i, g: (i, g, 0, 0))]
    out_shape = [jax.ShapeDtypeStruct((b, t, N_HEADS * dv), BF16),
                 jax.ShapeDtypeStruct((b, N_HEADS, dk, dv), F32)]
    cast_blocks = 0
    if cast_f32 is not None:
        c_in, c_out, c_shape, cast_blocks = _cast_specs(cast_f32, 0, cast_f32.shape[2], (b // sb) * nhp,
                                                        lambda i, g: i * nhp + g)
        in_specs.append(c_in)
        args.append(cast_f32)
        out_specs.append(c_out)
        out_shape.append(c_shape)
    pbuf_rows = sb * t if mc > t else SUBLANES
    return pl.pallas_call(
        functools.partial(_ret_body, sb=sb, ts=t, mc=mc, tt=tt, has_state=has_state,
                          cast_blocks=cast_blocks),
        grid=(b // sb, nhp),
        in_specs=in_specs,
        out_specs=out_specs,
        out_shape=out_shape,
        scratch_shapes=[pltpu.VMEM((4, pbuf_rows, hpb * dv), F32)],
        compiler_params=_params(("arbitrary", "arbitrary")),
        name="ret_branch",
    )(*args)


def _merge_body(*refs, mc, cast_blocks):
    if cast_blocks:
        h_ref, ya_ref, yb_ref, wma_ref, wmb_ref, wa_ref, wb_ref, wf_ref, o_ref, wr_ref = refs
        _side_cast(wf_ref, wr_ref, cast_blocks)
    else:
        h_ref, ya_ref, yb_ref, wma_ref, wmb_ref, wa_ref, wb_ref, o_ref = refs

    def chunk(i, carry):
        r0 = pl.multiple_of(i * mc, mc)
        rows = pl.ds(r0, mc)
        hr = h_ref[rows, :]
        ma = _sigmoid(_dot(hr, wma_ref[...])) * _dot(ya_ref[rows, :], wa_ref[0])
        mb = _sigmoid(_dot(hr, wmb_ref[...])) * _dot(yb_ref[rows, :], wb_ref[0])
        o_ref[rows, :] = (ma + mb).astype(BF16)
        return carry

    lax.fori_loop(0, h_ref.shape[0] // mc, chunk, 0, unroll=CHUNK_UNROLL)


def _merge(h2, ya2, yb2, w_in, w_branch, tm, mc, cast_f32=None):
    n, d = h2.shape
    nb = 512
    nblk = d // nb
    ma_off = (w_in.shape[1] - 2 * d) // nb
    mb_off = ma_off + nblk
    act = pl.BlockSpec((tm, d), lambda i, j: (i, 0))
    in_specs = [act, act, act,
                pl.BlockSpec((d, nb), lambda i, j: (0, ma_off + j)),
                pl.BlockSpec((d, nb), lambda i, j: (0, mb_off + j)),
                pl.BlockSpec((1, d, nb), lambda i, j: (0, 0, j)),
                pl.BlockSpec((1, d, nb), lambda i, j: (1, 0, j))]
    args = [h2, ya2, yb2, w_in, w_in, w_branch, w_branch]
    out_specs = [pl.BlockSpec((tm, nb), lambda i, j: (i, j))]
    out_shape = [jax.ShapeDtypeStruct((n, d), BF16)]
    cast_blocks = 0
    if cast_f32 is not None:
        c_in, c_out, c_shape, cast_blocks = _cast_specs(cast_f32, 0, cast_f32.shape[2], (n // tm) * nblk,
                                                        lambda i, j: i * nblk + j)
        in_specs.append(c_in)
        args.append(cast_f32)
        out_specs.append(c_out)
        out_shape.append(c_shape)
    return pl.pallas_call(
        functools.partial(_merge_body, mc=mc, cast_blocks=cast_blocks),
        grid=(n // tm, nblk),
        in_specs=in_specs,
        out_specs=out_specs,
        out_shape=out_shape,
        compiler_params=_params(("arbitrary", "arbitrary")),
        name="merge",
    )(*args)


def _out_body(m_ref, w_ref, x_ref, mod_ref, g_ref, y_ref, obuf, *, sb, ts, mc):
    g = g_ref[...]

    def finish(o, x, gate):
        ms = jnp.mean(o * o, axis=-1, keepdims=True)
        return x + gate * (o * lax.rsqrt(ms + EPS) * g)

    if mc <= ts:
        for s in range(sb):
            gate = mod_ref[s, 2:3, :]

            def chunk(i, carry, s=s, gate=gate):
                rows = pl.ds(pl.multiple_of(i * mc, mc), mc)
                o = _dot(m_ref[s, rows, :], w_ref[...])
                y_ref[s, rows, :] = finish(o, x_ref[s, rows, :], gate)
                return carry

            lax.fori_loop(0, ts // mc, chunk, 0, unroll=CHUNK_UNROLL)
    else:
        obuf[...] = _dot(m_ref[...].reshape(sb * ts, m_ref.shape[2]), w_ref[...])
        for s in range(sb):
            y_ref[s] = finish(obuf[s * ts:(s + 1) * ts, :], x_ref[s], mod_ref[s, 2:3, :])


def _out_proj(merged, w_out, x, mod3, g_post, sb, tr, mc):
    b, t, d = x.shape
    obuf_rows = sb * tr if mc > tr else SUBLANES
    act = lambda i, j: (i, j, 0)
    return pl.pallas_call(
        functools.partial(_out_body, sb=sb, ts=tr, mc=mc),
        grid=(b // sb, t // tr),
        in_specs=[pl.BlockSpec((sb, tr, d), act),
                  pl.BlockSpec((d, d), lambda i, j: (0, 0), pipeline_mode=pl.Buffered(1)),
                  pl.BlockSpec((sb, tr, d), act),
                  pl.BlockSpec((sb, 3, d), lambda i, j: (i, 0, 0)),
                  pl.BlockSpec((1, d), lambda i, j: (0, 0))],
        out_specs=pl.BlockSpec((sb, tr, d), act),
        out_shape=jax.ShapeDtypeStruct((b, t, d), F32),
        scratch_shapes=[pltpu.VMEM((obuf_rows, d), F32)],
        compiler_params=_params(("arbitrary", "arbitrary")),
        name="out_proj",
    )(merged, w_out, x, mod3, g_post.reshape(1, d))


class _LayerWeights:
    def __init__(self, w_in, w_branch, w_out, conv_cols):
        self.w_in, self.w_branch, self.w_out = w_in, w_branch, w_out
        self.conv = w_in[:, :conv_cols].astype(BF16)
        self.rest = self.branch = self.out = None


def _mixer_layer(x, mod3, hist, state, pos0, g_pre, g_post, conv_w, conv_b, w):
    b, t, d = x.shape
    long_seq = t >= ROW_CHUNK
    sb = 1 if long_seq else b
    tr = 512 if long_seq else t
    mc = ROW_CHUNK if long_seq else b * t
    conv_cols = w.conv.shape[1]
    if long_seq:
        outs = _conv_rows(x, mod3, g_pre, w.conv, conv_w, conv_b, hist, tr, ROW_CHUNK,
                          cast_f32=w.w_in[None] if w.rest is None else None, cast_col0=conv_cols)
        h, ya, new_hist = outs[:3]
        if w.rest is None:
            w.rest = outs[3][0]
    else:
        if w.rest is None:
            w.rest = w.w_in[:, conv_cols:].astype(BF16)
        h = _prenorm(x, mod3, g_pre, sb, tr)
        ya, new_hist = _conv_seqs(h, w.conv, conv_w, conv_b, hist, sb)
    outs = _ret_branch(h, w.rest, state, pos0, sb, mc,
                       cast_f32=w.w_branch if long_seq and w.branch is None else None)
    yb, new_state = outs[:2]
    if w.branch is None:
        w.branch = outs[2] if long_seq else w.w_branch.astype(BF16)
    n = b * t
    tm = 1024 if long_seq else n
    outs = _merge(h.reshape(n, d), ya.reshape(n, d), yb.reshape(n, d), w.rest, w.branch, tm, ROW_CHUNK,
                  cast_f32=w.w_out[None] if long_seq and w.out is None else None)
    if w.out is None:
        w.out = outs[1][0] if long_seq else w.w_out.astype(BF16)
    y = _out_proj(outs[0].reshape(b, t, d), w.out, x, mod3, g_post, sb, 2 * tr if long_seq else tr, mc)
    return y, new_hist, new_state


def kernel(x_prompt, x_sample, c_prompt, c_sample, state_conv, state_ret, ada_w, ada_b, norm_pre,
           norm_post, w_in, conv_w, conv_b, w_branch, w_out):
    depth = w_in.shape[0]
    bp = x_prompt.shape[0]
    d = x_prompt.shape[2]
    hp, hs = x_prompt, x_sample
    conv_p, ret_p, conv_s, ret_s = [], [], [], []
    c_all = jnp.concatenate([c_prompt, c_sample], axis=0)
    for l in range(depth):
        mod3 = _modulation(c_all, ada_w[l], ada_b[l]).reshape(c_all.shape[0], 3, d)
        shared = (norm_pre[l], norm_post[l], conv_w[l], conv_b[l],
                  _LayerWeights(w_in[l], w_branch[l], w_out[l], conv_cols=4 * d))
        hp, cp, rp = _mixer_layer(hp, mod3[:bp], None, None, 0, *shared)
        hs, cs, rs = _mixer_layer(hs, mod3[bp:], state_conv[l], state_ret[l], PAST_LEN, *shared)
        conv_p.append(cp)
        ret_p.append(rp)
        conv_s.append(cs)
        ret_s.append(rs)
    return (hp, hs, jnp.stack(conv_p), jnp.stack(ret_p), jnp.stack(conv_s), jnp.stack(ret_s))
```

```python
import functools

import jax
import jax.numpy as jnp
from jax import lax
from jax.experimental import pallas as pl
from jax.experimental.pallas import tpu as pltpu

F32 = jnp.float32
BF16 = jnp.bfloat16

EPS = 1e-6
ROPE_BASE = 10000.0
REF_CHUNK = 64
N_HEADS = 8
CONV_WIDTH = 3

LANES = 128
SUBLANES = 8
COL_BLOCK = 256
ROW_CHUNK = 256
RET_TILE = 256
CHUNK_UNROLL = True
PAST_LEN = 1024
VMEM_LIMIT = 56 * 1024 * 1024
CONV_ROWS_VMEM_LIMIT = 60 * 1024 * 1024


def _dot(a, b):
    return jnp.dot(a, b, preferred_element_type=F32)


def _sigmoid(x):
    return 1.0 / (1.0 + jnp.exp(-x))


def _silu(x):
    return x * _sigmoid(x)


def _params(semantics):
    return pltpu.CompilerParams(dimension_semantics=semantics, vmem_limit_bytes=VMEM_LIMIT)


def _cast_specs(w_f32, col0, cols, n_steps, step_of):
    m, d, _ = w_f32.shape
    per = cols // COL_BLOCK
    assert cols % COL_BLOCK == 0 and col0 % COL_BLOCK == 0 and m * per <= n_steps

    def block(*ids):
        return jnp.minimum(step_of(*ids), m * per - 1)

    in_spec = pl.BlockSpec((1, d, COL_BLOCK),
                           lambda *ids: (block(*ids) // per, 0, col0 // COL_BLOCK + block(*ids) % per))
    out_spec = pl.BlockSpec((1, d, COL_BLOCK), lambda *ids: (block(*ids) // per, 0, block(*ids) % per))
    return in_spec, out_spec, jax.ShapeDtypeStruct((m, d, cols), BF16), m * per


def _cast_body(wf_ref, wr_ref):
    wr_ref[...] = wf_ref[...].astype(BF16)


def _cast_cols(w_f32, cols):
    d = w_f32.shape[0]
    nb = 2 * COL_BLOCK
    return pl.pallas_call(
        _cast_body,
        grid=(cols // nb,),
        in_specs=[pl.BlockSpec((d, nb), lambda j: (0, j))],
        out_specs=pl.BlockSpec((d, nb), lambda j: (0, j)),
        out_shape=jax.ShapeDtypeStruct((d, cols), BF16),
        compiler_params=_params(("arbitrary",)),
        name="cast_cols",
    )(w_f32)


def _side_cast(wf_ref, wr_ref, n_blocks):
    step = pl.program_id(0) * pl.num_programs(1) + pl.program_id(1)

    @pl.when(step < n_blocks)
    def _():
        wr_ref[...] = wf_ref[...].astype(BF16)


def _mod_body(c_ref, w_ref, b_ref, o_ref):
    a = _silu(c_ref[...]).astype(BF16)
    o_ref[...] = _dot(a, w_ref[...].astype(BF16)) + b_ref[...]


def _modulation(c_all, ada_w, ada_b):
    nseq, d = c_all.shape
    n_out = ada_w.shape[1]
    nb = 768
    return pl.pallas_call(
        _mod_body,
        grid=(n_out // nb,),
        in_specs=[pl.BlockSpec((nseq, d), lambda j: (0, 0)),
                  pl.BlockSpec((d, nb), lambda j: (0, j)),
                  pl.BlockSpec((1, nb), lambda j: (0, j))],
        out_specs=pl.BlockSpec((nseq, nb), lambda j: (0, j)),
        out_shape=jax.ShapeDtypeStruct((nseq, n_out), F32),
        compiler_params=_params(("arbitrary",)),
        name="modulation",
    )(c_all, ada_w, ada_b.reshape(1, n_out))


def _prenorm_body(x_ref, mod_ref, g_ref, h_ref):
    x = x_ref[...]
    ms = jnp.mean(x * x, axis=-1, keepdims=True)
    y = x * lax.rsqrt(ms + EPS) * g_ref[...]
    shift = mod_ref[:, 0:1, :]
    scale = mod_ref[:, 1:2, :]
    h_ref[...] = (y * (1.0 + scale) + shift).astype(BF16)


def _prenorm(x, mod3, g_pre, sb, tr):
    b, t, d = x.shape
    return pl.pallas_call(
        _prenorm_body,
        grid=(b // sb, t // tr),
        in_specs=[pl.BlockSpec((sb, tr, d), lambda i, j: (i, j, 0)),
                  pl.BlockSpec((sb, 3, d), lambda i, j: (i, 0, 0)),
                  pl.BlockSpec((1, d), lambda i, j: (0, 0))],
        out_specs=pl.BlockSpec((sb, tr, d), lambda i, j: (i, j, 0)),
        out_shape=jax.ShapeDtypeStruct((b, t, d), BF16),
        compiler_params=_params(("arbitrary", "arbitrary")),
        name="prenorm",
    )(x, mod3, g_pre.reshape(1, d))


def _conv_epilogue(ubuf, ec, taps, gb, gc, gu, zc):
    w0, w1, w2, cb = taps
    top = SUBLANES
    u = gc * gu
    ubuf[top:top + ec, :] = u
    f0 = ubuf[top - 2:top - 2 + ec, :]
    f1 = ubuf[top - 1:top - 1 + ec, :]
    conv = cb + w0 * f0 + w1 * f1 + w2 * u
    ubuf[top - 2:top, :] = ubuf[top + ec - 2:top + ec, :]
    return (gb * conv * _silu(zc)).astype(BF16)


def _conv_rows_body(*refs, mc, has_hist, cast_blocks):
    it = iter(refs)
    x_ref, mod_ref, g_ref, wb_ref, wc_ref, wu_ref, wz_ref, cw_ref, cb_ref = (next(it) for _ in range(9))
    hist_ref = next(it) if has_hist else None
    wf_ref = next(it) if cast_blocks else None
    h_ref, ya_ref, nh_ref = next(it), next(it), next(it)
    wr_ref = next(it) if cast_blocks else None
    ubuf = next(it)
    if cast_blocks:
        _side_cast(wf_ref, wr_ref, cast_blocks)
    tr, d = x_ref.shape[1], x_ref.shape[2]
    cw = ubuf.shape[2]
    ng = d // cw
    top = SUBLANES
    r = pl.program_id(1)

    @pl.when(r == 0)
    def _():
        for gi in range(ng):
            cols = slice(gi * cw, (gi + 1) * cw)
            ubuf[gi, top - 2:top, :] = hist_ref[0, :, cols] if has_hist else jnp.zeros((2, cw), F32)

    g_pre = g_ref[...]
    shift = mod_ref[0, 0:1, :]
    scale1 = 1.0 + mod_ref[0, 1:2, :]
    for sub in range(tr // mc):
        rows = slice(sub * mc, (sub + 1) * mc)
        x = x_ref[0, rows, :]
        ms = jnp.mean(x * x, axis=-1, keepdims=True)
        h_ref[0, rows, :] = (x * lax.rsqrt(ms + EPS) * g_pre * scale1 + shift).astype(BF16)
        hr = h_ref[0, rows, :]
        for gi in range(ng):
            cols = slice(gi * cw, (gi + 1) * cw)
            taps = (cw_ref[0:1, cols], cw_ref[1:2, cols], cw_ref[2:3, cols], cb_ref[:, cols])
            ya_ref[0, rows, cols] = _conv_epilogue(
                ubuf.at[gi], mc, taps, gc=_dot(hr, wc_ref[:, cols]), gu=_dot(hr, wu_ref[:, cols]),
                zc=_dot(hr, wz_ref[:, cols]), gb=_dot(hr, wb_ref[:, cols]))

    @pl.when(r == pl.num_programs(1) - 1)
    def _():
        for gi in range(ng):
            nh_ref[0, :, gi * cw:(gi + 1) * cw] = ubuf[gi, top - 2:top, :]


def _conv_rows(x, mod3, g_pre, w_conv, conv_w, conv_b, hist, tr, mc, cast_f32=None, cast_col0=0):
    b, t, d = x.shape
    cw = COL_BLOCK
    nr = t // tr
    has_hist = hist is not None
    row_block = lambda i, r: (i, r, 0)
    per_seq = lambda i, r: (i, 0, 0)
    whole = lambda i, r: (0, 0)
    in_specs = [pl.BlockSpec((1, tr, d), row_block),
                pl.BlockSpec((1, 3, d), per_seq),
                pl.BlockSpec((1, d), whole)]
    in_specs += [pl.BlockSpec((d, d), functools.partial(lambda i, r, k: (0, k), k=k),
                              pipeline_mode=pl.Buffered(1)) for k in range(4)]
    in_specs += [pl.BlockSpec((CONV_WIDTH, d), whole), pl.BlockSpec((1, d), whole)]
    args = [x, mod3, g_pre.reshape(1, d), w_conv, w_conv, w_conv, w_conv, conv_w, conv_b.reshape(1, d)]
    if has_hist:
        in_specs.append(pl.BlockSpec((1, CONV_WIDTH - 1, d), per_seq))
        args.append(hist)
    out_specs = [pl.BlockSpec((1, tr, d), row_block),
                 pl.BlockSpec((1, tr, d), row_block),
                 pl.BlockSpec((1, CONV_WIDTH - 1, d), per_seq)]
    out_shape = [jax.ShapeDtypeStruct((b, t, d), BF16),
                 jax.ShapeDtypeStruct((b, t, d), BF16),
                 jax.ShapeDtypeStruct((b, CONV_WIDTH - 1, d), F32)]
    cast_blocks = 0
    if cast_f32 is not None:
        c_in, c_out, c_shape, cast_blocks = _cast_specs(
            cast_f32, cast_col0, cast_f32.shape[2] - cast_col0, b * nr, lambda i, r: i * nr + r)
        in_specs.append(c_in)
        args.append(cast_f32)
        out_specs.append(c_out)
        out_shape.append(c_shape)
    return pl.pallas_call(
        functools.partial(_conv_rows_body, mc=mc, has_hist=has_hist, cast_blocks=cast_blocks),
        grid=(b, nr),
        in_specs=in_specs,
        out_specs=out_specs,
        out_shape=out_shape,
        scratch_shapes=[pltpu.VMEM((d // cw, mc + SUBLANES, cw), F32)],
        compiler_params=pltpu.CompilerParams(dimension_semantics=("arbitrary", "arbitrary"),
                                             vmem_limit_bytes=CONV_ROWS_VMEM_LIMIT),
        name="conv_rows",
    )(*args)


def _conv_seqs_body(*refs, sb, ts, has_hist):
    if has_hist:
        (h_ref, wb_ref, wc_ref, wu_ref, wz_ref, cw_ref, cb_ref, hist_ref,
         ya_ref, nh_ref, ubuf, pbuf) = refs
    else:
        (h_ref, wb_ref, wc_ref, wu_ref, wz_ref, cw_ref, cb_ref,
         ya_ref, nh_ref, ubuf, pbuf) = refs
    taps = (cw_ref[0:1, :], cw_ref[1:2, :], cw_ref[2:3, :], cb_ref[...])
    top = SUBLANES
    hr = h_ref[...].reshape(sb * ts, h_ref.shape[2])
    for k, w_ref in enumerate((wb_ref, wc_ref, wu_ref, wz_ref)):
        pbuf[k] = _dot(hr, w_ref[...])
    for s in range(sb):
        ubuf[top - 2:top, :] = hist_ref[s] if has_hist else jnp.zeros((2, ubuf.shape[1]), F32)
        rows = slice(s * ts, (s + 1) * ts)
        ya_ref[s] = _conv_epilogue(ubuf, ts, taps, pbuf[0, rows, :], pbuf[1, rows, :],
                                   pbuf[2, rows, :], pbuf[3, rows, :])
        nh_ref[s] = ubuf[top - 2:top, :]


def _conv_seqs(h, w_in, conv_w, conv_b, hist, sb):
    b, t, d = h.shape
    cw = COL_BLOCK
    ng = d // cw
    has_hist = hist is not None
    in_specs = [pl.BlockSpec((sb, t, d), lambda i, g: (i, 0, 0))]
    in_specs += [pl.BlockSpec((d, cw), functools.partial(lambda i, g, k: (0, k * ng + g), k=k))
                 for k in range(4)]
    in_specs += [pl.BlockSpec((CONV_WIDTH, cw), lambda i, g: (0, g)),
                 pl.BlockSpec((1, cw), lambda i, g: (0, g))]
    args = [h, w_in, w_in, w_in, w_in, conv_w, conv_b.reshape(1, d)]
    if has_hist:
        in_specs.append(pl.BlockSpec((sb, CONV_WIDTH - 1, cw), lambda i, g: (i, 0, g)))
        args.append(hist)
    return pl.pallas_call(
        functools.partial(_conv_seqs_body, sb=sb, ts=t, has_hist=has_hist),
        grid=(b // sb, ng),
        in_specs=in_specs,
        out_specs=[pl.BlockSpec((sb, t, cw), lambda i, g: (i, 0, g)),
                   pl.BlockSpec((sb, CONV_WIDTH - 1, cw), lambda i, g: (i, 0, g))],
        out_shape=[jax.ShapeDtypeStruct((b, t, d), BF16),
                   jax.ShapeDtypeStruct((b, CONV_WIDTH - 1, d), F32)],
        scratch_shapes=[pltpu.VMEM((t + SUBLANES, cw), F32),
                        pltpu.VMEM((4, sb * t, cw), F32)],
        compiler_params=_params(("arbitrary", "arbitrary")),
        name="conv_seqs",
    )(*args)


def _ret_body(*refs, sb, ts, mc, tt, has_state, cast_blocks):
    it = iter(refs)
    (h_ref, wq_ref, wk_ref, wv_ref, wz_ref, rq_ref, rk_ref, mask_ref, qd_ref, kd_ref,
     bd_ref) = (next(it) for _ in range(11))
    st_ref = next(it) if has_state else None
    wf_ref = next(it) if cast_blocks else None
    yb_ref, ns_ref = next(it), next(it)
    wr_ref = next(it) if cast_blocks else None
    pbuf = next(it)
    if cast_blocks:
        _side_cast(wf_ref, wr_ref, cast_blocks)
    dk = LANES
    dv = 2 * LANES
    hpb = 2

    def tile(s, r0, q2, k2, v2, z2, cq, sq, ck, sk):
        for hh in range(hpb):
            q = q2[:, hh * dk:(hh + 1) * dk]
            k = k2[:, hh * dk:(hh + 1) * dk]
            v = v2[:, hh * dv:(hh + 1) * dv].astype(BF16)
            z = z2[:, hh * dv:(hh + 1) * dv]
            qr = q * cq + pltpu.roll(q, dk // 2, axis=1) * sq
            kr = k * ck + pltpu.roll(k, dk // 2, axis=1) * sk
            state = ns_ref[s, hh]
            scores = lax.dot_general(qr.astype(BF16), kr.astype(BF16),
                                     (((1,), (1,)), ((), ())), preferred_element_type=F32)
            p = (scores * mask_ref[hh]).astype(BF16)
            o = _dot(p, v) + _dot((qr * qd_ref[hh]).astype(BF16), state.astype(BF16))
            kv = lax.dot_general((kr * kd_ref[hh]).astype(BF16), v,
                                 (((0,), (0,)), ((), ())), preferred_element_type=F32)
            ns_ref[s, hh] = bd_ref[hh] * state + kv
            on = o * lax.rsqrt(jnp.mean(o * o, axis=-1, keepdims=True) + EPS)
            yb_ref[s, pl.ds(r0, tt), hh * dv:(hh + 1) * dv] = (on * _silu(z)).astype(BF16)

    def projections(hr):
        return (_dot(hr, wq_ref[...]), _dot(hr, wk_ref[...]),
                _dot(hr, wv_ref[...]), _dot(hr, wz_ref[...]))

    if has_state:
        ns_ref[...] = st_ref[...]
    else:
        ns_ref[...] = jnp.zeros(ns_ref.shape, F32)

    if mc <= ts:
        assert mc == tt
        for s in range(sb):
            def chunk(i, carry, s=s):
                r0 = pl.multiple_of(i * mc, mc)
                q2, k2, v2, z2 = projections(h_ref[s, pl.ds(r0, mc), :])
                rows = pl.ds(r0, mc)
                tile(s, r0, q2, k2, v2, z2, rq_ref[0, rows, :], rq_ref[1, rows, :],
                     rk_ref[0, rows, :], rk_ref[1, rows, :])
                return carry

            lax.fori_loop(0, ts // mc, chunk, 0, unroll=CHUNK_UNROLL)
    else:
        assert tt == ts
        hr = h_ref[...].reshape(sb * ts, h_ref.shape[2])
        for k, p in enumerate(projections(hr)):
            pbuf[k, :, 0:p.shape[1]] = p
        for s in range(sb):
            rows = slice(s * ts, (s + 1) * ts)
            tile(s, 0, pbuf[0, rows, 0:hpb * dk], pbuf[1, rows, 0:hpb * dk],
                 pbuf[2, rows, :], pbuf[3, rows, :],
                 rq_ref[0], rq_ref[1], rk_ref[0], rk_ref[1])


def _retention_tables(t, tt, pos0, chunk):
    dk = LANES
    inv = ROPE_BASE ** (-jnp.arange(0, dk, 2, dtype=F32) / dk)
    ang = (pos0 + jnp.arange(t)).astype(F32)[:, None] * inv[None, :]
    cos, sin = jnp.cos(ang), jnp.sin(ang)
    rope_q = jnp.stack([jnp.concatenate([cos, cos], -1), jnp.concatenate([-sin, sin], -1)])
    rope_k = rope_q * (dk ** -0.5)
    lg = jnp.log(1.0 - 2.0 ** (-5.0 - jnp.arange(N_HEADS, dtype=F32)))
    idx = jnp.arange(tt, dtype=F32)
    visible = (jnp.arange(tt)[None, :] // chunk) <= (jnp.arange(tt)[:, None] // chunk)
    mask = jnp.exp(lg[:, None, None] * jnp.abs(idx[:, None] - idx[None, :])) * visible.astype(F32)
    q_dec = jnp.broadcast_to(jnp.exp(lg[:, None] * (idx + 1.0))[..., None], (N_HEADS, tt, dk))
    k_dec = jnp.broadcast_to(jnp.exp(lg[:, None] * (tt - 1.0 - idx))[..., None], (N_HEADS, tt, dk))
    b_dec = jnp.broadcast_to(jnp.exp(lg * tt)[:, None, None], (N_HEADS, 1, 2 * dk))
    return rope_q, rope_k, mask, q_dec, k_dec, b_dec


def _ret_branch(h, w_in, state, pos0, sb, mc, cast_f32=None):
    b, t, d = h.shape
    dk, dv, hpb = LANES, 2 * LANES, 2
    nhp = N_HEADS // hpb
    tt = min(RET_TILE, t)
    chunk = min(REF_CHUNK, t)
    has_state = state is not None
    rope_q, rope_k, mask, q_dec, k_dec, b_dec = _retention_tables(t, tt, pos0, chunk)
    q_off = 0
    k_off = q_off + nhp
    v_off = 2 * N_HEADS * dk // (hpb * dv)
    z_off = v_off + nhp
    in_specs = [pl.BlockSpec((sb, t, d), lambda i, g: (i, 0, 0)),
                pl.BlockSpec((d, hpb * dk), lambda i, g: (0, q_off + g)),
                pl.BlockSpec((d, hpb * dk), lambda i, g: (0, k_off + g)),
                pl.BlockSpec((d, hpb * dv), lambda i, g: (0, v_off + g)),
                pl.BlockSpec((d, hpb * dv), lambda i, g: (0, z_off + g)),
                pl.BlockSpec((2, t, dk), lambda i, g: (0, 0, 0), pipeline_mode=pl.Buffered(1)),
                pl.BlockSpec((2, t, dk), lambda i, g: (0, 0, 0), pipeline_mode=pl.Buffered(1)),
                pl.BlockSpec((hpb, tt, tt), lambda i, g: (g, 0, 0)),
                pl.BlockSpec((hpb, tt, dk), lambda i, g: (g, 0, 0)),
                pl.BlockSpec((hpb, tt, dk), lambda i, g: (g, 0, 0)),
                pl.BlockSpec((hpb, 1, dv), lambda i, g: (g, 0, 0))]
    args = [h, w_in, w_in, w_in, w_in, rope_q, rope_k, mask, q_dec, k_dec, b_dec]
    if has_state:
        in_specs.append(pl.BlockSpec((sb, hpb, dk, dv), lambda i, g: (i, g, 0, 0)))
        args.append(state)
    out_specs = [pl.BlockSpec((sb, t, hpb * dv), lambda i, g: (i, 0, g)),
                 pl.BlockSpec((sb, hpb, dk, dv), lambda i, g: (i, g, 0, 0))]
    out_shape = [jax.ShapeDtypeStruct((b, t, N_HEADS * dv), BF16),
                 jax.ShapeDtypeStruct((b, N_HEADS, dk, dv), F32)]
    cast_blocks = 0
    if cast_f32 is not None:
        c_in, c_out, c_shape, cast_blocks = _cast_specs(cast_f32, 0, cast_f32.shape[2], (b // sb) * nhp,
                                                        lambda i, g: i * nhp + g)
        in_specs.append(c_in)
        args.append(cast_f32)
        out_specs.append(c_out)
        out_shape.append(c_shape)
    pbuf_rows = sb * t if mc > t else SUBLANES
    return pl.pallas_call(
        functools.partial(_ret_body, sb=sb, ts=t, mc=mc, tt=tt, has_state=has_state,
                          cast_blocks=cast_blocks),
        grid=(b // sb, nhp),
        in_specs=in_specs,
        out_specs=out_specs,
        out_shape=out_shape,
        scratch_shapes=[pltpu.VMEM((4, pbuf_rows, hpb * dv), F32)],
        compiler_params=_params(("arbitrary", "arbitrary")),
        name="ret_branch",
    )(*args)


def _merge_body(*refs, mc, cast_blocks):
    if cast_blocks:
        h_ref, ya_ref, yb_ref, wma_ref, wmb_ref, wa_ref, wb_ref, wf_ref, o_ref, wr_ref = refs
        _side_cast(wf_ref, wr_ref, cast_blocks)
    else:
        h_ref, ya_ref, yb_ref, wma_ref, wmb_ref, wa_ref, wb_ref, o_ref = refs

    def chunk(i, carry):
        r0 = pl.multiple_of(i * mc, mc)
        rows = pl.ds(r0, mc)
        hr = h_ref[rows, :]
        ma = _sigmoid(_dot(hr, wma_ref[...])) * _dot(ya_ref[rows, :], wa_ref[0])
        mb = _sigmoid(_dot(hr, wmb_ref[...])) * _dot(yb_ref[rows, :], wb_ref[0])
        o_ref[rows, :] = (ma + mb).astype(BF16)
        return carry

    lax.fori_loop(0, h_ref.shape[0] // mc, chunk, 0, unroll=CHUNK_UNROLL)


def _merge(h2, ya2, yb2, w_in, w_branch, tm, mc, cast_f32=None):
    n, d = h2.shape
    nb = 512
    nblk = d // nb
    ma_off = (w_in.shape[1] - 2 * d) // nb
    mb_off = ma_off + nblk
    act = pl.BlockSpec((tm, d), lambda i, j: (i, 0))
    in_specs = [act, act, act,
                pl.BlockSpec((d, nb), lambda i, j: (0, ma_off + j)),
                pl.BlockSpec((d, nb), lambda i, j: (0, mb_off + j)),
                pl.BlockSpec((1, d, nb), lambda i, j: (0, 0, j)),
                pl.BlockSpec((1, d, nb), lambda i, j: (1, 0, j))]
    args = [h2, ya2, yb2, w_in, w_in, w_branch, w_branch]
    out_specs = [pl.BlockSpec((tm, nb), lambda i, j: (i, j))]
    out_shape = [jax.ShapeDtypeStruct((n, d), BF16)]
    cast_blocks = 0
    if cast_f32 is not None:
        c_in, c_out, c_shape, cast_blocks = _cast_specs(cast_f32, 0, cast_f32.shape[2], (n // tm) * nblk,
                                                        lambda i, j: i * nblk + j)
        in_specs.append(c_in)
        args.append(cast_f32)
        out_specs.append(c_out)
        out_shape.append(c_shape)
    return pl.pallas_call(
        functools.partial(_merge_body, mc=mc, cast_blocks=cast_blocks),
        grid=(n // tm, nblk),
        in_specs=in_specs,
        out_specs=out_specs,
        out_shape=out_shape,
        compiler_params=_params(("arbitrary", "arbitrary")),
        name="merge",
    )(*args)


def _out_body(m_ref, w_ref, x_ref, mod_ref, g_ref, y_ref, obuf, *, sb, ts, mc):
    g = g_ref[...]

    def finish(o, x, gate):
        ms = jnp.mean(o * o, axis=-1, keepdims=True)
        return x + gate * (o * lax.rsqrt(ms + EPS) * g)

    if mc <= ts:
        for s in range(sb):
            gate = mod_ref[s, 2:3, :]

            def chunk(i, carry, s=s, gate=gate):
                rows = pl.ds(pl.multiple_of(i * mc, mc), mc)
                o = _dot(m_ref[s, rows, :], w_ref[...])
                y_ref[s, rows, :] = finish(o, x_ref[s, rows, :], gate)
                return carry

            lax.fori_loop(0, ts // mc, chunk, 0, unroll=CHUNK_UNROLL)
    else:
        obuf[...] = _dot(m_ref[...].reshape(sb * ts, m_ref.shape[2]), w_ref[...])
        for s in range(sb):
            y_ref[s] = finish(obuf[s * ts:(s + 1) * ts, :], x_ref[s], mod_ref[s, 2:3, :])


def _out_proj(merged, w_out, x, mod3, g_post, sb, tr, mc):
    b, t, d = x.shape
    obuf_rows = sb * tr if mc > tr else SUBLANES
    act = lambda i, j: (i, j, 0)
    return pl.pallas_call(
        functools.partial(_out_body, sb=sb, ts=tr, mc=mc),
        grid=(b // sb, t // tr),
        in_specs=[pl.BlockSpec((sb, tr, d), act),
                  pl.BlockSpec((d, d), lambda i, j: (0, 0), pipeline_mode=pl.Buffered(1)),
                  pl.BlockSpec((sb, tr, d), act),
                  pl.BlockSpec((sb, 3, d), lambda i, j: (i, 0, 0)),
                  pl.BlockSpec((1, d), lambda i, j: (0, 0))],
        out_specs=pl.BlockSpec((sb, tr, d), act),
        out_shape=jax.ShapeDtypeStruct((b, t, d), F32),
        scratch_shapes=[pltpu.VMEM((obuf_rows, d), F32)],
        compiler_params=_params(("arbitrary", "arbitrary")),
        name="out_proj",
    )(merged, w_out, x, mod3, g_post.reshape(1, d))


class _LayerWeights:
    def __init__(self, w_in, w_branch, w_out, conv_cols):
        self.w_in, self.w_branch, self.w_out = w_in, w_branch, w_out
        self.conv = _cast_cols(w_in, conv_cols)
        self.rest = self.branch = self.out = None


def _mixer_layer(x, mod3, hist, state, pos0, g_pre, g_post, conv_w, conv_b, w):
    b, t, d = x.shape
    long_seq = t >= ROW_CHUNK
    sb = 1 if long_seq else b
    tr = 512 if long_seq else t
    mc = ROW_CHUNK if long_seq else b * t
    conv_cols = w.conv.shape[1]
    if long_seq:
        outs = _conv_rows(x, mod3, g_pre, w.conv, conv_w, conv_b, hist, tr, ROW_CHUNK,
                          cast_f32=w.w_in[None] if w.rest is None else None, cast_col0=conv_cols)
        h, ya, new_hist = outs[:3]
        if w.rest is None:
            w.rest = outs[3][0]
    else:
        if w.rest is None:
            w.rest = w.w_in[:, conv_cols:].astype(BF16)
        h = _prenorm(x, mod3, g_pre, sb, tr)
        ya, new_hist = _conv_seqs(h, w.conv, conv_w, conv_b, hist, sb)
    outs = _ret_branch(h, w.rest, state, pos0, sb, mc,
                       cast_f32=w.w_branch if long_seq and w.branch is None else None)
    yb, new_state = outs[:2]
    if w.branch is None:
        w.branch = outs[2] if long_seq else w.w_branch.astype(BF16)
    n = b * t
    tm = 1024 if long_seq else n
    outs = _merge(h.reshape(n, d), ya.reshape(n, d), yb.reshape(n, d), w.rest, w.branch, tm, ROW_CHUNK,
                  cast_f32=w.w_out[None] if long_seq and w.out is None else None)
    if w.out is None:
        w.out = outs[1][0] if long_seq else w.w_out.astype(BF16)
    y = _out_proj(outs[0].reshape(b, t, d), w.out, x, mod3, g_post, sb, 2 * tr if long_seq else tr, mc)
    return y, new_hist, new_state


def kernel(x_prompt, x_sample, c_prompt, c_sample, state_conv, state_ret, ada_w, ada_b, norm_pre,
           norm_post, w_in, conv_w, conv_b, w_branch, w_out):
    depth = w_in.shape[0]
    bp = x_prompt.shape[0]
    d = x_prompt.shape[2]
    hp, hs = x_prompt, x_sample
    conv_p, ret_p, conv_s, ret_s = [], [], [], []
    c_all = jnp.concatenate([c_prompt, c_sample], axis=0)
    for l in range(depth):
        mod3 = _modulation(c_all, ada_w[l], ada_b[l]).reshape(c_all.shape[0], 3, d)
        shared = (norm_pre[l], norm_post[l], conv_w[l], conv_b[l],
                  _LayerWeights(w_in[l], w_branch[l], w_out[l], conv_cols=4 * d))
        hp, cp, rp = _mixer_layer(hp, mod3[:bp], None, None, 0, *shared)
        hs, cs, rs = _mixer_layer(hs, mod3[bp:], state_conv[l], state_ret[l], PAST_LEN, *shared)
        conv_p.append(cp)
        ret_p.append(rp)
        conv_s.append(cs)
        ret_s.append(rs)
    return (hp, hs, jnp.stack(conv_p), jnp.stack(ret_p), jnp.stack(conv_s), jnp.stack(ret_s))
```

```python
import functools

import jax
import jax.numpy as jnp
import numpy as np
from jax import lax
from jax.experimental import pallas as pl
from jax.experimental.pallas import tpu as pltpu

F32 = jnp.float32
BF16 = jnp.bfloat16

EPS = 1e-6
ROPE_BASE = 10000.0
REF_CHUNK = 64
N_HEADS = 8
CONV_WIDTH = 3

LANES = 128
SUBLANES = 8
COL_BLOCK = 256
ROW_CHUNK = 256
RET_TILE = 256
CHUNK_UNROLL = True
PAST_LEN = 1024
VMEM_LIMIT = 56 * 1024 * 1024
RESIDENT_VMEM_LIMIT = 60 * 1024 * 1024


def _dot(a, b):
    return jnp.dot(a, b, preferred_element_type=F32)


def _sigmoid(x):
    return 1.0 / (1.0 + jnp.exp(-x))


def _silu(x):
    return x * _sigmoid(x)


def _params(semantics):
    return pltpu.CompilerParams(dimension_semantics=semantics, vmem_limit_bytes=VMEM_LIMIT)


def _cast_specs(w_f32, col0, cols, n_steps, step_of):
    m, d, _ = w_f32.shape
    per = cols // COL_BLOCK
    assert cols % COL_BLOCK == 0 and col0 % COL_BLOCK == 0 and m * per <= n_steps

    def block(*ids):
        return jnp.minimum(step_of(*ids), m * per - 1)

    in_spec = pl.BlockSpec((1, d, COL_BLOCK),
                           lambda *ids: (block(*ids) // per, 0, col0 // COL_BLOCK + block(*ids) % per))
    out_spec = pl.BlockSpec((1, d, COL_BLOCK), lambda *ids: (block(*ids) // per, 0, block(*ids) % per))
    return in_spec, out_spec, jax.ShapeDtypeStruct((m, d, cols), BF16), m * per


def _cast_body(wf_ref, wr_ref):
    wr_ref[...] = wf_ref[...].astype(BF16)


def _cast_cols(w_f32, cols):
    d = w_f32.shape[0]
    nb = 2 * COL_BLOCK
    return pl.pallas_call(
        _cast_body,
        grid=(cols // nb,),
        in_specs=[pl.BlockSpec((d, nb), lambda j: (0, j))],
        out_specs=pl.BlockSpec((d, nb), lambda j: (0, j)),
        out_shape=jax.ShapeDtypeStruct((d, cols), BF16),
        compiler_params=_params(("arbitrary",)),
        name="cast_cols",
    )(w_f32)


def _side_cast(wf_ref, wr_ref, n_blocks):
    step = pl.program_id(0) * pl.num_programs(1) + pl.program_id(1)

    @pl.when(step < n_blocks)
    def _():
        wr_ref[...] = wf_ref[...].astype(BF16)


def _mod_body(c_ref, w_ref, b_ref, o_ref):
    a = _silu(c_ref[...]).astype(BF16)
    o_ref[...] = _dot(a, w_ref[...].astype(BF16)) + b_ref[...]


def _modulation(c_all, ada_w, ada_b):
    nseq, d = c_all.shape
    n_out = ada_w.shape[1]
    nb = 768
    return pl.pallas_call(
        _mod_body,
        grid=(n_out // nb,),
        in_specs=[pl.BlockSpec((nseq, d), lambda j: (0, 0)),
                  pl.BlockSpec((d, nb), lambda j: (0, j)),
                  pl.BlockSpec((1, nb), lambda j: (0, j))],
        out_specs=pl.BlockSpec((nseq, nb), lambda j: (0, j)),
        out_shape=jax.ShapeDtypeStruct((nseq, n_out), F32),
        compiler_params=_params(("arbitrary",)),
        name="modulation",
    )(c_all, ada_w, ada_b.reshape(1, n_out))


def _prenorm_body(x_ref, mod_ref, g_ref, h_ref):
    x = x_ref[...]
    ms = jnp.mean(x * x, axis=-1, keepdims=True)
    y = x * lax.rsqrt(ms + EPS) * g_ref[...]
    shift = mod_ref[:, 0:1, :]
    scale = mod_ref[:, 1:2, :]
    h_ref[...] = (y * (1.0 + scale) + shift).astype(BF16)


def _prenorm(x, mod3, g_pre, sb, tr):
    b, t, d = x.shape
    return pl.pallas_call(
        _prenorm_body,
        grid=(b // sb, t // tr),
        in_specs=[pl.BlockSpec((sb, tr, d), lambda i, j: (i, j, 0)),
                  pl.BlockSpec((sb, 3, d), lambda i, j: (i, 0, 0)),
                  pl.BlockSpec((1, d), lambda i, j: (0, 0))],
        out_specs=pl.BlockSpec((sb, tr, d), lambda i, j: (i, j, 0)),
        out_shape=jax.ShapeDtypeStruct((b, t, d), BF16),
        compiler_params=_params(("arbitrary", "arbitrary")),
        name="prenorm",
    )(x, mod3, g_pre.reshape(1, d))


def _conv_epilogue(ubuf, ec, taps, gb, gc, gu, zc):
    w0, w1, w2, cb = taps
    top = SUBLANES
    u = gc * gu
    ubuf[top:top + ec, :] = u
    f0 = ubuf[top - 2:top - 2 + ec, :]
    f1 = ubuf[top - 1:top - 1 + ec, :]
    conv = cb + w0 * f0 + w1 * f1 + w2 * u
    ubuf[top - 2:top, :] = ubuf[top + ec - 2:top + ec, :]
    return (gb * conv * _silu(zc)).astype(BF16)


def _conv_rows_body(*refs, mc, has_hist, cast_blocks):
    it = iter(refs)
    x_ref, mod_ref, g_ref, wb_ref, wc_ref, wu_ref, wz_ref, cw_ref, cb_ref = (next(it) for _ in range(9))
    hist_ref = next(it) if has_hist else None
    wf_ref = next(it) if cast_blocks else None
    h_ref, ya_ref, nh_ref = next(it), next(it), next(it)
    wr_ref = next(it) if cast_blocks else None
    ubuf = next(it)
    if cast_blocks:
        _side_cast(wf_ref, wr_ref, cast_blocks)
    tr, d = x_ref.shape[1], x_ref.shape[2]
    cw = ubuf.shape[2]
    ng = d // cw
    top = SUBLANES
    r = pl.program_id(1)

    @pl.when(r == 0)
    def _():
        for gi in range(ng):
            cols = slice(gi * cw, (gi + 1) * cw)
            ubuf[gi, top - 2:top, :] = hist_ref[0, :, cols] if has_hist else jnp.zeros((2, cw), F32)

    g_pre = g_ref[...]
    shift = mod_ref[0, 0:1, :]
    scale1 = 1.0 + mod_ref[0, 1:2, :]
    for sub in range(tr // mc):
        rows = slice(sub * mc, (sub + 1) * mc)
        x = x_ref[0, rows, :]
        ms = jnp.mean(x * x, axis=-1, keepdims=True)
        h_ref[0, rows, :] = (x * lax.rsqrt(ms + EPS) * g_pre * scale1 + shift).astype(BF16)
        hr = h_ref[0, rows, :]
        for gi in range(ng):
            cols = slice(gi * cw, (gi + 1) * cw)
            taps = (cw_ref[0:1, cols], cw_ref[1:2, cols], cw_ref[2:3, cols], cb_ref[:, cols])
            ya_ref[0, rows, cols] = _conv_epilogue(
                ubuf.at[gi], mc, taps, gc=_dot(hr, wc_ref[:, cols]), gu=_dot(hr, wu_ref[:, cols]),
                zc=_dot(hr, wz_ref[:, cols]), gb=_dot(hr, wb_ref[:, cols]))

    @pl.when(r == pl.num_programs(1) - 1)
    def _():
        for gi in range(ng):
            nh_ref[0, :, gi * cw:(gi + 1) * cw] = ubuf[gi, top - 2:top, :]


def _conv_rows(x, mod3, g_pre, w_conv, conv_w, conv_b, hist, tr, mc, cast_f32=None, cast_col0=0):
    b, t, d = x.shape
    cw = COL_BLOCK
    nr = t // tr
    has_hist = hist is not None
    row_block = lambda i, r: (i, r, 0)
    per_seq = lambda i, r: (i, 0, 0)
    whole = lambda i, r: (0, 0)
    in_specs = [pl.BlockSpec((1, tr, d), row_block),
                pl.BlockSpec((1, 3, d), per_seq),
                pl.BlockSpec((1, d), whole)]
    in_specs += [pl.BlockSpec((d, d), functools.partial(lambda i, r, k: (0, k), k=k),
                              pipeline_mode=pl.Buffered(1)) for k in range(4)]
    in_specs += [pl.BlockSpec((CONV_WIDTH, d), whole), pl.BlockSpec((1, d), whole)]
    args = [x, mod3, g_pre.reshape(1, d), w_conv, w_conv, w_conv, w_conv, conv_w, conv_b.reshape(1, d)]
    if has_hist:
        in_specs.append(pl.BlockSpec((1, CONV_WIDTH - 1, d), per_seq))
        args.append(hist)
    out_specs = [pl.BlockSpec((1, tr, d), row_block),
                 pl.BlockSpec((1, tr, d), row_block),
                 pl.BlockSpec((1, CONV_WIDTH - 1, d), per_seq)]
    out_shape = [jax.ShapeDtypeStruct((b, t, d), BF16),
                 jax.ShapeDtypeStruct((b, t, d), BF16),
                 jax.ShapeDtypeStruct((b, CONV_WIDTH - 1, d), F32)]
    cast_blocks = 0
    if cast_f32 is not None:
        c_in, c_out, c_shape, cast_blocks = _cast_specs(
            cast_f32, cast_col0, cast_f32.shape[2] - cast_col0, b * nr, lambda i, r: i * nr + r)
        in_specs.append(c_in)
        args.append(cast_f32)
        out_specs.append(c_out)
        out_shape.append(c_shape)
    return pl.pallas_call(
        functools.partial(_conv_rows_body, mc=mc, has_hist=has_hist, cast_blocks=cast_blocks),
        grid=(b, nr),
        in_specs=in_specs,
        out_specs=out_specs,
        out_shape=out_shape,
        scratch_shapes=[pltpu.VMEM((d // cw, mc + SUBLANES, cw), F32)],
        compiler_params=pltpu.CompilerParams(dimension_semantics=("arbitrary", "arbitrary"),
                                             vmem_limit_bytes=RESIDENT_VMEM_LIMIT),
        name="conv_rows",
    )(*args)


def _conv_seqs_body(*refs, sb, ts, has_hist):
    if has_hist:
        (h_ref, wb_ref, wc_ref, wu_ref, wz_ref, cw_ref, cb_ref, hist_ref,
         ya_ref, nh_ref, ubuf, pbuf) = refs
    else:
        (h_ref, wb_ref, wc_ref, wu_ref, wz_ref, cw_ref, cb_ref,
         ya_ref, nh_ref, ubuf, pbuf) = refs
    taps = (cw_ref[0:1, :], cw_ref[1:2, :], cw_ref[2:3, :], cb_ref[...])
    top = SUBLANES
    hr = h_ref[...].reshape(sb * ts, h_ref.shape[2])
    for k, w_ref in enumerate((wb_ref, wc_ref, wu_ref, wz_ref)):
        pbuf[k] = _dot(hr, w_ref[...])
    for s in range(sb):
        ubuf[top - 2:top, :] = hist_ref[s] if has_hist else jnp.zeros((2, ubuf.shape[1]), F32)
        rows = slice(s * ts, (s + 1) * ts)
        ya_ref[s] = _conv_epilogue(ubuf, ts, taps, pbuf[0, rows, :], pbuf[1, rows, :],
                                   pbuf[2, rows, :], pbuf[3, rows, :])
        nh_ref[s] = ubuf[top - 2:top, :]


def _conv_seqs(h, w_in, conv_w, conv_b, hist, sb):
    b, t, d = h.shape
    cw = COL_BLOCK
    ng = d // cw
    has_hist = hist is not None
    in_specs = [pl.BlockSpec((sb, t, d), lambda i, g: (i, 0, 0))]
    in_specs += [pl.BlockSpec((d, cw), functools.partial(lambda i, g, k: (0, k * ng + g), k=k))
                 for k in range(4)]
    in_specs += [pl.BlockSpec((CONV_WIDTH, cw), lambda i, g: (0, g)),
                 pl.BlockSpec((1, cw), lambda i, g: (0, g))]
    args = [h, w_in, w_in, w_in, w_in, conv_w, conv_b.reshape(1, d)]
    if has_hist:
        in_specs.append(pl.BlockSpec((sb, CONV_WIDTH - 1, cw), lambda i, g: (i, 0, g)))
        args.append(hist)
    return pl.pallas_call(
        functools.partial(_conv_seqs_body, sb=sb, ts=t, has_hist=has_hist),
        grid=(b // sb, ng),
        in_specs=in_specs,
        out_specs=[pl.BlockSpec((sb, t, cw), lambda i, g: (i, 0, g)),
                   pl.BlockSpec((sb, CONV_WIDTH - 1, cw), lambda i, g: (i, 0, g))],
        out_shape=[jax.ShapeDtypeStruct((b, t, d), BF16),
                   jax.ShapeDtypeStruct((b, CONV_WIDTH - 1, d), F32)],
        scratch_shapes=[pltpu.VMEM((t + SUBLANES, cw), F32),
                        pltpu.VMEM((4, sb * t, cw), F32)],
        compiler_params=_params(("arbitrary", "arbitrary")),
        name="conv_seqs",
    )(*args)


def _ret_body(*refs, sb, ts, mc, tt, has_state, cast_blocks):
    it = iter(refs)
    (h_ref, wq_ref, wk_ref, wv_ref, wz_ref, rq_ref, rk_ref, mask_ref, qd_ref, kd_ref,
     bd_ref) = (next(it) for _ in range(11))
    st_ref = next(it) if has_state else None
    wf_ref = next(it) if cast_blocks else None
    yb_ref, ns_ref = next(it), next(it)
    wr_ref = next(it) if cast_blocks else None
    pbuf = next(it)
    if cast_blocks:
        _side_cast(wf_ref, wr_ref, cast_blocks)
    dk = LANES
    dv = 2 * LANES
    hpb = 2

    def tile(s, r0, q2, k2, v2, z2, cq, sq, ck, sk):
        for hh in range(hpb):
            q = q2[:, hh * dk:(hh + 1) * dk]
            k = k2[:, hh * dk:(hh + 1) * dk]
            v = v2[:, hh * dv:(hh + 1) * dv].astype(BF16)
            z = z2[:, hh * dv:(hh + 1) * dv]
            qr = q * cq + pltpu.roll(q, dk // 2, axis=1) * sq
            kr = k * ck + pltpu.roll(k, dk // 2, axis=1) * sk
            state = ns_ref[s, hh]
            scores = lax.dot_general(qr.astype(BF16), kr.astype(BF16),
                                     (((1,), (1,)), ((), ())), preferred_element_type=F32)
            p = (scores * mask_ref[hh]).astype(BF16)
            o = _dot(p, v) + _dot((qr * qd_ref[hh]).astype(BF16), state.astype(BF16))
            kv = lax.dot_general((kr * kd_ref[hh]).astype(BF16), v,
                                 (((0,), (0,)), ((), ())), preferred_element_type=F32)
            ns_ref[s, hh] = bd_ref[hh] * state + kv
            on = o * lax.rsqrt(jnp.mean(o * o, axis=-1, keepdims=True) + EPS)
            yb_ref[s, pl.ds(r0, tt), hh * dv:(hh + 1) * dv] = (on * _silu(z)).astype(BF16)

    def projections(hr):
        return (_dot(hr, wq_ref[...]), _dot(hr, wk_ref[...]),
                _dot(hr, wv_ref[...]), _dot(hr, wz_ref[...]))

    if has_state:
        ns_ref[...] = st_ref[...]
    else:
        ns_ref[...] = jnp.zeros(ns_ref.shape, F32)

    if mc <= ts:
        assert mc == tt
        for s in range(sb):
            def chunk(i, carry, s=s):
                r0 = pl.multiple_of(i * mc, mc)
                q2, k2, v2, z2 = projections(h_ref[s, pl.ds(r0, mc), :])
                rows = pl.ds(r0, mc)
                tile(s, r0, q2, k2, v2, z2, rq_ref[0, rows, :], rq_ref[1, rows, :],
                     rk_ref[0, rows, :], rk_ref[1, rows, :])
                return carry

            lax.fori_loop(0, ts // mc, chunk, 0, unroll=CHUNK_UNROLL)
    else:
        assert tt == ts
        hr = h_ref[...].reshape(sb * ts, h_ref.shape[2])
        for k, p in enumerate(projections(hr)):
            pbuf[k, :, 0:p.shape[1]] = p
        for s in range(sb):
            rows = slice(s * ts, (s + 1) * ts)
            tile(s, 0, pbuf[0, rows, 0:hpb * dk], pbuf[1, rows, 0:hpb * dk],
                 pbuf[2, rows, :], pbuf[3, rows, :],
                 rq_ref[0], rq_ref[1], rk_ref[0], rk_ref[1])


def _retention_tables(t, tt, pos0, chunk):
    dk = LANES
    f32 = np.float32
    inv = f32(ROPE_BASE) ** (-np.arange(0, dk, 2, dtype=f32) / f32(dk))
    ang = (pos0 + np.arange(t)).astype(f32)[:, None] * inv[None, :]
    cos, sin = np.cos(ang), np.sin(ang)
    rope_q = np.stack([np.concatenate([cos, cos], -1), np.concatenate([-sin, sin], -1)])
    rope_k = rope_q * f32(dk ** -0.5)
    lg = np.log(f32(1.0) - f32(2.0) ** (f32(-5.0) - np.arange(N_HEADS, dtype=f32)))
    idx = np.arange(tt, dtype=f32)
    visible = (np.arange(tt)[None, :] // chunk) <= (np.arange(tt)[:, None] // chunk)
    mask = np.exp(lg[:, None, None] * np.abs(idx[:, None] - idx[None, :])) * visible.astype(f32)
    q_dec = np.broadcast_to(np.exp(lg[:, None] * (idx + f32(1.0)))[..., None], (N_HEADS, tt, dk))
    k_dec = np.broadcast_to(np.exp(lg[:, None] * (f32(tt - 1.0) - idx))[..., None], (N_HEADS, tt, dk))
    b_dec = np.broadcast_to(np.exp(lg * f32(tt))[:, None, None], (N_HEADS, 1, 2 * dk))
    return tuple(jnp.asarray(a, dtype=F32) for a in (rope_q, rope_k, mask, q_dec, k_dec, b_dec))


def _ret_branch(h, w_in, state, pos0, sb, mc, cast_f32=None):
    b, t, d = h.shape
    dk, dv, hpb = LANES, 2 * LANES, 2
    nhp = N_HEADS // hpb
    tt = min(RET_TILE, t)
    chunk = min(REF_CHUNK, t)
    has_state = state is not None
    rope_q, rope_k, mask, q_dec, k_dec, b_dec = _retention_tables(t, tt, pos0, chunk)
    q_off = 0
    k_off = q_off + nhp
    v_off = 2 * N_HEADS * dk // (hpb * dv)
    z_off = v_off + nhp
    in_specs = [pl.BlockSpec((sb, t, d), lambda i, g: (i, 0, 0)),
                pl.BlockSpec((d, hpb * dk), lambda i, g: (0, q_off + g)),
                pl.BlockSpec((d, hpb * dk), lambda i, g: (0, k_off + g)),
                pl.BlockSpec((d, hpb * dv), lambda i, g: (0, v_off + g)),
                pl.BlockSpec((d, hpb * dv), lambda i, g: (0, z_off + g)),
                pl.BlockSpec((2, t, dk), lambda i, g: (0, 0, 0), pipeline_mode=pl.Buffered(1)),
                pl.BlockSpec((2, t, dk), lambda i, g: (0, 0, 0), pipeline_mode=pl.Buffered(1)),
                pl.BlockSpec((hpb, tt, tt), lambda i, g: (g, 0, 0)),
                pl.BlockSpec((hpb, tt, dk), lambda i, g: (g, 0, 0)),
                pl.BlockSpec((hpb, tt, dk), lambda i, g: (g, 0, 0)),
                pl.BlockSpec((hpb, 1, dv), lambda i, g: (g, 0, 0))]
    args = [h, w_in, w_in, w_in, w_in, rope_q, rope_k, mask, q_dec, k_dec, b_dec]
    if has_state:
        in_specs.append(pl.BlockSpec((sb, hpb, dk, dv), lambda i, g: (i, g, 0, 0)))
        args.append(state)
    out_specs = [pl.BlockSpec((sb, t, hpb * dv), lambda i, g: (i, 0, g)),
                 pl.BlockSpec((sb, hpb, dk, dv), lambda i, g: (i, g, 0, 0))]
    out_shape = [jax.ShapeDtypeStruct((b, t, N_HEADS * dv), BF16),
                 jax.ShapeDtypeStruct((b, N_HEADS, dk, dv), F32)]
    cast_blocks = 0
    if cast_f32 is not None:
        c_in, c_out, c_shape, cast_blocks = _cast_specs(cast_f32, 0, cast_f32.shape[2], (b // sb) * nhp,
                                                        lambda i, g: i * nhp + g)
        in_specs.append(c_in)
        args.append(cast_f32)
        out_specs.append(c_out)
        out_shape.append(c_shape)
    pbuf_rows = sb * t if mc > t else SUBLANES
    return pl.pallas_call(
        functools.partial(_ret_body, sb=sb, ts=t, mc=mc, tt=tt, has_state=has_state,
                          cast_blocks=cast_blocks),
        grid=(b // sb, nhp),
        in_specs=in_specs,
        out_specs=out_specs,
        out_shape=out_shape,
        scratch_shapes=[pltpu.VMEM((4, pbuf_rows, hpb * dv), F32)],
        compiler_params=_params(("arbitrary", "arbitrary")),
        name="ret_branch",
    )(*args)


def _merge_body(*refs, mc, cast_blocks):
    if cast_blocks:
        h_ref, ya_ref, yb_ref, wma_ref, wmb_ref, wa_ref, wb_ref, wf_ref, o_ref, wr_ref = refs
        _side_cast(wf_ref, wr_ref, cast_blocks)
    else:
        h_ref, ya_ref, yb_ref, wma_ref, wmb_ref, wa_ref, wb_ref, o_ref = refs

    def chunk(i, carry):
        r0 = pl.multiple_of(i * mc, mc)
        rows = pl.ds(r0, mc)
        hr = h_ref[rows, :]
        ma = _sigmoid(_dot(hr, wma_ref[...])) * _dot(ya_ref[rows, :], wa_ref[0])
        mb = _sigmoid(_dot(hr, wmb_ref[...])) * _dot(yb_ref[rows, :], wb_ref[0])
        o_ref[rows, :] = (ma + mb).astype(BF16)
        return carry

    lax.fori_loop(0, h_ref.shape[0] // mc, chunk, 0, unroll=CHUNK_UNROLL)


def _merge(h2, ya2, yb2, w_in, w_branch, tm, mc, cast_f32=None):
    n, d = h2.shape
    nb = 512
    nblk = d // nb
    ma_off = (w_in.shape[1] - 2 * d) // nb
    mb_off = ma_off + nblk
    act = pl.BlockSpec((tm, d), lambda i, j: (i, 0))
    in_specs = [act, act, act,
                pl.BlockSpec((d, nb), lambda i, j: (0, ma_off + j)),
                pl.BlockSpec((d, nb), lambda i, j: (0, mb_off + j)),
                pl.BlockSpec((1, d, nb), lambda i, j: (0, 0, j)),
                pl.BlockSpec((1, d, nb), lambda i, j: (1, 0, j))]
    args = [h2, ya2, yb2, w_in, w_in, w_branch, w_branch]
    out_specs = [pl.BlockSpec((tm, nb), lambda i, j: (i, j))]
    out_shape = [jax.ShapeDtypeStruct((n, d), BF16)]
    cast_blocks = 0
    if cast_f32 is not None:
        c_in, c_out, c_shape, cast_blocks = _cast_specs(cast_f32, 0, cast_f32.shape[2], (n // tm) * nblk,
                                                        lambda i, j: i * nblk + j)
        in_specs.append(c_in)
        args.append(cast_f32)
        out_specs.append(c_out)
        out_shape.append(c_shape)
    return pl.pallas_call(
        functools.partial(_merge_body, mc=mc, cast_blocks=cast_blocks),
        grid=(n // tm, nblk),
        in_specs=in_specs,
        out_specs=out_specs,
        out_shape=out_shape,
        compiler_params=_params(("arbitrary", "arbitrary")),
        name="merge",
    )(*args)


def _merge_rows_body(*refs, mc, cast_blocks):
    it = iter(refs)
    h_ref, ya_ref, yb_ref, wma_ref, wmb_ref, wa_ref, wb_ref = (next(it) for _ in range(7))
    wf_ref = next(it) if cast_blocks else None
    o_ref = next(it)
    wr_ref = next(it) if cast_blocks else None
    if cast_blocks:
        _side_cast(wf_ref, wr_ref, cast_blocks)
    tm, d = h_ref.shape
    cw = COL_BLOCK
    for sub in range(tm // mc):
        rows = slice(sub * mc, (sub + 1) * mc)
        hr, yar, ybr = h_ref[rows, :], ya_ref[rows, :], yb_ref[rows, :]
        for gi in range(d // cw):
            cols = slice(gi * cw, (gi + 1) * cw)
            ma = _sigmoid(_dot(hr, wma_ref[:, cols])) * _dot(yar, wa_ref[0, :, cols])
            mb = _sigmoid(_dot(hr, wmb_ref[:, cols])) * _dot(ybr, wb_ref[0, :, cols])
            o_ref[rows, cols] = (ma + mb).astype(BF16)


def _merge_rows(h2, ya2, yb2, w_in, w_branch, tm, mc, cast_f32=None):
    n, d = h2.shape
    ma_blk = w_in.shape[1] // d - 2
    act = pl.BlockSpec((tm, d), lambda i, j: (i, 0))
    resident = pl.Buffered(1)
    in_specs = [act, act, act,
                pl.BlockSpec((d, d), lambda i, j: (0, ma_blk), pipeline_mode=resident),
                pl.BlockSpec((d, d), lambda i, j: (0, ma_blk + 1), pipeline_mode=resident),
                pl.BlockSpec((1, d, d), lambda i, j: (0, 0, 0), pipeline_mode=resident),
                pl.BlockSpec((1, d, d), lambda i, j: (1, 0, 0), pipeline_mode=resident)]
    args = [h2, ya2, yb2, w_in, w_in, w_branch, w_branch]
    out_specs = [act]
    out_shape = [jax.ShapeDtypeStruct((n, d), BF16)]
    cast_blocks = 0
    if cast_f32 is not None:
        c_in, c_out, c_shape, cast_blocks = _cast_specs(cast_f32, 0, cast_f32.shape[2], n // tm,
                                                        lambda i, j: i)
        in_specs.append(c_in)
        args.append(cast_f32)
        out_specs.append(c_out)
        out_shape.append(c_shape)
    return pl.pallas_call(
        functools.partial(_merge_rows_body, mc=mc, cast_blocks=cast_blocks),
        grid=(n // tm, 1),
        in_specs=in_specs,
        out_specs=out_specs,
        out_shape=out_shape,
        compiler_params=pltpu.CompilerParams(dimension_semantics=("arbitrary", "arbitrary"),
                                             vmem_limit_bytes=RESIDENT_VMEM_LIMIT),
        name="merge_rows",
    )(*args)


def _out_body(m_ref, w_ref, x_ref, mod_ref, g_ref, y_ref, obuf, *, sb, ts, mc):
    g = g_ref[...]

    def finish(o, x, gate):
        ms = jnp.mean(o * o, axis=-1, keepdims=True)
        return x + gate * (o * lax.rsqrt(ms + EPS) * g)

    if mc <= ts:
        for s in range(sb):
            gate = mod_ref[s, 2:3, :]

            def chunk(i, carry, s=s, gate=gate):
                rows = pl.ds(pl.multiple_of(i * mc, mc), mc)
                o = _dot(m_ref[s, rows, :], w_ref[...])
                y_ref[s, rows, :] = finish(o, x_ref[s, rows, :], gate)
                return carry

            lax.fori_loop(0, ts // mc, chunk, 0, unroll=CHUNK_UNROLL)
    else:
        obuf[...] = _dot(m_ref[...].reshape(sb * ts, m_ref.shape[2]), w_ref[...])
        for s in range(sb):
            y_ref[s] = finish(obuf[s * ts:(s + 1) * ts, :], x_ref[s], mod_ref[s, 2:3, :])


def _out_proj(merged, w_out, x, mod3, g_post, sb, tr, mc):
    b, t, d = x.shape
    obuf_rows = sb * tr if mc > tr else SUBLANES
    act = lambda i, j: (i, j, 0)
    return pl.pallas_call(
        functools.partial(_out_body, sb=sb, ts=tr, mc=mc),
        grid=(b // sb, t // tr),
        in_specs=[pl.BlockSpec((sb, tr, d), act),
                  pl.BlockSpec((d, d), lambda i, j: (0, 0), pipeline_mode=pl.Buffered(1)),
                  pl.BlockSpec((sb, tr, d), act),
                  pl.BlockSpec((sb, 3, d), lambda i, j: (i, 0, 0)),
                  pl.BlockSpec((1, d), lambda i, j: (0, 0))],
        out_specs=pl.BlockSpec((sb, tr, d), act),
        out_shape=jax.ShapeDtypeStruct((b, t, d), F32),
        scratch_shapes=[pltpu.VMEM((obuf_rows, d), F32)],
        compiler_params=_params(("arbitrary", "arbitrary")),
        name="out_proj",
    )(merged, w_out, x, mod3, g_post.reshape(1, d))


class _LayerWeights:
    def __init__(self, w_in, w_branch, w_out, conv_cols):
        self.w_in, self.w_branch, self.w_out = w_in, w_branch, w_out
        self.conv = _cast_cols(w_in, conv_cols)
        self.rest = self.branch = self.out = None


def _mixer_layer(x, mod3, hist, state, pos0, g_pre, g_post, conv_w, conv_b, w):
    b, t, d = x.shape
    long_seq = t >= ROW_CHUNK
    sb = 1 if long_seq else b
    tr = 512 if long_seq else t
    mc = ROW_CHUNK if long_seq else b * t
    conv_cols = w.conv.shape[1]
    if long_seq:
        outs = _conv_rows(x, mod3, g_pre, w.conv, conv_w, conv_b, hist, tr, ROW_CHUNK,
                          cast_f32=w.w_in[None] if w.rest is None else None, cast_col0=conv_cols)
        h, ya, new_hist = outs[:3]
        if w.rest is None:
            w.rest = outs[3][0]
    else:
        if w.rest is None:
            w.rest = w.w_in[:, conv_cols:].astype(BF16)
        h = _prenorm(x, mod3, g_pre, sb, tr)
        ya, new_hist = _conv_seqs(h, w.conv, conv_w, conv_b, hist, sb)
    outs = _ret_branch(h, w.rest, state, pos0, sb, mc,
                       cast_f32=w.w_branch if long_seq and w.branch is None else None)
    yb, new_state = outs[:2]
    if w.branch is None:
        w.branch = outs[2] if long_seq else w.w_branch.astype(BF16)
    n = b * t
    acts = (h.reshape(n, d), ya.reshape(n, d), yb.reshape(n, d))
    if long_seq:
        outs = _merge_rows(*acts, w.rest, w.branch, tr, ROW_CHUNK,
                           cast_f32=w.w_out[None] if w.out is None else None)
    else:
        outs = _merge(*acts, w.rest, w.branch, n, ROW_CHUNK)
    if w.out is None:
        w.out = outs[1][0] if long_seq else w.w_out.astype(BF16)
    y = _out_proj(outs[0].reshape(b, t, d), w.out, x, mod3, g_post, sb, 2 * tr if long_seq else tr, mc)
    return y, new_hist, new_state


def kernel(x_prompt, x_sample, c_prompt, c_sample, state_conv, state_ret, ada_w, ada_b, norm_pre,
           norm_post, w_in, conv_w, conv_b, w_branch, w_out):
    depth = w_in.shape[0]
    bp = x_prompt.shape[0]
    d = x_prompt.shape[2]
    hp, hs = x_prompt, x_sample
    conv_p, ret_p, conv_s, ret_s = [], [], [], []
    c_all = jnp.concatenate([c_prompt, c_sample], axis=0)
    for l in range(depth):
        mod3 = _modulation(c_all, ada_w[l], ada_b[l]).reshape(c_all.shape[0], 3, d)
        shared = (norm_pre[l], norm_post[l], conv_w[l], conv_b[l],
                  _LayerWeights(w_in[l], w_branch[l], w_out[l], conv_cols=4 * d))
        hp, cp, rp = _mixer_layer(hp, mod3[:bp], None, None, 0, *shared)
        hs, cs, rs = _mixer_layer(hs, mod3[bp:], state_conv[l], state_ret[l], PAST_LEN, *shared)
        conv_p.append(cp)
        ret_p.append(rp)
        conv_s.append(cs)
        ret_s.append(rs)
    return (hp, hs, jnp.stack(conv_p), jnp.stack(ret_p), jnp.stack(conv_s), jnp.stack(ret_s))
```

```python
import functools

import jax
import jax.numpy as jnp
import numpy as np
from jax import lax
from jax.experimental import pallas as pl
from jax.experimental.pallas import tpu as pltpu

F32 = jnp.float32
BF16 = jnp.bfloat16

EPS = 1e-6
ROPE_BASE = 10000.0
REF_CHUNK = 64
N_HEADS = 8
CONV_WIDTH = 3

LANES = 128
SUBLANES = 8
COL_BLOCK = 256
ROW_CHUNK = 256
RET_TILE = 256
CHUNK_UNROLL = True
PAST_LEN = 1024
VMEM_LIMIT = 56 * 1024 * 1024
RESIDENT_VMEM_LIMIT = 60 * 1024 * 1024


def _dot(a, b):
    return jnp.dot(a, b, preferred_element_type=F32)


def _sigmoid(x):
    return 1.0 / (1.0 + jnp.exp(-x))


def _silu(x):
    return x * _sigmoid(x)


def _params(semantics):
    return pltpu.CompilerParams(dimension_semantics=semantics, vmem_limit_bytes=VMEM_LIMIT)


def _cast_specs(w_f32, col0, cols, n_steps, step_of):
    m, d, _ = w_f32.shape
    per = cols // COL_BLOCK
    assert cols % COL_BLOCK == 0 and col0 % COL_BLOCK == 0 and m * per <= n_steps

    def block(*ids):
        return jnp.minimum(step_of(*ids), m * per - 1)

    in_spec = pl.BlockSpec((1, d, COL_BLOCK),
                           lambda *ids: (block(*ids) // per, 0, col0 // COL_BLOCK + block(*ids) % per))
    out_spec = pl.BlockSpec((1, d, COL_BLOCK), lambda *ids: (block(*ids) // per, 0, block(*ids) % per))
    return in_spec, out_spec, jax.ShapeDtypeStruct((m, d, cols), BF16), m * per


def _cast_body(wf_ref, wr_ref):
    wr_ref[...] = wf_ref[...].astype(BF16)


def _cast_cols(w_f32, cols):
    d = w_f32.shape[0]
    nb = 2 * COL_BLOCK
    return pl.pallas_call(
        _cast_body,
        grid=(cols // nb,),
        in_specs=[pl.BlockSpec((d, nb), lambda j: (0, j))],
        out_specs=pl.BlockSpec((d, nb), lambda j: (0, j)),
        out_shape=jax.ShapeDtypeStruct((d, cols), BF16),
        compiler_params=_params(("arbitrary",)),
        name="cast_cols",
    )(w_f32)


def _side_cast(wf_ref, wr_ref, n_blocks):
    step = pl.program_id(0) * pl.num_programs(1) + pl.program_id(1)

    @pl.when(step < n_blocks)
    def _():
        wr_ref[...] = wf_ref[...].astype(BF16)


def _mod_body(c_ref, w_ref, b_ref, o_ref):
    a = _silu(c_ref[...]).astype(BF16)
    o_ref[...] = _dot(a, w_ref[...].astype(BF16)) + b_ref[...]


def _modulation(c_all, ada_w, ada_b):
    nseq, d = c_all.shape
    n_out = ada_w.shape[1]
    nb = 768
    return pl.pallas_call(
        _mod_body,
        grid=(n_out // nb,),
        in_specs=[pl.BlockSpec((nseq, d), lambda j: (0, 0)),
                  pl.BlockSpec((d, nb), lambda j: (0, j)),
                  pl.BlockSpec((1, nb), lambda j: (0, j))],
        out_specs=pl.BlockSpec((nseq, nb), lambda j: (0, j)),
        out_shape=jax.ShapeDtypeStruct((nseq, n_out), F32),
        compiler_params=_params(("arbitrary",)),
        name="modulation",
    )(c_all, ada_w, ada_b.reshape(1, n_out))


def _prenorm_body(x_ref, mod_ref, g_ref, h_ref):
    x = x_ref[...]
    ms = jnp.mean(x * x, axis=-1, keepdims=True)
    y = x * lax.rsqrt(ms + EPS) * g_ref[...]
    shift = mod_ref[:, 0:1, :]
    scale = mod_ref[:, 1:2, :]
    h_ref[...] = (y * (1.0 + scale) + shift).astype(BF16)


def _prenorm(x, mod3, g_pre, sb, tr):
    b, t, d = x.shape
    return pl.pallas_call(
        _prenorm_body,
        grid=(b // sb, t // tr),
        in_specs=[pl.BlockSpec((sb, tr, d), lambda i, j: (i, j, 0)),
                  pl.BlockSpec((sb, 3, d), lambda i, j: (i, 0, 0)),
                  pl.BlockSpec((1, d), lambda i, j: (0, 0))],
        out_specs=pl.BlockSpec((sb, tr, d), lambda i, j: (i, j, 0)),
        out_shape=jax.ShapeDtypeStruct((b, t, d), BF16),
        compiler_params=_params(("arbitrary", "arbitrary")),
        name="prenorm",
    )(x, mod3, g_pre.reshape(1, d))


def _conv_epilogue(ubuf, ec, taps, gb, gc, gu, zc):
    w0, w1, w2, cb = taps
    top = SUBLANES
    u = gc * gu
    ubuf[top:top + ec, :] = u
    f0 = ubuf[top - 2:top - 2 + ec, :]
    f1 = ubuf[top - 1:top - 1 + ec, :]
    conv = cb + w0 * f0 + w1 * f1 + w2 * u
    ubuf[top - 2:top, :] = ubuf[top + ec - 2:top + ec, :]
    return (gb * conv * _silu(zc)).astype(BF16)


def _conv_rows_body(*refs, mc, has_hist, cast_blocks):
    it = iter(refs)
    x_ref, mod_ref, g_ref, wb_ref, wc_ref, wu_ref, wz_ref, cw_ref, cb_ref = (next(it) for _ in range(9))
    hist_ref = next(it) if has_hist else None
    wf_ref = next(it) if cast_blocks else None
    h_ref, ya_ref, nh_ref = next(it), next(it), next(it)
    wr_ref = next(it) if cast_blocks else None
    ubuf = next(it)
    if cast_blocks:
        _side_cast(wf_ref, wr_ref, cast_blocks)
    tr, d = x_ref.shape[1], x_ref.shape[2]
    cw = ubuf.shape[2]
    ng = d // cw
    top = SUBLANES
    r = pl.program_id(1)

    @pl.when(r == 0)
    def _():
        for gi in range(ng):
            cols = slice(gi * cw, (gi + 1) * cw)
            ubuf[gi, top - 2:top, :] = hist_ref[0, :, cols] if has_hist else jnp.zeros((2, cw), F32)

    g_pre = g_ref[...]
    shift = mod_ref[0, 0:1, :]
    scale1 = 1.0 + mod_ref[0, 1:2, :]
    for sub in range(tr // mc):
        rows = slice(sub * mc, (sub + 1) * mc)
        x = x_ref[0, rows, :]
        ms = jnp.mean(x * x, axis=-1, keepdims=True)
        h_ref[0, rows, :] = (x * lax.rsqrt(ms + EPS) * g_pre * scale1 + shift).astype(BF16)
        hr = h_ref[0, rows, :]
        for gi in range(ng):
            cols = slice(gi * cw, (gi + 1) * cw)
            taps = (cw_ref[0:1, cols], cw_ref[1:2, cols], cw_ref[2:3, cols], cb_ref[:, cols])
            ya_ref[0, rows, cols] = _conv_epilogue(
                ubuf.at[gi], mc, taps, gc=_dot(hr, wc_ref[:, cols]), gu=_dot(hr, wu_ref[:, cols]),
                zc=_dot(hr, wz_ref[:, cols]), gb=_dot(hr, wb_ref[:, cols]))

    @pl.when(r == pl.num_programs(1) - 1)
    def _():
        for gi in range(ng):
            nh_ref[0, :, gi * cw:(gi + 1) * cw] = ubuf[gi, top - 2:top, :]


def _conv_rows(x, mod3, g_pre, w_conv, conv_w, conv_b, hist, tr, mc, cast_f32=None, cast_col0=0):
    b, t, d = x.shape
    cw = COL_BLOCK
    nr = t // tr
    has_hist = hist is not None
    row_block = lambda i, r: (i, r, 0)
    per_seq = lambda i, r: (i, 0, 0)
    whole = lambda i, r: (0, 0)
    in_specs = [pl.BlockSpec((1, tr, d), row_block),
                pl.BlockSpec((1, 3, d), per_seq),
                pl.BlockSpec((1, d), whole)]
    in_specs += [pl.BlockSpec((d, d), functools.partial(lambda i, r, k: (0, k), k=k),
                              pipeline_mode=pl.Buffered(1)) for k in range(4)]
    in_specs += [pl.BlockSpec((CONV_WIDTH, d), whole), pl.BlockSpec((1, d), whole)]
    args = [x, mod3, g_pre.reshape(1, d), w_conv, w_conv, w_conv, w_conv, conv_w, conv_b.reshape(1, d)]
    if has_hist:
        in_specs.append(pl.BlockSpec((1, CONV_WIDTH - 1, d), per_seq))
        args.append(hist)
    out_specs = [pl.BlockSpec((1, tr, d), row_block),
                 pl.BlockSpec((1, tr, d), row_block),
                 pl.BlockSpec((1, CONV_WIDTH - 1, d), per_seq)]
    out_shape = [jax.ShapeDtypeStruct((b, t, d), BF16),
                 jax.ShapeDtypeStruct((b, t, d), BF16),
                 jax.ShapeDtypeStruct((b, CONV_WIDTH - 1, d), F32)]
    cast_blocks = 0
    if cast_f32 is not None:
        c_in, c_out, c_shape, cast_blocks = _cast_specs(
            cast_f32, cast_col0, cast_f32.shape[2] - cast_col0, b * nr, lambda i, r: i * nr + r)
        in_specs.append(c_in)
        args.append(cast_f32)
        out_specs.append(c_out)
        out_shape.append(c_shape)
    return pl.pallas_call(
        functools.partial(_conv_rows_body, mc=mc, has_hist=has_hist, cast_blocks=cast_blocks),
        grid=(b, nr),
        in_specs=in_specs,
        out_specs=out_specs,
        out_shape=out_shape,
        scratch_shapes=[pltpu.VMEM((d // cw, mc + SUBLANES, cw), F32)],
        compiler_params=pltpu.CompilerParams(dimension_semantics=("arbitrary", "arbitrary"),
                                             vmem_limit_bytes=RESIDENT_VMEM_LIMIT),
        name="conv_rows",
    )(*args)


def _conv_seqs_body(*refs, sb, ts, has_hist):
    if has_hist:
        (h_ref, wb_ref, wc_ref, wu_ref, wz_ref, cw_ref, cb_ref, hist_ref,
         ya_ref, nh_ref, ubuf, pbuf) = refs
    else:
        (h_ref, wb_ref, wc_ref, wu_ref, wz_ref, cw_ref, cb_ref,
         ya_ref, nh_ref, ubuf, pbuf) = refs
    taps = (cw_ref[0:1, :], cw_ref[1:2, :], cw_ref[2:3, :], cb_ref[...])
    top = SUBLANES
    hr = h_ref[...].reshape(sb * ts, h_ref.shape[2])
    for k, w_ref in enumerate((wb_ref, wc_ref, wu_ref, wz_ref)):
        pbuf[k] = _dot(hr, w_ref[...])
    for s in range(sb):
        ubuf[top - 2:top, :] = hist_ref[s] if has_hist else jnp.zeros((2, ubuf.shape[1]), F32)
        rows = slice(s * ts, (s + 1) * ts)
        ya_ref[s] = _conv_epilogue(ubuf, ts, taps, pbuf[0, rows, :], pbuf[1, rows, :],
                                   pbuf[2, rows, :], pbuf[3, rows, :])
        nh_ref[s] = ubuf[top - 2:top, :]


def _conv_seqs(h, w_in, conv_w, conv_b, hist, sb):
    b, t, d = h.shape
    cw = COL_BLOCK
    ng = d // cw
    has_hist = hist is not None
    in_specs = [pl.BlockSpec((sb, t, d), lambda i, g: (i, 0, 0))]
    in_specs += [pl.BlockSpec((d, cw), functools.partial(lambda i, g, k: (0, k * ng + g), k=k))
                 for k in range(4)]
    in_specs += [pl.BlockSpec((CONV_WIDTH, cw), lambda i, g: (0, g)),
                 pl.BlockSpec((1, cw), lambda i, g: (0, g))]
    args = [h, w_in, w_in, w_in, w_in, conv_w, conv_b.reshape(1, d)]
    if has_hist:
        in_specs.append(pl.BlockSpec((sb, CONV_WIDTH - 1, cw), lambda i, g: (i, 0, g)))
        args.append(hist)
    return pl.pallas_call(
        functools.partial(_conv_seqs_body, sb=sb, ts=t, has_hist=has_hist),
        grid=(b // sb, ng),
        in_specs=in_specs,
        out_specs=[pl.BlockSpec((sb, t, cw), lambda i, g: (i, 0, g)),
                   pl.BlockSpec((sb, CONV_WIDTH - 1, cw), lambda i, g: (i, 0, g))],
        out_shape=[jax.ShapeDtypeStruct((b, t, d), BF16),
                   jax.ShapeDtypeStruct((b, CONV_WIDTH - 1, d), F32)],
        scratch_shapes=[pltpu.VMEM((t + SUBLANES, cw), F32),
                        pltpu.VMEM((4, sb * t, cw), F32)],
        compiler_params=_params(("arbitrary", "arbitrary")),
        name="conv_seqs",
    )(*args)


def _ret_body(*refs, sb, ts, mc, tt, has_state, cast_blocks):
    it = iter(refs)
    (h_ref, wq_ref, wk_ref, wv_ref, wz_ref, rq_ref, rk_ref, mask_ref, qd_ref, kd_ref,
     bd_ref) = (next(it) for _ in range(11))
    st_ref = next(it) if has_state else None
    wf_ref = next(it) if cast_blocks else None
    yb_ref, ns_ref = next(it), next(it)
    wr_ref = next(it) if cast_blocks else None
    pbuf = next(it)
    if cast_blocks:
        _side_cast(wf_ref, wr_ref, cast_blocks)
    dk = LANES
    dv = 2 * LANES
    hpb = 2

    def tile(s, r0, q2, k2, v2, z2, cq, sq, ck, sk):
        for hh in range(hpb):
            q = q2[:, hh * dk:(hh + 1) * dk]
            k = k2[:, hh * dk:(hh + 1) * dk]
            v = v2[:, hh * dv:(hh + 1) * dv].astype(BF16)
            z = z2[:, hh * dv:(hh + 1) * dv]
            qr = q * cq + pltpu.roll(q, dk // 2, axis=1) * sq
            kr = k * ck + pltpu.roll(k, dk // 2, axis=1) * sk
            state = ns_ref[s, hh]
            scores = lax.dot_general(qr.astype(BF16), kr.astype(BF16),
                                     (((1,), (1,)), ((), ())), preferred_element_type=F32)
            p = (scores * mask_ref[hh]).astype(BF16)
            o = _dot(p, v) + _dot((qr * qd_ref[hh]).astype(BF16), state.astype(BF16))
            kv = lax.dot_general((kr * kd_ref[hh]).astype(BF16), v,
                                 (((0,), (0,)), ((), ())), preferred_element_type=F32)
            ns_ref[s, hh] = bd_ref[hh] * state + kv
            on = o * lax.rsqrt(jnp.mean(o * o, axis=-1, keepdims=True) + EPS)
            yb_ref[s, pl.ds(r0, tt), hh * dv:(hh + 1) * dv] = (on * _silu(z)).astype(BF16)

    def projections(hr):
        return (_dot(hr, wq_ref[...]), _dot(hr, wk_ref[...]),
                _dot(hr, wv_ref[...]), _dot(hr, wz_ref[...]))

    if has_state:
        ns_ref[...] = st_ref[...]
    else:
        ns_ref[...] = jnp.zeros(ns_ref.shape, F32)

    if mc <= ts:
        assert mc == tt
        for s in range(sb):
            def chunk(i, carry, s=s):
                r0 = pl.multiple_of(i * mc, mc)
                q2, k2, v2, z2 = projections(h_ref[s, pl.ds(r0, mc), :])
                rows = pl.ds(r0, mc)
                tile(s, r0, q2, k2, v2, z2, rq_ref[0, rows, :], rq_ref[1, rows, :],
                     rk_ref[0, rows, :], rk_ref[1, rows, :])
                return carry

            lax.fori_loop(0, ts // mc, chunk, 0, unroll=CHUNK_UNROLL)
    else:
        assert tt == ts
        hr = h_ref[...].reshape(sb * ts, h_ref.shape[2])
        for k, p in enumerate(projections(hr)):
            pbuf[k, :, 0:p.shape[1]] = p
        for s in range(sb):
            rows = slice(s * ts, (s + 1) * ts)
            tile(s, 0, pbuf[0, rows, 0:hpb * dk], pbuf[1, rows, 0:hpb * dk],
                 pbuf[2, rows, :], pbuf[3, rows, :],
                 rq_ref[0], rq_ref[1], rk_ref[0], rk_ref[1])


def _retention_tables(t, tt, pos0, chunk):
    dk = LANES
    f32 = np.float32
    inv = f32(ROPE_BASE) ** (-np.arange(0, dk, 2, dtype=f32) / f32(dk))
    ang = (pos0 + np.arange(t)).astype(f32)[:, None] * inv[None, :]
    cos, sin = np.cos(ang), np.sin(ang)
    rope_q = np.stack([np.concatenate([cos, cos], -1), np.concatenate([-sin, sin], -1)])
    rope_k = rope_q * f32(dk ** -0.5)
    lg = np.log(f32(1.0) - f32(2.0) ** (f32(-5.0) - np.arange(N_HEADS, dtype=f32)))
    idx = np.arange(tt, dtype=f32)
    visible = (np.arange(tt)[None, :] // chunk) <= (np.arange(tt)[:, None] // chunk)
    mask = np.exp(lg[:, None, None] * np.abs(idx[:, None] - idx[None, :])) * visible.astype(f32)
    q_dec = np.broadcast_to(np.exp(lg[:, None] * (idx + f32(1.0)))[..., None], (N_HEADS, tt, dk))
    k_dec = np.broadcast_to(np.exp(lg[:, None] * (f32(tt - 1.0) - idx))[..., None], (N_HEADS, tt, dk))
    b_dec = np.broadcast_to(np.exp(lg * f32(tt))[:, None, None], (N_HEADS, 1, 2 * dk))
    return tuple(jnp.asarray(a, dtype=F32) for a in (rope_q, rope_k, mask, q_dec, k_dec, b_dec))


def _ret_branch(h, w_in, state, pos0, sb, mc, cast_f32=None):
    b, t, d = h.shape
    dk, dv, hpb = LANES, 2 * LANES, 2
    nhp = N_HEADS // hpb
    tt = min(RET_TILE, t)
    chunk = min(REF_CHUNK, t)
    has_state = state is not None
    rope_q, rope_k, mask, q_dec, k_dec, b_dec = _retention_tables(t, tt, pos0, chunk)
    q_off = 0
    k_off = q_off + nhp
    v_off = 2 * N_HEADS * dk // (hpb * dv)
    z_off = v_off + nhp
    in_specs = [pl.BlockSpec((sb, t, d), lambda i, g: (i, 0, 0)),
                pl.BlockSpec((d, hpb * dk), lambda i, g: (0, q_off + g)),
                pl.BlockSpec((d, hpb * dk), lambda i, g: (0, k_off + g)),
                pl.BlockSpec((d, hpb * dv), lambda i, g: (0, v_off + g)),
                pl.BlockSpec((d, hpb * dv), lambda i, g: (0, z_off + g)),
                pl.BlockSpec((2, t, dk), lambda i, g: (0, 0, 0)),
                pl.BlockSpec((2, t, dk), lambda i, g: (0, 0, 0)),
                pl.BlockSpec((hpb, tt, tt), lambda i, g: (g, 0, 0)),
                pl.BlockSpec((hpb, tt, dk), lambda i, g: (g, 0, 0)),
                pl.BlockSpec((hpb, tt, dk), lambda i, g: (g, 0, 0)),
                pl.BlockSpec((hpb, 1, dv), lambda i, g: (g, 0, 0))]
    args = [h, w_in, w_in, w_in, w_in, rope_q, rope_k, mask, q_dec, k_dec, b_dec]
    if has_state:
        in_specs.append(pl.BlockSpec((sb, hpb, dk, dv), lambda i, g: (i, g, 0, 0)))
        args.append(state)
    out_specs = [pl.BlockSpec((sb, t, hpb * dv), lambda i, g: (i, 0, g)),
                 pl.BlockSpec((sb, hpb, dk, dv), lambda i, g: (i, g, 0, 0))]
    out_shape = [jax.ShapeDtypeStruct((b, t, N_HEADS * dv), BF16),
                 jax.ShapeDtypeStruct((b, N_HEADS, dk, dv), F32)]
    cast_blocks = 0
    if cast_f32 is not None:
        c_in, c_out, c_shape, cast_blocks = _cast_specs(cast_f32, 0, cast_f32.shape[2], (b // sb) * nhp,
                                                        lambda i, g: i * nhp + g)
        in_specs.append(c_in)
        args.append(cast_f32)
        out_specs.append(c_out)
        out_shape.append(c_shape)
    pbuf_rows = sb * t if mc > t else SUBLANES
    return pl.pallas_call(
        functools.partial(_ret_body, sb=sb, ts=t, mc=mc, tt=tt, has_state=has_state,
                          cast_blocks=cast_blocks),
        grid=(b // sb, nhp),
        in_specs=in_specs,
        out_specs=out_specs,
        out_shape=out_shape,
        scratch_shapes=[pltpu.VMEM((4, pbuf_rows, hpb * dv), F32)],
        compiler_params=pltpu.CompilerParams(dimension_semantics=("arbitrary", "arbitrary"),
                                             vmem_limit_bytes=RESIDENT_VMEM_LIMIT),
        name="ret_branch",
    )(*args)


def _merge_body(h_ref, ya_ref, yb_ref, wma_ref, wmb_ref, wa_ref, wb_ref, o_ref, *, mc):
    def chunk(i, carry):
        r0 = pl.multiple_of(i * mc, mc)
        rows = pl.ds(r0, mc)
        hr = h_ref[rows, :]
        ma = _sigmoid(_dot(hr, wma_ref[...])) * _dot(ya_ref[rows, :], wa_ref[0])
        mb = _sigmoid(_dot(hr, wmb_ref[...])) * _dot(yb_ref[rows, :], wb_ref[0])
        o_ref[rows, :] = (ma + mb).astype(BF16)
        return carry

    lax.fori_loop(0, h_ref.shape[0] // mc, chunk, 0, unroll=CHUNK_UNROLL)


def _merge(h2, ya2, yb2, w_in, w_branch, tm, mc):
    n, d = h2.shape
    nb = 512
    nblk = d // nb
    ma_off = (w_in.shape[1] - 2 * d) // nb
    mb_off = ma_off + nblk
    act = pl.BlockSpec((tm, d), lambda i, j: (i, 0))
    in_specs = [act, act, act,
                pl.BlockSpec((d, nb), lambda i, j: (0, ma_off + j)),
                pl.BlockSpec((d, nb), lambda i, j: (0, mb_off + j)),
                pl.BlockSpec((1, d, nb), lambda i, j: (0, 0, j)),
                pl.BlockSpec((1, d, nb), lambda i, j: (1, 0, j))]
    args = [h2, ya2, yb2, w_in, w_in, w_branch, w_branch]
    return pl.pallas_call(
        functools.partial(_merge_body, mc=mc),
        grid=(n // tm, nblk),
        in_specs=in_specs,
        out_specs=[pl.BlockSpec((tm, nb), lambda i, j: (i, j))],
        out_shape=[jax.ShapeDtypeStruct((n, d), BF16)],
        compiler_params=_params(("arbitrary", "arbitrary")),
        name="merge",
    )(*args)


def _merge_rows_body(*refs, mc, cast_blocks):
    it = iter(refs)
    h_ref, ya_ref, yb_ref, wma_ref, wmb_ref, wa_ref, wb_ref = (next(it) for _ in range(7))
    wf_ref = next(it) if cast_blocks else None
    o_ref = next(it)
    wr_ref = next(it) if cast_blocks else None
    if cast_blocks:
        _side_cast(wf_ref, wr_ref, cast_blocks)
    tm, d = h_ref.shape
    cw = COL_BLOCK
    for sub in range(tm // mc):
        rows = slice(sub * mc, (sub + 1) * mc)
        hr, yar, ybr = h_ref[rows, :], ya_ref[rows, :], yb_ref[rows, :]
        for gi in range(d // cw):
            cols = slice(gi * cw, (gi + 1) * cw)
            ma = _sigmoid(_dot(hr, wma_ref[:, cols])) * _dot(yar, wa_ref[0, :, cols])
            mb = _sigmoid(_dot(hr, wmb_ref[:, cols])) * _dot(ybr, wb_ref[0, :, cols])
            o_ref[rows, cols] = (ma + mb).astype(BF16)


def _merge_rows(h2, ya2, yb2, w_in, w_branch, tm, mc, cast_f32=None):
    n, d = h2.shape
    ma_blk = w_in.shape[1] // d - 2
    act = pl.BlockSpec((tm, d), lambda i, j: (i, 0))
    resident = pl.Buffered(1)
    in_specs = [act, act, act,
                pl.BlockSpec((d, d), lambda i, j: (0, ma_blk), pipeline_mode=resident),
                pl.BlockSpec((d, d), lambda i, j: (0, ma_blk + 1), pipeline_mode=resident),
                pl.BlockSpec((1, d, d), lambda i, j: (0, 0, 0), pipeline_mode=resident),
                pl.BlockSpec((1, d, d), lambda i, j: (1, 0, 0), pipeline_mode=resident)]
    args = [h2, ya2, yb2, w_in, w_in, w_branch, w_branch]
    out_specs = [act]
    out_shape = [jax.ShapeDtypeStruct((n, d), BF16)]
    cast_blocks = 0
    if cast_f32 is not None:
        c_in, c_out, c_shape, cast_blocks = _cast_specs(cast_f32, 0, cast_f32.shape[2], n // tm,
                                                        lambda i, j: i)
        in_specs.append(c_in)
        args.append(cast_f32)
        out_specs.append(c_out)
        out_shape.append(c_shape)
    return pl.pallas_call(
        functools.partial(_merge_rows_body, mc=mc, cast_blocks=cast_blocks),
        grid=(n // tm, 1),
        in_specs=in_specs,
        out_specs=out_specs,
        out_shape=out_shape,
        compiler_params=pltpu.CompilerParams(dimension_semantics=("arbitrary", "arbitrary"),
                                             vmem_limit_bytes=RESIDENT_VMEM_LIMIT),
        name="merge_rows",
    )(*args)


def _out_body(m_ref, w_ref, x_ref, mod_ref, g_ref, y_ref, obuf, *, sb, ts, mc):
    g = g_ref[...]

    def finish(o, x, gate):
        ms = jnp.mean(o * o, axis=-1, keepdims=True)
        return x + gate * (o * lax.rsqrt(ms + EPS) * g)

    if mc <= ts:
        for s in range(sb):
            gate = mod_ref[s, 2:3, :]

            def chunk(i, carry, s=s, gate=gate):
                rows = pl.ds(pl.multiple_of(i * mc, mc), mc)
                o = _dot(m_ref[s, rows, :], w_ref[...])
                y_ref[s, rows, :] = finish(o, x_ref[s, rows, :], gate)
                return carry

            lax.fori_loop(0, ts // mc, chunk, 0, unroll=CHUNK_UNROLL)
    else:
        obuf[...] = _dot(m_ref[...].reshape(sb * ts, m_ref.shape[2]), w_ref[...])
        for s in range(sb):
            y_ref[s] = finish(obuf[s * ts:(s + 1) * ts, :], x_ref[s], mod_ref[s, 2:3, :])


def _out_proj(merged, w_out, x, mod3, g_post, sb, tr, mc):
    b, t, d = x.shape
    obuf_rows = sb * tr if mc > tr else SUBLANES
    act = lambda i, j: (i, j, 0)
    return pl.pallas_call(
        functools.partial(_out_body, sb=sb, ts=tr, mc=mc),
        grid=(b // sb, t // tr),
        in_specs=[pl.BlockSpec((sb, tr, d), act),
                  pl.BlockSpec((d, d), lambda i, j: (0, 0)),
                  pl.BlockSpec((sb, tr, d), act),
                  pl.BlockSpec((sb, 3, d), lambda i, j: (i, 0, 0)),
                  pl.BlockSpec((1, d), lambda i, j: (0, 0))],
        out_specs=pl.BlockSpec((sb, tr, d), act),
        out_shape=jax.ShapeDtypeStruct((b, t, d), F32),
        scratch_shapes=[pltpu.VMEM((obuf_rows, d), F32)],
        compiler_params=pltpu.CompilerParams(dimension_semantics=("arbitrary", "arbitrary"),
                                             vmem_limit_bytes=RESIDENT_VMEM_LIMIT),
        name="out_proj",
    )(merged, w_out, x, mod3, g_post.reshape(1, d))


class _LayerWeights:
    def __init__(self, w_in, w_branch, w_out, conv_cols):
        self.w_in, self.w_branch, self.w_out = w_in, w_branch, w_out
        self.conv = _cast_cols(w_in, conv_cols)
        self.rest = self.branch = self.out = None


def _mixer_layer(x, mod3, hist, state, pos0, g_pre, g_post, conv_w, conv_b, w):
    b, t, d = x.shape
    long_seq = t >= ROW_CHUNK
    sb = 1 if long_seq else b
    tr = 512 if long_seq else t
    mc = ROW_CHUNK if long_seq else b * t
    conv_cols = w.conv.shape[1]
    if long_seq:
        outs = _conv_rows(x, mod3, g_pre, w.conv, conv_w, conv_b, hist, tr, ROW_CHUNK,
                          cast_f32=w.w_in[None] if w.rest is None else None, cast_col0=conv_cols)
        h, ya, new_hist = outs[:3]
        if w.rest is None:
            w.rest = outs[3][0]
    else:
        if w.rest is None:
            w.rest = w.w_in[:, conv_cols:].astype(BF16)
        h = _prenorm(x, mod3, g_pre, sb, tr)
        ya, new_hist = _conv_seqs(h, w.conv, conv_w, conv_b, hist, sb)
    outs = _ret_branch(h, w.rest, state, pos0, sb, mc,
                       cast_f32=w.w_branch if long_seq and w.branch is None else None)
    yb, new_state = outs[:2]
    if w.branch is None:
        w.branch = outs[2] if long_seq else w.w_branch.astype(BF16)
    n = b * t
    acts = (h.reshape(n, d), ya.reshape(n, d), yb.reshape(n, d))
    if long_seq:
        outs = _merge_rows(*acts, w.rest, w.branch, tr, ROW_CHUNK,
                           cast_f32=w.w_out[None] if w.out is None else None)
    else:
        outs = _merge(*acts, w.rest, w.branch, n, ROW_CHUNK)
    if w.out is None:
        w.out = outs[1][0] if long_seq else w.w_out.astype(BF16)
    y = _out_proj(outs[0].reshape(b, t, d), w.out, x, mod3, g_post, sb, 2 * tr if long_seq else tr, mc)
    return y, new_hist, new_state


def kernel(x_prompt, x_sample, c_prompt, c_sample, state_conv, state_ret, ada_w, ada_b, norm_pre,
           norm_post, w_in, conv_w, conv_b, w_branch, w_out):
    depth = w_in.shape[0]
    bp = x_prompt.shape[0]
    d = x_prompt.shape[2]
    hp, hs = x_prompt, x_sample
    conv_p, ret_p, conv_s, ret_s = [], [], [], []
    c_all = jnp.concatenate([c_prompt, c_sample], axis=0)
    for l in range(depth):
        mod3 = _modulation(c_all, ada_w[l], ada_b[l]).reshape(c_all.shape[0], 3, d)
        shared = (norm_pre[l], norm_post[l], conv_w[l], conv_b[l],
                  _LayerWeights(w_in[l], w_branch[l], w_out[l], conv_cols=4 * d))
        hp, cp, rp = _mixer_layer(hp, mod3[:bp], None, None, 0, *shared)
        hs, cs, rs = _mixer_layer(hs, mod3[bp:], state_conv[l], state_ret[l], PAST_LEN, *shared)
        conv_p.append(cp)
        ret_p.append(rp)
        conv_s.append(cs)
        ret_s.append(rs)
    return (hp, hs, jnp.stack(conv_p), jnp.stack(ret_p), jnp.stack(conv_s), jnp.stack(ret_s))
```

```python
import functools

import jax
import jax.numpy as jnp
import numpy as np
from jax import lax
from jax.experimental import pallas as pl
from jax.experimental.pallas import tpu as pltpu

F32 = jnp.float32
BF16 = jnp.bfloat16

EPS = 1e-6
ROPE_BASE = 10000.0
REF_CHUNK = 64
N_HEADS = 8
CONV_WIDTH = 3

LANES = 128
SUBLANES = 8
COL_BLOCK = 256
ROW_CHUNK = 256
RET_TILE = 256
CHUNK_UNROLL = True
PAST_LEN = 1024
VMEM_LIMIT = 56 * 1024 * 1024
RESIDENT_VMEM_LIMIT = 60 * 1024 * 1024


def _dot(a, b):
    return jnp.dot(a, b, preferred_element_type=F32)


def _sigmoid(x):
    return 1.0 / (1.0 + jnp.exp(-x))


def _silu(x):
    return x * _sigmoid(x)


def _params(semantics):
    return pltpu.CompilerParams(dimension_semantics=semantics, vmem_limit_bytes=VMEM_LIMIT)


def _cast_specs(w_f32, col0, cols, n_steps, step_of):
    m, d, _ = w_f32.shape
    per = cols // COL_BLOCK
    assert cols % COL_BLOCK == 0 and col0 % COL_BLOCK == 0 and m * per <= n_steps

    def block(*ids):
        return jnp.minimum(step_of(*ids), m * per - 1)

    in_spec = pl.BlockSpec((1, d, COL_BLOCK),
                           lambda *ids: (block(*ids) // per, 0, col0 // COL_BLOCK + block(*ids) % per))
    out_spec = pl.BlockSpec((1, d, COL_BLOCK), lambda *ids: (block(*ids) // per, 0, block(*ids) % per))
    return in_spec, out_spec, jax.ShapeDtypeStruct((m, d, cols), BF16), m * per


def _cast_body(wf_ref, wr_ref):
    wr_ref[...] = wf_ref[...].astype(BF16)


def _cast_cols(w_f32, cols):
    d = w_f32.shape[0]
    nb = 2 * COL_BLOCK
    return pl.pallas_call(
        _cast_body,
        grid=(cols // nb,),
        in_specs=[pl.BlockSpec((d, nb), lambda j: (0, j))],
        out_specs=pl.BlockSpec((d, nb), lambda j: (0, j)),
        out_shape=jax.ShapeDtypeStruct((d, cols), BF16),
        compiler_params=_params(("arbitrary",)),
        name="cast_cols",
    )(w_f32)


def _side_cast(wf_ref, wr_ref, n_blocks):
    step = pl.program_id(0) * pl.num_programs(1) + pl.program_id(1)

    @pl.when(step < n_blocks)
    def _():
        wr_ref[...] = wf_ref[...].astype(BF16)


def _mod_body(c_ref, w_ref, b_ref, o_ref):
    a = _silu(c_ref[...]).astype(BF16)
    o_ref[...] = _dot(a, w_ref[...].astype(BF16)) + b_ref[...]


def _modulation(c_all, ada_w, ada_b):
    nseq, d = c_all.shape
    n_out = ada_w.shape[1]
    nb = 768
    return pl.pallas_call(
        _mod_body,
        grid=(n_out // nb,),
        in_specs=[pl.BlockSpec((nseq, d), lambda j: (0, 0)),
                  pl.BlockSpec((d, nb), lambda j: (0, j)),
                  pl.BlockSpec((1, nb), lambda j: (0, j))],
        out_specs=pl.BlockSpec((nseq, nb), lambda j: (0, j)),
        out_shape=jax.ShapeDtypeStruct((nseq, n_out), F32),
        compiler_params=_params(("arbitrary",)),
        name="modulation",
    )(c_all, ada_w, ada_b.reshape(1, n_out))


def _prenorm_body(x_ref, mod_ref, g_ref, h_ref):
    x = x_ref[...]
    ms = jnp.mean(x * x, axis=-1, keepdims=True)
    y = x * lax.rsqrt(ms + EPS) * g_ref[...]
    shift = mod_ref[:, 0:1, :]
    scale = mod_ref[:, 1:2, :]
    h_ref[...] = (y * (1.0 + scale) + shift).astype(BF16)


def _prenorm(x, mod3, g_pre, sb, tr):
    b, t, d = x.shape
    return pl.pallas_call(
        _prenorm_body,
        grid=(b // sb, t // tr),
        in_specs=[pl.BlockSpec((sb, tr, d), lambda i, j: (i, j, 0)),
                  pl.BlockSpec((sb, 3, d), lambda i, j: (i, 0, 0)),
                  pl.BlockSpec((1, d), lambda i, j: (0, 0))],
        out_specs=pl.BlockSpec((sb, tr, d), lambda i, j: (i, j, 0)),
        out_shape=jax.ShapeDtypeStruct((b, t, d), BF16),
        compiler_params=_params(("arbitrary", "arbitrary")),
        name="prenorm",
    )(x, mod3, g_pre.reshape(1, d))


def _conv_epilogue(ubuf, ec, taps, gb, gc, gu, zc):
    w0, w1, w2, cb = taps
    top = SUBLANES
    u = gc * gu
    ubuf[top:top + ec, :] = u
    f0 = ubuf[top - 2:top - 2 + ec, :]
    f1 = ubuf[top - 1:top - 1 + ec, :]
    conv = cb + w0 * f0 + w1 * f1 + w2 * u
    ubuf[top - 2:top, :] = ubuf[top + ec - 2:top + ec, :]
    return (gb * conv * _silu(zc)).astype(BF16)


def _conv_rows_body(*refs, mc, has_hist, cast_blocks):
    it = iter(refs)
    x_ref, mod_ref, g_ref, wb_ref, wc_ref, wu_ref, wz_ref, cw_ref, cb_ref = (next(it) for _ in range(9))
    hist_ref = next(it) if has_hist else None
    wf_ref = next(it) if cast_blocks else None
    h_ref, ya_ref, nh_ref = next(it), next(it), next(it)
    wr_ref = next(it) if cast_blocks else None
    ubuf = next(it)
    if cast_blocks:
        _side_cast(wf_ref, wr_ref, cast_blocks)
    tr, d = x_ref.shape[1], x_ref.shape[2]
    cw = ubuf.shape[2]
    ng = d // cw
    top = SUBLANES
    r = pl.program_id(1)

    @pl.when(r == 0)
    def _():
        for gi in range(ng):
            cols = slice(gi * cw, (gi + 1) * cw)
            ubuf[gi, top - 2:top, :] = hist_ref[0, :, cols] if has_hist else jnp.zeros((2, cw), F32)

    g_pre = g_ref[...]
    shift = mod_ref[0, 0:1, :]
    scale1 = 1.0 + mod_ref[0, 1:2, :]
    for sub in range(tr // mc):
        rows = slice(sub * mc, (sub + 1) * mc)
        x = x_ref[0, rows, :]
        ms = jnp.mean(x * x, axis=-1, keepdims=True)
        h_ref[0, rows, :] = (x * lax.rsqrt(ms + EPS) * g_pre * scale1 + shift).astype(BF16)
        hr = h_ref[0, rows, :]
        for gi in range(ng):
            cols = slice(gi * cw, (gi + 1) * cw)
            taps = (cw_ref[0:1, cols], cw_ref[1:2, cols], cw_ref[2:3, cols], cb_ref[:, cols])
            ya_ref[0, rows, cols] = _conv_epilogue(
                ubuf.at[gi], mc, taps, gc=_dot(hr, wc_ref[:, cols]), gu=_dot(hr, wu_ref[:, cols]),
                zc=_dot(hr, wz_ref[:, cols]), gb=_dot(hr, wb_ref[:, cols]))

    @pl.when(r == pl.num_programs(1) - 1)
    def _():
        for gi in range(ng):
            nh_ref[0, :, gi * cw:(gi + 1) * cw] = ubuf[gi, top - 2:top, :]


def _conv_rows(x, mod3, g_pre, w_conv, conv_w, conv_b, hist, tr, mc, cast_f32=None, cast_col0=0):
    b, t, d = x.shape
    cw = COL_BLOCK
    nr = t // tr
    has_hist = hist is not None
    row_block = lambda i, r: (i, r, 0)
    per_seq = lambda i, r: (i, 0, 0)
    whole = lambda i, r: (0, 0)
    in_specs = [pl.BlockSpec((1, tr, d), row_block),
                pl.BlockSpec((1, 3, d), per_seq),
                pl.BlockSpec((1, d), whole)]
    in_specs += [pl.BlockSpec((d, d), functools.partial(lambda i, r, k: (0, k), k=k),
                              pipeline_mode=pl.Buffered(1)) for k in range(4)]
    in_specs += [pl.BlockSpec((CONV_WIDTH, d), whole), pl.BlockSpec((1, d), whole)]
    args = [x, mod3, g_pre.reshape(1, d), w_conv, w_conv, w_conv, w_conv, conv_w, conv_b.reshape(1, d)]
    if has_hist:
        in_specs.append(pl.BlockSpec((1, CONV_WIDTH - 1, d), per_seq))
        args.append(hist)
    out_specs = [pl.BlockSpec((1, tr, d), row_block),
                 pl.BlockSpec((1, tr, d), row_block),
                 pl.BlockSpec((1, CONV_WIDTH - 1, d), per_seq)]
    out_shape = [jax.ShapeDtypeStruct((b, t, d), BF16),
                 jax.ShapeDtypeStruct((b, t, d), BF16),
                 jax.ShapeDtypeStruct((b, CONV_WIDTH - 1, d), F32)]
    cast_blocks = 0
    if cast_f32 is not None:
        c_in, c_out, c_shape, cast_blocks = _cast_specs(
            cast_f32, cast_col0, cast_f32.shape[2] - cast_col0, b * nr, lambda i, r: i * nr + r)
        in_specs.append(c_in)
        args.append(cast_f32)
        out_specs.append(c_out)
        out_shape.append(c_shape)
    return pl.pallas_call(
        functools.partial(_conv_rows_body, mc=mc, has_hist=has_hist, cast_blocks=cast_blocks),
        grid=(b, nr),
        in_specs=in_specs,
        out_specs=out_specs,
        out_shape=out_shape,
        scratch_shapes=[pltpu.VMEM((d // cw, mc + SUBLANES, cw), F32)],
        compiler_params=pltpu.CompilerParams(dimension_semantics=("arbitrary", "arbitrary"),
                                             vmem_limit_bytes=RESIDENT_VMEM_LIMIT),
        name="conv_rows",
    )(*args)


def _conv_seqs_body(*refs, sb, ts, has_hist):
    if has_hist:
        (h_ref, wb_ref, wc_ref, wu_ref, wz_ref, cw_ref, cb_ref, hist_ref,
         ya_ref, nh_ref, ubuf, pbuf) = refs
    else:
        (h_ref, wb_ref, wc_ref, wu_ref, wz_ref, cw_ref, cb_ref,
         ya_ref, nh_ref, ubuf, pbuf) = refs
    taps = (cw_ref[0:1, :], cw_ref[1:2, :], cw_ref[2:3, :], cb_ref[...])
    top = SUBLANES
    hr = h_ref[...].reshape(sb * ts, h_ref.shape[2])
    for k, w_ref in enumerate((wb_ref, wc_ref, wu_ref, wz_ref)):
        pbuf[k] = _dot(hr, w_ref[...])
    for s in range(sb):
        ubuf[top - 2:top, :] = hist_ref[s] if has_hist else jnp.zeros((2, ubuf.shape[1]), F32)
        rows = slice(s * ts, (s + 1) * ts)
        ya_ref[s] = _conv_epilogue(ubuf, ts, taps, pbuf[0, rows, :], pbuf[1, rows, :],
                                   pbuf[2, rows, :], pbuf[3, rows, :])
        nh_ref[s] = ubuf[top - 2:top, :]


def _conv_seqs(h, w_in, conv_w, conv_b, hist, sb):
    b, t, d = h.shape
    cw = COL_BLOCK
    ng = d // cw
    has_hist = hist is not None
    in_specs = [pl.BlockSpec((sb, t, d), lambda i, g: (i, 0, 0))]
    in_specs += [pl.BlockSpec((d, cw), functools.partial(lambda i, g, k: (0, k * ng + g), k=k))
                 for k in range(4)]
    in_specs += [pl.BlockSpec((CONV_WIDTH, cw), lambda i, g: (0, g)),
                 pl.BlockSpec((1, cw), lambda i, g: (0, g))]
    args = [h, w_in, w_in, w_in, w_in, conv_w, conv_b.reshape(1, d)]
    if has_hist:
        in_specs.append(pl.BlockSpec((sb, CONV_WIDTH - 1, cw), lambda i, g: (i, 0, g)))
        args.append(hist)
    return pl.pallas_call(
        functools.partial(_conv_seqs_body, sb=sb, ts=t, has_hist=has_hist),
        grid=(b // sb, ng),
        in_specs=in_specs,
        out_specs=[pl.BlockSpec((sb, t, cw), lambda i, g: (i, 0, g)),
                   pl.BlockSpec((sb, CONV_WIDTH - 1, cw), lambda i, g: (i, 0, g))],
        out_shape=[jax.ShapeDtypeStruct((b, t, d), BF16),
                   jax.ShapeDtypeStruct((b, CONV_WIDTH - 1, d), F32)],
        scratch_shapes=[pltpu.VMEM((t + SUBLANES, cw), F32),
                        pltpu.VMEM((4, sb * t, cw), F32)],
        compiler_params=_params(("arbitrary", "arbitrary")),
        name="conv_seqs",
    )(*args)


def _ret_body(*refs, sb, ts, mc, tt, has_state, cast_blocks):
    it = iter(refs)
    (h_ref, wq_ref, wk_ref, wv_ref, wz_ref, rq_ref, rk_ref, mask_ref, qd_ref, kd_ref,
     bd_ref) = (next(it) for _ in range(11))
    st_ref = next(it) if has_state else None
    wf_ref = next(it) if cast_blocks else None
    yb_ref, ns_ref = next(it), next(it)
    wr_ref = next(it) if cast_blocks else None
    pbuf = next(it)
    if cast_blocks:
        _side_cast(wf_ref, wr_ref, cast_blocks)
    dk = LANES
    dv = 2 * LANES
    hpb = 2

    def tile(s, r0, q2, k2, v2, z2, cq, sq, ck, sk):
        for hh in range(hpb):
            q = q2[:, hh * dk:(hh + 1) * dk]
            k = k2[:, hh * dk:(hh + 1) * dk]
            v = v2[:, hh * dv:(hh + 1) * dv].astype(BF16)
            z = z2[:, hh * dv:(hh + 1) * dv]
            qr = q * cq + pltpu.roll(q, dk // 2, axis=1) * sq
            kr = k * ck + pltpu.roll(k, dk // 2, axis=1) * sk
            state = ns_ref[s, hh]
            scores = lax.dot_general(qr.astype(BF16), kr.astype(BF16),
                                     (((1,), (1,)), ((), ())), preferred_element_type=F32)
            p = (scores * mask_ref[hh]).astype(BF16)
            o = _dot(p, v) + _dot((qr * qd_ref[hh]).astype(BF16), state.astype(BF16))
            kv = lax.dot_general((kr * kd_ref[hh]).astype(BF16), v,
                                 (((0,), (0,)), ((), ())), preferred_element_type=F32)
            ns_ref[s, hh] = bd_ref[hh] * state + kv
            on = o * lax.rsqrt(jnp.mean(o * o, axis=-1, keepdims=True) + EPS)
            yb_ref[s, pl.ds(r0, tt), hh * dv:(hh + 1) * dv] = (on * _silu(z)).astype(BF16)

    def projections(hr):
        return (_dot(hr, wq_ref[...]), _dot(hr, wk_ref[...]),
                _dot(hr, wv_ref[...]), _dot(hr, wz_ref[...]))

    if has_state:
        ns_ref[...] = st_ref[...]
    else:
        ns_ref[...] = jnp.zeros(ns_ref.shape, F32)

    if mc <= ts:
        assert mc == tt
        for s in range(sb):
            def chunk(i, carry, s=s):
                r0 = pl.multiple_of(i * mc, mc)
                q2, k2, v2, z2 = projections(h_ref[s, pl.ds(r0, mc), :])
                rows = pl.ds(r0, mc)
                tile(s, r0, q2, k2, v2, z2, rq_ref[0, rows, :], rq_ref[1, rows, :],
                     rk_ref[0, rows, :], rk_ref[1, rows, :])
                return carry

            lax.fori_loop(0, ts // mc, chunk, 0, unroll=CHUNK_UNROLL)
    else:
        assert tt == ts
        hr = h_ref[...].reshape(sb * ts, h_ref.shape[2])
        for k, p in enumerate(projections(hr)):
            pbuf[k, :, 0:p.shape[1]] = p
        for s in range(sb):
            rows = slice(s * ts, (s + 1) * ts)
            tile(s, 0, pbuf[0, rows, 0:hpb * dk], pbuf[1, rows, 0:hpb * dk],
                 pbuf[2, rows, :], pbuf[3, rows, :],
                 rq_ref[0], rq_ref[1], rk_ref[0], rk_ref[1])


def _retention_tables(t, tt, pos0, chunk):
    dk = LANES
    f32 = np.float32
    inv = f32(ROPE_BASE) ** (-np.arange(0, dk, 2, dtype=f32) / f32(dk))
    ang = (pos0 + np.arange(t)).astype(f32)[:, None] * inv[None, :]
    cos, sin = np.cos(ang), np.sin(ang)
    rope_q = np.stack([np.concatenate([cos, cos], -1), np.concatenate([-sin, sin], -1)])
    rope_k = rope_q * f32(dk ** -0.5)
    lg = np.log(f32(1.0) - f32(2.0) ** (f32(-5.0) - np.arange(N_HEADS, dtype=f32)))
    idx = np.arange(tt, dtype=f32)
    visible = (np.arange(tt)[None, :] // chunk) <= (np.arange(tt)[:, None] // chunk)
    mask = np.exp(lg[:, None, None] * np.abs(idx[:, None] - idx[None, :])) * visible.astype(f32)
    q_dec = np.broadcast_to(np.exp(lg[:, None] * (idx + f32(1.0)))[..., None], (N_HEADS, tt, dk))
    k_dec = np.broadcast_to(np.exp(lg[:, None] * (f32(tt - 1.0) - idx))[..., None], (N_HEADS, tt, dk))
    b_dec = np.broadcast_to(np.exp(lg * f32(tt))[:, None, None], (N_HEADS, 1, 2 * dk))
    return tuple(jnp.asarray(a, dtype=F32) for a in (rope_q, rope_k, mask, q_dec, k_dec, b_dec))


def _ret_branch(h, w_in, state, pos0, sb, mc, cast_f32=None):
    b, t, d = h.shape
    dk, dv, hpb = LANES, 2 * LANES, 2
    nhp = N_HEADS // hpb
    tt = min(RET_TILE, t)
    chunk = min(REF_CHUNK, t)
    has_state = state is not None
    rope_q, rope_k, mask, q_dec, k_dec, b_dec = _retention_tables(t, tt, pos0, chunk)
    q_off = 0
    k_off = q_off + nhp
    v_off = 2 * N_HEADS * dk // (hpb * dv)
    z_off = v_off + nhp
    in_specs = [pl.BlockSpec((sb, t, d), lambda i, g: (i, 0, 0)),
                pl.BlockSpec((d, hpb * dk), lambda i, g: (0, q_off + g)),
                pl.BlockSpec((d, hpb * dk), lambda i, g: (0, k_off + g)),
                pl.BlockSpec((d, hpb * dv), lambda i, g: (0, v_off + g)),
                pl.BlockSpec((d, hpb * dv), lambda i, g: (0, z_off + g)),
                pl.BlockSpec((2, t, dk), lambda i, g: (0, 0, 0)),
                pl.BlockSpec((2, t, dk), lambda i, g: (0, 0, 0)),
                pl.BlockSpec((hpb, tt, tt), lambda i, g: (g, 0, 0)),
                pl.BlockSpec((hpb, tt, dk), lambda i, g: (g, 0, 0)),
                pl.BlockSpec((hpb, tt, dk), lambda i, g: (g, 0, 0)),
                pl.BlockSpec((hpb, 1, dv), lambda i, g: (g, 0, 0))]
    args = [h, w_in, w_in, w_in, w_in, rope_q, rope_k, mask, q_dec, k_dec, b_dec]
    if has_state:
        in_specs.append(pl.BlockSpec((sb, hpb, dk, dv), lambda i, g: (i, g, 0, 0)))
        args.append(state)
    out_specs = [pl.BlockSpec((sb, t, hpb * dv), lambda i, g: (i, 0, g)),
                 pl.BlockSpec((sb, hpb, dk, dv), lambda i, g: (i, g, 0, 0))]
    out_shape = [jax.ShapeDtypeStruct((b, t, N_HEADS * dv), BF16),
                 jax.ShapeDtypeStruct((b, N_HEADS, dk, dv), F32)]
    cast_blocks = 0
    if cast_f32 is not None:
        c_in, c_out, c_shape, cast_blocks = _cast_specs(cast_f32, 0, cast_f32.shape[2], (b // sb) * nhp,
                                                        lambda i, g: i * nhp + g)
        in_specs.append(c_in)
        args.append(cast_f32)
        out_specs.append(c_out)
        out_shape.append(c_shape)
    pbuf_rows = sb * t if mc > t else SUBLANES
    return pl.pallas_call(
        functools.partial(_ret_body, sb=sb, ts=t, mc=mc, tt=tt, has_state=has_state,
                          cast_blocks=cast_blocks),
        grid=(b // sb, nhp),
        in_specs=in_specs,
        out_specs=out_specs,
        out_shape=out_shape,
        scratch_shapes=[pltpu.VMEM((4, pbuf_rows, hpb * dv), F32)],
        compiler_params=pltpu.CompilerParams(dimension_semantics=("arbitrary", "arbitrary"),
                                             vmem_limit_bytes=RESIDENT_VMEM_LIMIT),
        name="ret_branch",
    )(*args)


def _merge_body(h_ref, ya_ref, yb_ref, wma_ref, wmb_ref, wa_ref, wb_ref, o_ref, *, mc):
    def chunk(i, carry):
        r0 = pl.multiple_of(i * mc, mc)
        rows = pl.ds(r0, mc)
        hr = h_ref[rows, :]
        ma = _sigmoid(_dot(hr, wma_ref[...])) * _dot(ya_ref[rows, :], wa_ref[0])
        mb = _sigmoid(_dot(hr, wmb_ref[...])) * _dot(yb_ref[rows, :], wb_ref[0])
        o_ref[rows, :] = (ma + mb).astype(BF16)
        return carry

    lax.fori_loop(0, h_ref.shape[0] // mc, chunk, 0, unroll=CHUNK_UNROLL)


def _merge(h2, ya2, yb2, w_in, w_branch, tm, mc):
    n, d = h2.shape
    nb = 512
    nblk = d // nb
    ma_off = (w_in.shape[1] - 2 * d) // nb
    mb_off = ma_off + nblk
    act = pl.BlockSpec((tm, d), lambda i, j: (i, 0))
    in_specs = [act, act, act,
                pl.BlockSpec((d, nb), lambda i, j: (0, ma_off + j)),
                pl.BlockSpec((d, nb), lambda i, j: (0, mb_off + j)),
                pl.BlockSpec((1, d, nb), lambda i, j: (0, 0, j)),
                pl.BlockSpec((1, d, nb), lambda i, j: (1, 0, j))]
    args = [h2, ya2, yb2, w_in, w_in, w_branch, w_branch]
    return pl.pallas_call(
        functools.partial(_merge_body, mc=mc),
        grid=(n // tm, nblk),
        in_specs=in_specs,
        out_specs=[pl.BlockSpec((tm, nb), lambda i, j: (i, j))],
        out_shape=[jax.ShapeDtypeStruct((n, d), BF16)],
        compiler_params=_params(("arbitrary", "arbitrary")),
        name="merge",
    )(*args)


def _merge_rows_body(*refs, mc, cast_blocks):
    it = iter(refs)
    h_ref, ya_ref, yb_ref, wma_ref, wmb_ref, wa_ref, wb_ref = (next(it) for _ in range(7))
    wf_ref = next(it) if cast_blocks else None
    o_ref = next(it)
    wr_ref = next(it) if cast_blocks else None
    if cast_blocks:
        _side_cast(wf_ref, wr_ref, cast_blocks)
    tm, d = h_ref.shape
    cw = COL_BLOCK
    for sub in range(tm // mc):
        rows = slice(sub * mc, (sub + 1) * mc)
        hr, yar, ybr = h_ref[rows, :], ya_ref[rows, :], yb_ref[rows, :]
        for gi in range(d // cw):
            cols = slice(gi * cw, (gi + 1) * cw)
            ma = _sigmoid(_dot(hr, wma_ref[:, cols])) * _dot(yar, wa_ref[0, :, cols])
            mb = _sigmoid(_dot(hr, wmb_ref[:, cols])) * _dot(ybr, wb_ref[0, :, cols])
            o_ref[rows, cols] = (ma + mb).astype(BF16)


def _merge_rows(h2, ya2, yb2, w_in, w_branch, tm, mc, cast_f32=None):
    n, d = h2.shape
    ma_blk = w_in.shape[1] // d - 2
    act = pl.BlockSpec((tm, d), lambda i, j: (i, 0))
    resident = pl.Buffered(1)
    in_specs = [act, act, act,
                pl.BlockSpec((d, d), lambda i, j: (0, ma_blk), pipeline_mode=resident),
                pl.BlockSpec((d, d), lambda i, j: (0, ma_blk + 1), pipeline_mode=resident),
                pl.BlockSpec((1, d, d), lambda i, j: (0, 0, 0), pipeline_mode=resident),
                pl.BlockSpec((1, d, d), lambda i, j: (1, 0, 0), pipeline_mode=resident)]
    args = [h2, ya2, yb2, w_in, w_in, w_branch, w_branch]
    out_specs = [act]
    out_shape = [jax.ShapeDtypeStruct((n, d), BF16)]
    cast_blocks = 0
    if cast_f32 is not None:
        c_in, c_out, c_shape, cast_blocks = _cast_specs(cast_f32, 0, cast_f32.shape[2], n // tm,
                                                        lambda i, j: i)
        in_specs.append(c_in)
        args.append(cast_f32)
        out_specs.append(c_out)
        out_shape.append(c_shape)
    return pl.pallas_call(
        functools.partial(_merge_rows_body, mc=mc, cast_blocks=cast_blocks),
        grid=(n // tm, 1),
        in_specs=in_specs,
        out_specs=out_specs,
        out_shape=out_shape,
        compiler_params=pltpu.CompilerParams(dimension_semantics=("arbitrary", "arbitrary"),
                                             vmem_limit_bytes=RESIDENT_VMEM_LIMIT),
        name="merge_rows",
    )(*args)


def _out_body(m_ref, w_ref, x_ref, mod_ref, g_ref, y_ref, obuf, *, sb, ts, mc):
    g = g_ref[...]

    def finish(o, x, gate):
        ms = jnp.mean(o * o, axis=-1, keepdims=True)
        return x + gate * (o * lax.rsqrt(ms + EPS) * g)

    if mc <= ts:
        d = w_ref.shape[1]
        cw = COL_BLOCK
        for s in range(sb):
            gain = mod_ref[s, 2:3, :] * g
            for c in range(ts // mc):
                rows = slice(c * mc, (c + 1) * mc)
                mr = m_ref[s, rows, :]
                ss = jnp.zeros((mc, 1), F32)
                for gi in range(d // cw):
                    cols = slice(gi * cw, (gi + 1) * cw)
                    o = _dot(mr, w_ref[:, cols])
                    obuf[rows, cols] = o
                    ss = ss + jnp.sum(o * o, axis=-1, keepdims=True)
                rs = lax.rsqrt(ss * (1.0 / d) + EPS)
                for gi in range(d // cw):
                    cols = slice(gi * cw, (gi + 1) * cw)
                    y_ref[s, rows, cols] = x_ref[s, rows, cols] + gain[:, cols] * (obuf[rows, cols] * rs)
    else:
        obuf[...] = _dot(m_ref[...].reshape(sb * ts, m_ref.shape[2]), w_ref[...])
        for s in range(sb):
            y_ref[s] = finish(obuf[s * ts:(s + 1) * ts, :], x_ref[s], mod_ref[s, 2:3, :])


def _out_proj(merged, w_out, x, mod3, g_post, sb, tr, mc):
    b, t, d = x.shape
    obuf_rows = sb * tr
    act = lambda i, j: (i, j, 0)
    return pl.pallas_call(
        functools.partial(_out_body, sb=sb, ts=tr, mc=mc),
        grid=(b // sb, t // tr),
        in_specs=[pl.BlockSpec((sb, tr, d), act),
                  pl.BlockSpec((d, d), lambda i, j: (0, 0)),
                  pl.BlockSpec((sb, tr, d), act),
                  pl.BlockSpec((sb, 3, d), lambda i, j: (i, 0, 0)),
                  pl.BlockSpec((1, d), lambda i, j: (0, 0))],
        out_specs=pl.BlockSpec((sb, tr, d), act),
        out_shape=jax.ShapeDtypeStruct((b, t, d), F32),
        scratch_shapes=[pltpu.VMEM((obuf_rows, d), F32)],
        compiler_params=pltpu.CompilerParams(dimension_semantics=("arbitrary", "arbitrary"),
                                             vmem_limit_bytes=RESIDENT_VMEM_LIMIT),
        name="out_proj",
    )(merged, w_out, x, mod3, g_post.reshape(1, d))


class _LayerWeights:
    def __init__(self, w_in, w_branch, w_out, conv_cols):
        self.w_in, self.w_branch, self.w_out = w_in, w_branch, w_out
        self.conv = _cast_cols(w_in, conv_cols)
        self.rest = self.branch = self.out = None


def _mixer_layer(x, mod3, hist, state, pos0, g_pre, g_post, conv_w, conv_b, w):
    b, t, d = x.shape
    long_seq = t >= ROW_CHUNK
    sb = 1 if long_seq else b
    tr = 512 if long_seq else t
    mc = ROW_CHUNK if long_seq else b * t
    conv_cols = w.conv.shape[1]
    if long_seq:
        outs = _conv_rows(x, mod3, g_pre, w.conv, conv_w, conv_b, hist, tr, ROW_CHUNK,
                          cast_f32=w.w_in[None] if w.rest is None else None, cast_col0=conv_cols)
        h, ya, new_hist = outs[:3]
        if w.rest is None:
            w.rest = outs[3][0]
    else:
        if w.rest is None:
            w.rest = w.w_in[:, conv_cols:].astype(BF16)
        h = _prenorm(x, mod3, g_pre, sb, tr)
        ya, new_hist = _conv_seqs(h, w.conv, conv_w, conv_b, hist, sb)
    outs = _ret_branch(h, w.rest, state, pos0, sb, mc,
                       cast_f32=w.w_branch if long_seq and w.branch is None else None)
    yb, new_state = outs[:2]
    if w.branch is None:
        w.branch = outs[2] if long_seq else w.w_branch.astype(BF16)
    n = b * t
    acts = (h.reshape(n, d), ya.reshape(n, d), yb.reshape(n, d))
    if long_seq:
        outs = _merge_rows(*acts, w.rest, w.branch, tr, ROW_CHUNK,
                           cast_f32=w.w_out[None] if w.out is None else None)
    else:
        outs = _merge(*acts, w.rest, w.branch, n, ROW_CHUNK)
    if w.out is None:
        w.out = outs[1][0] if long_seq else w.w_out.astype(BF16)
    y = _out_proj(outs[0].reshape(b, t, d), w.out, x, mod3, g_post, sb, 2 * tr if long_seq else tr, mc)
    return y, new_hist, new_state


def kernel(x_prompt, x_sample, c_prompt, c_sample, state_conv, state_ret, ada_w, ada_b, norm_pre,
           norm_post, w_in, conv_w, conv_b, w_branch, w_out):
    depth = w_in.shape[0]
    bp = x_prompt.shape[0]
    d = x_prompt.shape[2]
    hp, hs = x_prompt, x_sample
    conv_p, ret_p, conv_s, ret_s = [], [], [], []
    c_all = jnp.concatenate([c_prompt, c_sample], axis=0)
    for l in range(depth):
        mod3 = _modulation(c_all, ada_w[l], ada_b[l]).reshape(c_all.shape[0], 3, d)
        shared = (norm_pre[l], norm_post[l], conv_w[l], conv_b[l],
                  _LayerWeights(w_in[l], w_branch[l], w_out[l], conv_cols=4 * d))
        hp, cp, rp = _mixer_layer(hp, mod3[:bp], None, None, 0, *shared)
        hs, cs, rs = _mixer_layer(hs, mod3[bp:], state_conv[l], state_ret[l], PAST_LEN, *shared)
        conv_p.append(cp)
        ret_p.append(rp)
        conv_s.append(cs)
        ret_s.append(rs)
    return (hp, hs, jnp.stack(conv_p), jnp.stack(ret_p), jnp.stack(conv_s), jnp.stack(ret_s))
```

```python
import functools

import jax
import jax.numpy as jnp
import numpy as np
from jax import lax
from jax.experimental import pallas as pl
from jax.experimental.pallas import tpu as pltpu

F32 = jnp.float32
BF16 = jnp.bfloat16

EPS = 1e-6
ROPE_BASE = 10000.0
REF_CHUNK = 64
N_HEADS = 8
CONV_WIDTH = 3

LANES = 128
SUBLANES = 8
COL_BLOCK = 256
ROW_CHUNK = 256
RET_TILE = 256
CHUNK_UNROLL = True
PAST_LEN = 1024
VMEM_LIMIT = 56 * 1024 * 1024
RESIDENT_VMEM_LIMIT = 60 * 1024 * 1024


def _dot(a, b):
    return jnp.dot(a, b, preferred_element_type=F32)


def _sigmoid(x):
    return 1.0 / (1.0 + jnp.exp(-x))


def _silu(x):
    return x * _sigmoid(x)


def _params(semantics):
    return pltpu.CompilerParams(dimension_semantics=semantics, vmem_limit_bytes=VMEM_LIMIT)


def _cast_specs(w_f32, col0, cols, n_steps, step_of):
    m, d, _ = w_f32.shape
    per = cols // COL_BLOCK
    assert cols % COL_BLOCK == 0 and col0 % COL_BLOCK == 0 and m * per <= n_steps

    def block(*ids):
        return jnp.minimum(step_of(*ids), m * per - 1)

    in_spec = pl.BlockSpec((1, d, COL_BLOCK),
                           lambda *ids: (block(*ids) // per, 0, col0 // COL_BLOCK + block(*ids) % per))
    out_spec = pl.BlockSpec((1, d, COL_BLOCK), lambda *ids: (block(*ids) // per, 0, block(*ids) % per))
    return in_spec, out_spec, jax.ShapeDtypeStruct((m, d, cols), BF16), m * per


def _side_cast(wf_ref, wr_ref, n_blocks):
    step = pl.program_id(0) * pl.num_programs(1) + pl.program_id(1)

    @pl.when(step < n_blocks)
    def _():
        wr_ref[...] = wf_ref[...].astype(BF16)


def _mod_body(cp_ref, cs_ref, w_ref, b_ref, wf_ref, op_ref, os_ref, wr_ref):
    wr_ref[...] = wf_ref[...].astype(BF16)
    w = w_ref[...].astype(BF16)
    op_ref[...] = _dot(_silu(cp_ref[...]).astype(BF16), w) + b_ref[...]
    os_ref[...] = _dot(_silu(cs_ref[...]).astype(BF16), w) + b_ref[...]


def _modulation(c_a, c_b, ada_w, ada_b, w_f32, cast_cols):
    d = c_a.shape[1]
    n_out = ada_w.shape[1]
    nb = 768
    steps = n_out // nb
    cb = cast_cols // steps
    assert cast_cols % steps == 0 and cb % LANES == 0
    cond = lambda c: pl.BlockSpec((c.shape[0], d), lambda j: (0, 0))
    mod = lambda c: pl.BlockSpec((c.shape[0], nb), lambda j: (0, j))
    return pl.pallas_call(
        _mod_body,
        grid=(steps,),
        in_specs=[cond(c_a), cond(c_b),
                  pl.BlockSpec((d, nb), lambda j: (0, j)),
                  pl.BlockSpec((1, nb), lambda j: (0, j)),
                  pl.BlockSpec((d, cb), lambda j: (0, j))],
        out_specs=[mod(c_a), mod(c_b), pl.BlockSpec((d, cb), lambda j: (0, j))],
        out_shape=[jax.ShapeDtypeStruct((c_a.shape[0], n_out), F32),
                   jax.ShapeDtypeStruct((c_b.shape[0], n_out), F32),
                   jax.ShapeDtypeStruct((d, cast_cols), BF16)],
        compiler_params=_params(("arbitrary",)),
        name="modulation",
    )(c_a, c_b, ada_w, ada_b.reshape(1, n_out), w_f32)


def _prenorm(x, g_pre, scale1, shift):
    ms = jnp.mean(x * x, axis=-1, keepdims=True)
    return (x * lax.rsqrt(ms + EPS) * g_pre * scale1 + shift).astype(BF16)


def _conv_epilogue(ubuf, ec, taps, gb, gc, gu, zc):
    w0, w1, w2, cb = taps
    top = SUBLANES
    u = gc * gu
    ubuf[top:top + ec, :] = u
    f0 = ubuf[top - 2:top - 2 + ec, :]
    f1 = ubuf[top - 1:top - 1 + ec, :]
    conv = cb + w0 * f0 + w1 * f1 + w2 * u
    ubuf[top - 2:top, :] = ubuf[top + ec - 2:top + ec, :]
    return (gb * conv * _silu(zc)).astype(BF16)


def _conv_rows_body(*refs, mc, has_hist, cast_blocks):
    it = iter(refs)
    x_ref, mod_ref, g_ref, wb_ref, wc_ref, wu_ref, wz_ref, cw_ref, cb_ref = (next(it) for _ in range(9))
    hist_ref = next(it) if has_hist else None
    wf_ref = next(it) if cast_blocks else None
    h_ref, ya_ref, nh_ref = next(it), next(it), next(it)
    wr_ref = next(it) if cast_blocks else None
    ubuf = next(it)
    if cast_blocks:
        _side_cast(wf_ref, wr_ref, cast_blocks)
    tr, d = x_ref.shape[1], x_ref.shape[2]
    cw = ubuf.shape[2]
    ng = d // cw
    top = SUBLANES
    r = pl.program_id(1)

    @pl.when(r == 0)
    def _():
        for gi in range(ng):
            cols = slice(gi * cw, (gi + 1) * cw)
            ubuf[gi, top - 2:top, :] = hist_ref[0, :, cols] if has_hist else jnp.zeros((2, cw), F32)

    g_pre = g_ref[...]
    shift = mod_ref[0, 0:1, :]
    scale1 = 1.0 + mod_ref[0, 1:2, :]
    for sub in range(tr // mc):
        rows = slice(sub * mc, (sub + 1) * mc)
        h_ref[0, rows, :] = _prenorm(x_ref[0, rows, :], g_pre, scale1, shift)
        hr = h_ref[0, rows, :]
        for gi in range(ng):
            cols = slice(gi * cw, (gi + 1) * cw)
            taps = (cw_ref[0:1, cols], cw_ref[1:2, cols], cw_ref[2:3, cols], cb_ref[:, cols])
            ya_ref[0, rows, cols] = _conv_epilogue(
                ubuf.at[gi], mc, taps, gc=_dot(hr, wc_ref[:, cols]), gu=_dot(hr, wu_ref[:, cols]),
                zc=_dot(hr, wz_ref[:, cols]), gb=_dot(hr, wb_ref[:, cols]))

    @pl.when(r == pl.num_programs(1) - 1)
    def _():
        for gi in range(ng):
            nh_ref[0, :, gi * cw:(gi + 1) * cw] = ubuf[gi, top - 2:top, :]


def _conv_rows(x, mod3, g_pre, w_conv, conv_w, conv_b, hist, tr, mc, cast_f32=None, cast_col0=0):
    b, t, d = x.shape
    cw = COL_BLOCK
    nr = t // tr
    has_hist = hist is not None
    row_block = lambda i, r: (i, r, 0)
    per_seq = lambda i, r: (i, 0, 0)
    whole = lambda i, r: (0, 0)
    in_specs = [pl.BlockSpec((1, tr, d), row_block),
                pl.BlockSpec((1, 3, d), per_seq),
                pl.BlockSpec((1, d), whole)]
    in_specs += [pl.BlockSpec((d, d), functools.partial(lambda i, r, k: (0, k), k=k),
                              pipeline_mode=pl.Buffered(1)) for k in range(4)]
    in_specs += [pl.BlockSpec((CONV_WIDTH, d), whole), pl.BlockSpec((1, d), whole)]
    args = [x, mod3, g_pre.reshape(1, d), w_conv, w_conv, w_conv, w_conv, conv_w, conv_b.reshape(1, d)]
    if has_hist:
        in_specs.append(pl.BlockSpec((1, CONV_WIDTH - 1, d), per_seq))
        args.append(hist)
    out_specs = [pl.BlockSpec((1, tr, d), row_block),
                 pl.BlockSpec((1, tr, d), row_block),
                 pl.BlockSpec((1, CONV_WIDTH - 1, d), per_seq)]
    out_shape = [jax.ShapeDtypeStruct((b, t, d), BF16),
                 jax.ShapeDtypeStruct((b, t, d), BF16),
                 jax.ShapeDtypeStruct((b, CONV_WIDTH - 1, d), F32)]
    cast_blocks = 0
    if cast_f32 is not None:
        c_in, c_out, c_shape, cast_blocks = _cast_specs(
            cast_f32, cast_col0, cast_f32.shape[2] - cast_col0, b * nr, lambda i, r: i * nr + r)
        in_specs.append(c_in)
        args.append(cast_f32)
        out_specs.append(c_out)
        out_shape.append(c_shape)
    return pl.pallas_call(
        functools.partial(_conv_rows_body, mc=mc, has_hist=has_hist, cast_blocks=cast_blocks),
        grid=(b, nr),
        in_specs=in_specs,
        out_specs=out_specs,
        out_shape=out_shape,
        scratch_shapes=[pltpu.VMEM((d // cw, mc + SUBLANES, cw), F32)],
        compiler_params=pltpu.CompilerParams(dimension_semantics=("arbitrary", "arbitrary"),
                                             vmem_limit_bytes=RESIDENT_VMEM_LIMIT),
        name="conv_rows",
    )(*args)


def _conv_seqs_body(*refs, sb, ts, has_hist):
    if has_hist:
        (x_ref, mod_ref, g_ref, wb_ref, wc_ref, wu_ref, wz_ref, cw_ref, cb_ref, hist_ref,
         h_ref, ya_ref, nh_ref, ubuf, pbuf) = refs
    else:
        (x_ref, mod_ref, g_ref, wb_ref, wc_ref, wu_ref, wz_ref, cw_ref, cb_ref,
         h_ref, ya_ref, nh_ref, ubuf, pbuf) = refs

    @pl.when(pl.program_id(1) == 0)
    def _():
        h_ref[...] = _prenorm(x_ref[...], g_ref[...], 1.0 + mod_ref[:, 1:2, :], mod_ref[:, 0:1, :])

    taps = (cw_ref[0:1, :], cw_ref[1:2, :], cw_ref[2:3, :], cb_ref[...])
    top = SUBLANES
    hr = h_ref[...].reshape(sb * ts, h_ref.shape[2])
    for k, w_ref in enumerate((wb_ref, wc_ref, wu_ref, wz_ref)):
        pbuf[k] = _dot(hr, w_ref[...])
    for s in range(sb):
        ubuf[top - 2:top, :] = hist_ref[s] if has_hist else jnp.zeros((2, ubuf.shape[1]), F32)
        rows = slice(s * ts, (s + 1) * ts)
        ya_ref[s] = _conv_epilogue(ubuf, ts, taps, pbuf[0, rows, :], pbuf[1, rows, :],
                                   pbuf[2, rows, :], pbuf[3, rows, :])
        nh_ref[s] = ubuf[top - 2:top, :]


def _conv_seqs(x, mod3, g_pre, w_in, conv_w, conv_b, hist, sb):
    b, t, d = x.shape
    cw = COL_BLOCK
    ng = d // cw
    has_hist = hist is not None
    seq_block = lambda i, g: (i, 0, 0)
    in_specs = [pl.BlockSpec((sb, t, d), seq_block),
                pl.BlockSpec((sb, 3, d), seq_block),
                pl.BlockSpec((1, d), lambda i, g: (0, 0))]
    in_specs += [pl.BlockSpec((d, cw), functools.partial(lambda i, g, k: (0, k * ng + g), k=k))
                 for k in range(4)]
    in_specs += [pl.BlockSpec((CONV_WIDTH, cw), lambda i, g: (0, g)),
                 pl.BlockSpec((1, cw), lambda i, g: (0, g))]
    args = [x, mod3, g_pre.reshape(1, d), w_in, w_in, w_in, w_in, conv_w, conv_b.reshape(1, d)]
    if has_hist:
        in_specs.append(pl.BlockSpec((sb, CONV_WIDTH - 1, cw), lambda i, g: (i, 0, g)))
        args.append(hist)
    return pl.pallas_call(
        functools.partial(_conv_seqs_body, sb=sb, ts=t, has_hist=has_hist),
        grid=(b // sb, ng),
        in_specs=in_specs,
        out_specs=[pl.BlockSpec((sb, t, d), seq_block),
                   pl.BlockSpec((sb, t, cw), lambda i, g: (i, 0, g)),
                   pl.BlockSpec((sb, CONV_WIDTH - 1, cw), lambda i, g: (i, 0, g))],
        out_shape=[jax.ShapeDtypeStruct((b, t, d), BF16),
                   jax.ShapeDtypeStruct((b, t, d), BF16),
                   jax.ShapeDtypeStruct((b, CONV_WIDTH - 1, d), F32)],
        scratch_shapes=[pltpu.VMEM((t + SUBLANES, cw), F32),
                        pltpu.VMEM((4, sb * t, cw), F32)],
        compiler_params=_params(("arbitrary", "arbitrary")),
        name="conv_seqs",
    )(*args)


def _ret_body(*refs, sb, ts, mc, tt, has_state, cast_blocks):
    it = iter(refs)
    (h_ref, wq_ref, wk_ref, wv_ref, wz_ref, rq_ref, rk_ref, mask_ref, qd_ref, kd_ref,
     bd_ref) = (next(it) for _ in range(11))
    st_ref = next(it) if has_state else None
    wf_ref = next(it) if cast_blocks else None
    yb_ref, ns_ref = next(it), next(it)
    wr_ref = next(it) if cast_blocks else None
    pbuf = next(it)
    if cast_blocks:
        _side_cast(wf_ref, wr_ref, cast_blocks)
    dk = LANES
    dv = 2 * LANES
    hpb = 2

    def tile(s, r0, q2, k2, v2, z2, cq, sq, ck, sk):
        for hh in range(hpb):
            q = q2[:, hh * dk:(hh + 1) * dk]
            k = k2[:, hh * dk:(hh + 1) * dk]
            v = v2[:, hh * dv:(hh + 1) * dv].astype(BF16)
            z = z2[:, hh * dv:(hh + 1) * dv]
            qr = q * cq + pltpu.roll(q, dk // 2, axis=1) * sq
            kr = k * ck + pltpu.roll(k, dk // 2, axis=1) * sk
            state = ns_ref[s, hh]
            scores = lax.dot_general(qr.astype(BF16), kr.astype(BF16),
                                     (((1,), (1,)), ((), ())), preferred_element_type=F32)
            p = (scores * mask_ref[hh]).astype(BF16)
            o = _dot(p, v) + _dot((qr * qd_ref[hh]).astype(BF16), state.astype(BF16))
            kv = lax.dot_general((kr * kd_ref[hh]).astype(BF16), v,
                                 (((0,), (0,)), ((), ())), preferred_element_type=F32)
            ns_ref[s, hh] = bd_ref[hh] * state + kv
            on = o * lax.rsqrt(jnp.mean(o * o, axis=-1, keepdims=True) + EPS)
            yb_ref[s, pl.ds(r0, tt), hh * dv:(hh + 1) * dv] = (on * _silu(z)).astype(BF16)

    def projections(hr):
        return (_dot(hr, wq_ref[...]), _dot(hr, wk_ref[...]),
                _dot(hr, wv_ref[...]), _dot(hr, wz_ref[...]))

    if has_state:
        ns_ref[...] = st_ref[...]
    else:
        ns_ref[...] = jnp.zeros(ns_ref.shape, F32)

    if mc <= ts:
        assert mc == tt
        for s in range(sb):
            def chunk(i, carry, s=s):
                r0 = pl.multiple_of(i * mc, mc)
                q2, k2, v2, z2 = projections(h_ref[s, pl.ds(r0, mc), :])
                rows = pl.ds(r0, mc)
                tile(s, r0, q2, k2, v2, z2, rq_ref[0, rows, :], rq_ref[1, rows, :],
                     rk_ref[0, rows, :], rk_ref[1, rows, :])
                return carry

            lax.fori_loop(0, ts // mc, chunk, 0, unroll=CHUNK_UNROLL)
    else:
        assert tt == ts
        hr = h_ref[...].reshape(sb * ts, h_ref.shape[2])
        for k, p in enumerate(projections(hr)):
            pbuf[k, :, 0:p.shape[1]] = p
        for s in range(sb):
            rows = slice(s * ts, (s + 1) * ts)
            tile(s, 0, pbuf[0, rows, 0:hpb * dk], pbuf[1, rows, 0:hpb * dk],
                 pbuf[2, rows, :], pbuf[3, rows, :],
                 rq_ref[0], rq_ref[1], rk_ref[0], rk_ref[1])


def _retention_tables(t, tt, pos0, chunk):
    dk = LANES
    f32 = np.float32
    inv = f32(ROPE_BASE) ** (-np.arange(0, dk, 2, dtype=f32) / f32(dk))
    ang = (pos0 + np.arange(t)).astype(f32)[:, None] * inv[None, :]
    cos, sin = np.cos(ang), np.sin(ang)
    rope_q = np.stack([np.concatenate([cos, cos], -1), np.concatenate([-sin, sin], -1)])
    rope_k = rope_q * f32(dk ** -0.5)
    lg = np.log(f32(1.0) - f32(2.0) ** (f32(-5.0) - np.arange(N_HEADS, dtype=f32)))
    idx = np.arange(tt, dtype=f32)
    visible = (np.arange(tt)[None, :] // chunk) <= (np.arange(tt)[:, None] // chunk)
    mask = np.exp(lg[:, None, None] * np.abs(idx[:, None] - idx[None, :])) * visible.astype(f32)
    q_dec = np.broadcast_to(np.exp(lg[:, None] * (idx + f32(1.0)))[..., None], (N_HEADS, tt, dk))
    k_dec = np.broadcast_to(np.exp(lg[:, None] * (f32(tt - 1.0) - idx))[..., None], (N_HEADS, tt, dk))
    b_dec = np.broadcast_to(np.exp(lg * f32(tt))[:, None, None], (N_HEADS, 1, 2 * dk))
    return tuple(jnp.asarray(a, dtype=F32) for a in (rope_q, rope_k, mask, q_dec, k_dec, b_dec))


def _ret_branch(h, w_in, state, pos0, sb, mc, cast_f32=None):
    b, t, d = h.shape
    dk, dv, hpb = LANES, 2 * LANES, 2
    nhp = N_HEADS // hpb
    tt = min(RET_TILE, t)
    chunk = min(REF_CHUNK, t)
    has_state = state is not None
    rope_q, rope_k, mask, q_dec, k_dec, b_dec = _retention_tables(t, tt, pos0, chunk)
    q_off = 0
    k_off = q_off + nhp
    v_off = 2 * N_HEADS * dk // (hpb * dv)
    z_off = v_off + nhp
    in_specs = [pl.BlockSpec((sb, t, d), lambda i, g: (i, 0, 0)),
                pl.BlockSpec((d, hpb * dk), lambda i, g: (0, q_off + g)),
                pl.BlockSpec((d, hpb * dk), lambda i, g: (0, k_off + g)),
                pl.BlockSpec((d, hpb * dv), lambda i, g: (0, v_off + g)),
                pl.BlockSpec((d, hpb * dv), lambda i, g: (0, z_off + g)),
                pl.BlockSpec((2, t, dk), lambda i, g: (0, 0, 0)),
                pl.BlockSpec((2, t, dk), lambda i, g: (0, 0, 0)),
                pl.BlockSpec((hpb, tt, tt), lambda i, g: (g, 0, 0)),
                pl.BlockSpec((hpb, tt, dk), lambda i, g: (g, 0, 0)),
                pl.BlockSpec((hpb, tt, dk), lambda i, g: (g, 0, 0)),
                pl.BlockSpec((hpb, 1, dv), lambda i, g: (g, 0, 0))]
    args = [h, w_in, w_in, w_in, w_in, rope_q, rope_k, mask, q_dec, k_dec, b_dec]
    if has_state:
        in_specs.append(pl.BlockSpec((sb, hpb, dk, dv), lambda i, g: (i, g, 0, 0)))
        args.append(state)
    out_specs = [pl.BlockSpec((sb, t, hpb * dv), lambda i, g: (i, 0, g)),
                 pl.BlockSpec((sb, hpb, dk, dv), lambda i, g: (i, g, 0, 0))]
    out_shape = [jax.ShapeDtypeStruct((b, t, N_HEADS * dv), BF16),
                 jax.ShapeDtypeStruct((b, N_HEADS, dk, dv), F32)]
    cast_blocks = 0
    if cast_f32 is not None:
        c_in, c_out, c_shape, cast_blocks = _cast_specs(cast_f32, 0, cast_f32.shape[2], (b // sb) * nhp,
                                                        lambda i, g: i * nhp + g)
        in_specs.append(c_in)
        args.append(cast_f32)
        out_specs.append(c_out)
        out_shape.append(c_shape)
    pbuf_rows = sb * t if mc > t else SUBLANES
    return pl.pallas_call(
        functools.partial(_ret_body, sb=sb, ts=t, mc=mc, tt=tt, has_state=has_state,
                          cast_blocks=cast_blocks),
        grid=(b // sb, nhp),
        in_specs=in_specs,
        out_specs=out_specs,
        out_shape=out_shape,
        scratch_shapes=[pltpu.VMEM((4, pbuf_rows, hpb * dv), F32)],
        compiler_params=pltpu.CompilerParams(dimension_semantics=("arbitrary", "arbitrary"),
                                             vmem_limit_bytes=RESIDENT_VMEM_LIMIT),
        name="ret_branch",
    )(*args)


def _merge_body(h_ref, ya_ref, yb_ref, wma_ref, wmb_ref, wa_ref, wb_ref, o_ref, *, mc):
    def chunk(i, carry):
        r0 = pl.multiple_of(i * mc, mc)
        rows = pl.ds(r0, mc)
        hr = h_ref[rows, :]
        ma = _sigmoid(_dot(hr, wma_ref[...])) * _dot(ya_ref[rows, :], wa_ref[0])
        mb = _sigmoid(_dot(hr, wmb_ref[...])) * _dot(yb_ref[rows, :], wb_ref[0])
        o_ref[rows, :] = (ma + mb).astype(BF16)
        return carry

    lax.fori_loop(0, h_ref.shape[0] // mc, chunk, 0, unroll=CHUNK_UNROLL)


def _merge(h2, ya2, yb2, w_in, w_branch, tm, mc):
    n, d = h2.shape
    nb = 512
    nblk = d // nb
    ma_off = (w_in.shape[1] - 2 * d) // nb
    mb_off = ma_off + nblk
    act = pl.BlockSpec((tm, d), lambda i, j: (i, 0))
    in_specs = [act, act, act,
                pl.BlockSpec((d, nb), lambda i, j: (0, ma_off + j)),
                pl.BlockSpec((d, nb), lambda i, j: (0, mb_off + j)),
                pl.BlockSpec((1, d, nb), lambda i, j: (0, 0, j)),
                pl.BlockSpec((1, d, nb), lambda i, j: (1, 0, j))]
    args = [h2, ya2, yb2, w_in, w_in, w_branch, w_branch]
    return pl.pallas_call(
        functools.partial(_merge_body, mc=mc),
        grid=(n // tm, nblk),
        in_specs=in_specs,
        out_specs=[pl.BlockSpec((tm, nb), lambda i, j: (i, j))],
        out_shape=[jax.ShapeDtypeStruct((n, d), BF16)],
        compiler_params=_params(("arbitrary", "arbitrary")),
        name="merge",
    )(*args)


def _merge_rows_body(*refs, mc, cast_blocks):
    it = iter(refs)
    h_ref, ya_ref, yb_ref, wma_ref, wmb_ref, wa_ref, wb_ref = (next(it) for _ in range(7))
    wf_ref = next(it) if cast_blocks else None
    o_ref = next(it)
    wr_ref = next(it) if cast_blocks else None
    if cast_blocks:
        _side_cast(wf_ref, wr_ref, cast_blocks)
    tm, d = h_ref.shape
    cw = COL_BLOCK
    for sub in range(tm // mc):
        rows = slice(sub * mc, (sub + 1) * mc)
        hr, yar, ybr = h_ref[rows, :], ya_ref[rows, :], yb_ref[rows, :]
        for gi in range(d // cw):
            cols = slice(gi * cw, (gi + 1) * cw)
            ma = _sigmoid(_dot(hr, wma_ref[:, cols])) * _dot(yar, wa_ref[0, :, cols])
            mb = _sigmoid(_dot(hr, wmb_ref[:, cols])) * _dot(ybr, wb_ref[0, :, cols])
            o_ref[rows, cols] = (ma + mb).astype(BF16)


def _merge_rows(h2, ya2, yb2, w_in, w_branch, tm, mc, cast_f32=None):
    n, d = h2.shape
    ma_blk = w_in.shape[1] // d - 2
    act = pl.BlockSpec((tm, d), lambda i, j: (i, 0))
    resident = pl.Buffered(1)
    in_specs = [act, act, act,
                pl.BlockSpec((d, d), lambda i, j: (0, ma_blk), pipeline_mode=resident),
                pl.BlockSpec((d, d), lambda i, j: (0, ma_blk + 1), pipeline_mode=resident),
                pl.BlockSpec((1, d, d), lambda i, j: (0, 0, 0), pipeline_mode=resident),
                pl.BlockSpec((1, d, d), lambda i, j: (1, 0, 0), pipeline_mode=resident)]
    args = [h2, ya2, yb2, w_in, w_in, w_branch, w_branch]
    out_specs = [act]
    out_shape = [jax.ShapeDtypeStruct((n, d), BF16)]
    cast_blocks = 0
    if cast_f32 is not None:
        c_in, c_out, c_shape, cast_blocks = _cast_specs(cast_f32, 0, cast_f32.shape[2], n // tm,
                                                        lambda i, j: i)
        in_specs.append(c_in)
        args.append(cast_f32)
        out_specs.append(c_out)
        out_shape.append(c_shape)
    return pl.pallas_call(
        functools.partial(_merge_rows_body, mc=mc, cast_blocks=cast_blocks),
        grid=(n // tm, 1),
        in_specs=in_specs,
        out_specs=out_specs,
        out_shape=out_shape,
        compiler_params=pltpu.CompilerParams(dimension_semantics=("arbitrary", "arbitrary"),
                                             vmem_limit_bytes=RESIDENT_VMEM_LIMIT),
        name="merge_rows",
    )(*args)


def _out_body(m_ref, w_ref, x_ref, mod_ref, g_ref, y_ref, obuf, *, sb, ts, mc):
    g = g_ref[...]

    def finish(o, x, gate):
        ms = jnp.mean(o * o, axis=-1, keepdims=True)
        return x + gate * (o * lax.rsqrt(ms + EPS) * g)

    if mc <= ts:
        d = w_ref.shape[1]
        cw = COL_BLOCK
        for s in range(sb):
            gain = mod_ref[s, 2:3, :] * g
            for c in range(ts // mc):
                rows = slice(c * mc, (c + 1) * mc)
                mr = m_ref[s, rows, :]
                ss = jnp.zeros((mc, 1), F32)
                for gi in range(d // cw):
                    cols = slice(gi * cw, (gi + 1) * cw)
                    o = _dot(mr, w_ref[:, cols])
                    obuf[rows, cols] = o
                    ss = ss + jnp.sum(o * o, axis=-1, keepdims=True)
                rs = lax.rsqrt(ss * (1.0 / d) + EPS)
                for gi in range(d // cw):
                    cols = slice(gi * cw, (gi + 1) * cw)
                    y_ref[s, rows, cols] = x_ref[s, rows, cols] + gain[:, cols] * (obuf[rows, cols] * rs)
    else:
        obuf[...] = _dot(m_ref[...].reshape(sb * ts, m_ref.shape[2]), w_ref[...])
        for s in range(sb):
            y_ref[s] = finish(obuf[s * ts:(s + 1) * ts, :], x_ref[s], mod_ref[s, 2:3, :])


def _out_proj(merged, w_out, x, mod3, g_post, sb, tr, mc):
    b, t, d = x.shape
    obuf_rows = sb * tr
    act = lambda i, j: (i, j, 0)
    return pl.pallas_call(
        functools.partial(_out_body, sb=sb, ts=tr, mc=mc),
        grid=(b // sb, t // tr),
        in_specs=[pl.BlockSpec((sb, tr, d), act),
                  pl.BlockSpec((d, d), lambda i, j: (0, 0)),
                  pl.BlockSpec((sb, tr, d), act),
                  pl.BlockSpec((sb, 3, d), lambda i, j: (i, 0, 0)),
                  pl.BlockSpec((1, d), lambda i, j: (0, 0))],
        out_specs=pl.BlockSpec((sb, tr, d), act),
        out_shape=jax.ShapeDtypeStruct((b, t, d), F32),
        scratch_shapes=[pltpu.VMEM((obuf_rows, d), F32)],
        compiler_params=pltpu.CompilerParams(dimension_semantics=("arbitrary", "arbitrary"),
                                             vmem_limit_bytes=RESIDENT_VMEM_LIMIT),
        name="out_proj",
    )(merged, w_out, x, mod3, g_post.reshape(1, d))


class _LayerWeights:
    def __init__(self, w_in, w_branch, w_out, conv):
        self.w_in, self.w_branch, self.w_out, self.conv = w_in, w_branch, w_out, conv
        self.rest = self.branch = self.out = None


def _mixer_layer(x, mod3, hist, state, pos0, g_pre, g_post, conv_w, conv_b, w):
    b, t, d = x.shape
    long_seq = t >= ROW_CHUNK
    sb = 1 if long_seq else b
    tr = 512 if long_seq else t
    mc = ROW_CHUNK if long_seq else b * t
    conv_cols = w.conv.shape[1]
    if long_seq:
        outs = _conv_rows(x, mod3, g_pre, w.conv, conv_w, conv_b, hist, tr, ROW_CHUNK,
                          cast_f32=w.w_in[None] if w.rest is None else None, cast_col0=conv_cols)
        h, ya, new_hist = outs[:3]
        if w.rest is None:
            w.rest = outs[3][0]
    else:
        if w.rest is None:
            w.rest = w.w_in[:, conv_cols:].astype(BF16)
        h, ya, new_hist = _conv_seqs(x, mod3, g_pre, w.conv, conv_w, conv_b, hist, sb)
    outs = _ret_branch(h, w.rest, state, pos0, sb, mc,
                       cast_f32=w.w_branch if long_seq and w.branch is None else None)
    yb, new_state = outs[:2]
    if w.branch is None:
        w.branch = outs[2] if long_seq else w.w_branch.astype(BF16)
    n = b * t
    acts = (h.reshape(n, d), ya.reshape(n, d), yb.reshape(n, d))
    if long_seq:
        outs = _merge_rows(*acts, w.rest, w.branch, tr, ROW_CHUNK,
                           cast_f32=w.w_out[None] if w.out is None else None)
    else:
        outs = _merge(*acts, w.rest, w.branch, n, ROW_CHUNK)
    if w.out is None:
        w.out = outs[1][0] if long_seq else w.w_out.astype(BF16)
    y = _out_proj(outs[0].reshape(b, t, d), w.out, x, mod3, g_post, sb, 2 * tr if long_seq else tr, mc)
    return y, new_hist, new_state


def kernel(x_prompt, x_sample, c_prompt, c_sample, state_conv, state_ret, ada_w, ada_b, norm_pre,
           norm_post, w_in, conv_w, conv_b, w_branch, w_out):
    depth = w_in.shape[0]
    d = x_prompt.shape[2]
    hp, hs = x_prompt, x_sample
    conv_p, ret_p, conv_s, ret_s = [], [], [], []
    for l in range(depth):
        mod_p, mod_s, w_conv = _modulation(c_prompt, c_sample, ada_w[l], ada_b[l], w_in[l], 4 * d)
        shared = (norm_pre[l], norm_post[l], conv_w[l], conv_b[l],
                  _LayerWeights(w_in[l], w_branch[l], w_out[l], w_conv))
        hp, cp, rp = _mixer_layer(hp, mod_p.reshape(-1, 3, d), None, None, 0, *shared)
        hs, cs, rs = _mixer_layer(hs, mod_s.reshape(-1, 3, d), state_conv[l], state_ret[l], PAST_LEN,
                                  *shared)
        conv_p.append(cp)
        ret_p.append(rp)
        conv_s.append(cs)
        ret_s.append(rs)
    return (hp, hs, jnp.stack(conv_p), jnp.stack(ret_p), jnp.stack(conv_s), jnp.stack(ret_s))
```

```python
import functools

import jax
import jax.numpy as jnp
import numpy as np
from jax import lax
from jax.experimental import pallas as pl
from jax.experimental.pallas import tpu as pltpu

F32 = jnp.float32
BF16 = jnp.bfloat16

EPS = 1e-6
ROPE_BASE = 10000.0
REF_CHUNK = 64
N_HEADS = 8
CONV_WIDTH = 3

LANES = 128
SUBLANES = 8
COL_BLOCK = 256
ROW_CHUNK = 256
RET_TILE = 256
CHUNK_UNROLL = True
PAST_LEN = 1024
VMEM_LIMIT = 56 * 1024 * 1024
RESIDENT_VMEM_LIMIT = 60 * 1024 * 1024


def _dot(a, b):
    return jnp.dot(a, b, preferred_element_type=F32)


def _sigmoid(x):
    return 1.0 / (1.0 + jnp.exp(-x))


def _silu(x):
    return x * _sigmoid(x)


def _params(semantics):
    return pltpu.CompilerParams(dimension_semantics=semantics, vmem_limit_bytes=VMEM_LIMIT)


def _cast_specs(w_f32, col0, cols, n_steps, step_of):
    m, d, _ = w_f32.shape
    per = cols // COL_BLOCK
    assert cols % COL_BLOCK == 0 and col0 % COL_BLOCK == 0 and m * per <= n_steps

    def block(*ids):
        return jnp.minimum(step_of(*ids), m * per - 1)

    in_spec = pl.BlockSpec((1, d, COL_BLOCK),
                           lambda *ids: (block(*ids) // per, 0, col0 // COL_BLOCK + block(*ids) % per))
    out_spec = pl.BlockSpec((1, d, COL_BLOCK), lambda *ids: (block(*ids) // per, 0, block(*ids) % per))
    return in_spec, out_spec, jax.ShapeDtypeStruct((m, d, cols), BF16), m * per


def _side_cast(wf_ref, wr_ref, n_blocks):
    step = pl.program_id(0) * pl.num_programs(1) + pl.program_id(1)

    @pl.when(step < n_blocks)
    def _():
        wr_ref[...] = wf_ref[...].astype(BF16)


def _mod_body(cp_ref, cs_ref, w_ref, b_ref, wf_ref, op_ref, os_ref, wr_ref):
    wr_ref[...] = wf_ref[...].astype(BF16)
    w = w_ref[...].astype(BF16)
    op_ref[...] = _dot(_silu(cp_ref[...]).astype(BF16), w) + b_ref[...]
    os_ref[...] = _dot(_silu(cs_ref[...]).astype(BF16), w) + b_ref[...]


def _modulation(c_a, c_b, ada_w, ada_b, w_f32, cast_cols):
    d = c_a.shape[1]
    n_out = ada_w.shape[1]
    nb = 768
    steps = n_out // nb
    cb = cast_cols // steps
    assert cast_cols % steps == 0 and cb % LANES == 0
    cond = lambda c: pl.BlockSpec((c.shape[0], d), lambda j: (0, 0))
    mod = lambda c: pl.BlockSpec((c.shape[0], nb), lambda j: (0, j))
    return pl.pallas_call(
        _mod_body,
        grid=(steps,),
        in_specs=[cond(c_a), cond(c_b),
                  pl.BlockSpec((d, nb), lambda j: (0, j)),
                  pl.BlockSpec((1, nb), lambda j: (0, j)),
                  pl.BlockSpec((d, cb), lambda j: (0, j))],
        out_specs=[mod(c_a), mod(c_b), pl.BlockSpec((d, cb), lambda j: (0, j))],
        out_shape=[jax.ShapeDtypeStruct((c_a.shape[0], n_out), F32),
                   jax.ShapeDtypeStruct((c_b.shape[0], n_out), F32),
                   jax.ShapeDtypeStruct((d, cast_cols), BF16)],
        compiler_params=_params(("arbitrary",)),
        name="modulation",
    )(c_a, c_b, ada_w, ada_b.reshape(1, n_out), w_f32)


def _prenorm(x, g_pre, scale1, shift):
    ms = jnp.mean(x * x, axis=-1, keepdims=True)
    return (x * lax.rsqrt(ms + EPS) * g_pre * scale1 + shift).astype(BF16)


def _conv_epilogue(ubuf, ec, taps, gb, gc, gu, zc):
    w0, w1, w2, cb = taps
    top = SUBLANES
    u = gc * gu
    ubuf[top:top + ec, :] = u
    f0 = ubuf[top - 2:top - 2 + ec, :]
    f1 = ubuf[top - 1:top - 1 + ec, :]
    conv = cb + w0 * f0 + w1 * f1 + w2 * u
    ubuf[top - 2:top, :] = ubuf[top + ec - 2:top + ec, :]
    return (gb * conv * _silu(zc)).astype(BF16)


def _conv_rows_body(*refs, mc, has_hist, cast_blocks):
    it = iter(refs)
    x_ref, mod_ref, g_ref, wb_ref, wc_ref, wu_ref, wz_ref, cw_ref, cb_ref = (next(it) for _ in range(9))
    hist_ref = next(it) if has_hist else None
    wf_ref = next(it) if cast_blocks else None
    h_ref, ya_ref, nh_ref = next(it), next(it), next(it)
    wr_ref = next(it) if cast_blocks else None
    ubuf = next(it)
    if cast_blocks:
        _side_cast(wf_ref, wr_ref, cast_blocks)
    tr, d = x_ref.shape[1], x_ref.shape[2]
    cw = ubuf.shape[2]
    ng = d // cw
    top = SUBLANES
    r = pl.program_id(1)

    @pl.when(r == 0)
    def _():
        for gi in range(ng):
            cols = slice(gi * cw, (gi + 1) * cw)
            ubuf[gi, top - 2:top, :] = hist_ref[0, :, cols] if has_hist else jnp.zeros((2, cw), F32)

    g_pre = g_ref[...]
    shift = mod_ref[0, 0:1, :]
    scale1 = 1.0 + mod_ref[0, 1:2, :]
    for sub in range(tr // mc):
        rows = slice(sub * mc, (sub + 1) * mc)
        h_ref[0, rows, :] = _prenorm(x_ref[0, rows, :], g_pre, scale1, shift)
        hr = h_ref[0, rows, :]
        for gi in range(ng):
            cols = slice(gi * cw, (gi + 1) * cw)
            taps = (cw_ref[0:1, cols], cw_ref[1:2, cols], cw_ref[2:3, cols], cb_ref[:, cols])
            ya_ref[0, rows, cols] = _conv_epilogue(
                ubuf.at[gi], mc, taps, gc=_dot(hr, wc_ref[:, cols]), gu=_dot(hr, wu_ref[:, cols]),
                zc=_dot(hr, wz_ref[:, cols]), gb=_dot(hr, wb_ref[:, cols]))

    @pl.when(r == pl.num_programs(1) - 1)
    def _():
        for gi in range(ng):
            nh_ref[0, :, gi * cw:(gi + 1) * cw] = ubuf[gi, top - 2:top, :]


def _conv_rows(x, mod3, g_pre, w_conv, conv_w, conv_b, hist, tr, mc, cast_f32=None, cast_col0=0):
    b, t, d = x.shape
    cw = COL_BLOCK
    nr = t // tr
    has_hist = hist is not None
    row_block = lambda i, r: (i, r, 0)
    per_seq = lambda i, r: (i, 0, 0)
    whole = lambda i, r: (0, 0)
    in_specs = [pl.BlockSpec((1, tr, d), row_block),
                pl.BlockSpec((1, 3, d), per_seq),
                pl.BlockSpec((1, d), whole)]
    in_specs += [pl.BlockSpec((d, d), functools.partial(lambda i, r, k: (0, k), k=k),
                              pipeline_mode=pl.Buffered(1)) for k in range(4)]
    in_specs += [pl.BlockSpec((CONV_WIDTH, d), whole), pl.BlockSpec((1, d), whole)]
    args = [x, mod3, g_pre.reshape(1, d), w_conv, w_conv, w_conv, w_conv, conv_w, conv_b.reshape(1, d)]
    if has_hist:
        in_specs.append(pl.BlockSpec((1, CONV_WIDTH - 1, d), per_seq))
        args.append(hist)
    out_specs = [pl.BlockSpec((1, tr, d), row_block),
                 pl.BlockSpec((1, tr, d), row_block),
                 pl.BlockSpec((1, CONV_WIDTH - 1, d), per_seq)]
    out_shape = [jax.ShapeDtypeStruct((b, t, d), BF16),
                 jax.ShapeDtypeStruct((b, t, d), BF16),
                 jax.ShapeDtypeStruct((b, CONV_WIDTH - 1, d), F32)]
    cast_blocks = 0
    if cast_f32 is not None:
        c_in, c_out, c_shape, cast_blocks = _cast_specs(
            cast_f32, cast_col0, cast_f32.shape[2] - cast_col0, b * nr, lambda i, r: i * nr + r)
        in_specs.append(c_in)
        args.append(cast_f32)
        out_specs.append(c_out)
        out_shape.append(c_shape)
    return pl.pallas_call(
        functools.partial(_conv_rows_body, mc=mc, has_hist=has_hist, cast_blocks=cast_blocks),
        grid=(b, nr),
        in_specs=in_specs,
        out_specs=out_specs,
        out_shape=out_shape,
        scratch_shapes=[pltpu.VMEM((d // cw, mc + SUBLANES, cw), F32)],
        compiler_params=pltpu.CompilerParams(dimension_semantics=("parallel", "arbitrary"),
                                             vmem_limit_bytes=RESIDENT_VMEM_LIMIT),
        name="conv_rows",
    )(*args)


def _conv_seqs_body(*refs, sb, ts, has_hist):
    if has_hist:
        (x_ref, mod_ref, g_ref, wb_ref, wc_ref, wu_ref, wz_ref, cw_ref, cb_ref, hist_ref,
         h_ref, ya_ref, nh_ref, ubuf, pbuf) = refs
    else:
        (x_ref, mod_ref, g_ref, wb_ref, wc_ref, wu_ref, wz_ref, cw_ref, cb_ref,
         h_ref, ya_ref, nh_ref, ubuf, pbuf) = refs

    @pl.when(pl.program_id(1) == 0)
    def _():
        h_ref[...] = _prenorm(x_ref[...], g_ref[...], 1.0 + mod_ref[:, 1:2, :], mod_ref[:, 0:1, :])

    taps = (cw_ref[0:1, :], cw_ref[1:2, :], cw_ref[2:3, :], cb_ref[...])
    top = SUBLANES
    hr = h_ref[...].reshape(sb * ts, h_ref.shape[2])
    for k, w_ref in enumerate((wb_ref, wc_ref, wu_ref, wz_ref)):
        pbuf[k] = _dot(hr, w_ref[...])
    for s in range(sb):
        ubuf[top - 2:top, :] = hist_ref[s] if has_hist else jnp.zeros((2, ubuf.shape[1]), F32)
        rows = slice(s * ts, (s + 1) * ts)
        ya_ref[s] = _conv_epilogue(ubuf, ts, taps, pbuf[0, rows, :], pbuf[1, rows, :],
                                   pbuf[2, rows, :], pbuf[3, rows, :])
        nh_ref[s] = ubuf[top - 2:top, :]


def _conv_seqs(x, mod3, g_pre, w_in, conv_w, conv_b, hist, sb):
    b, t, d = x.shape
    cw = COL_BLOCK
    ng = d // cw
    has_hist = hist is not None
    seq_block = lambda i, g: (i, 0, 0)
    in_specs = [pl.BlockSpec((sb, t, d), seq_block),
                pl.BlockSpec((sb, 3, d), seq_block),
                pl.BlockSpec((1, d), lambda i, g: (0, 0))]
    in_specs += [pl.BlockSpec((d, cw), functools.partial(lambda i, g, k: (0, k * ng + g), k=k))
                 for k in range(4)]
    in_specs += [pl.BlockSpec((CONV_WIDTH, cw), lambda i, g: (0, g)),
                 pl.BlockSpec((1, cw), lambda i, g: (0, g))]
    args = [x, mod3, g_pre.reshape(1, d), w_in, w_in, w_in, w_in, conv_w, conv_b.reshape(1, d)]
    if has_hist:
        in_specs.append(pl.BlockSpec((sb, CONV_WIDTH - 1, cw), lambda i, g: (i, 0, g)))
        args.append(hist)
    return pl.pallas_call(
        functools.partial(_conv_seqs_body, sb=sb, ts=t, has_hist=has_hist),
        grid=(b // sb, ng),
        in_specs=in_specs,
        out_specs=[pl.BlockSpec((sb, t, d), seq_block),
                   pl.BlockSpec((sb, t, cw), lambda i, g: (i, 0, g)),
                   pl.BlockSpec((sb, CONV_WIDTH - 1, cw), lambda i, g: (i, 0, g))],
        out_shape=[jax.ShapeDtypeStruct((b, t, d), BF16),
                   jax.ShapeDtypeStruct((b, t, d), BF16),
                   jax.ShapeDtypeStruct((b, CONV_WIDTH - 1, d), F32)],
        scratch_shapes=[pltpu.VMEM((t + SUBLANES, cw), F32),
                        pltpu.VMEM((4, sb * t, cw), F32)],
        compiler_params=_params(("arbitrary", "arbitrary")),
        name="conv_seqs",
    )(*args)


def _ret_body(*refs, sb, ts, mc, tt, has_state, cast_blocks):
    it = iter(refs)
    (h_ref, wq_ref, wk_ref, wv_ref, wz_ref, rq_ref, rk_ref, mask_ref, qd_ref, kd_ref,
     bd_ref) = (next(it) for _ in range(11))
    st_ref = next(it) if has_state else None
    wf_ref = next(it) if cast_blocks else None
    yb_ref, ns_ref = next(it), next(it)
    wr_ref = next(it) if cast_blocks else None
    pbuf = next(it)
    if cast_blocks:
        _side_cast(wf_ref, wr_ref, cast_blocks)
    dk = LANES
    dv = 2 * LANES
    hpb = 2

    def tile(s, r0, q2, k2, v2, z2, cq, sq, ck, sk):
        for hh in range(hpb):
            q = q2[:, hh * dk:(hh + 1) * dk]
            k = k2[:, hh * dk:(hh + 1) * dk]
            v = v2[:, hh * dv:(hh + 1) * dv].astype(BF16)
            z = z2[:, hh * dv:(hh + 1) * dv]
            qr = q * cq + pltpu.roll(q, dk // 2, axis=1) * sq
            kr = k * ck + pltpu.roll(k, dk // 2, axis=1) * sk
            state = ns_ref[s, hh]
            scores = lax.dot_general(qr.astype(BF16), kr.astype(BF16),
                                     (((1,), (1,)), ((), ())), preferred_element_type=F32)
            p = (scores * mask_ref[hh]).astype(BF16)
            o = _dot(p, v) + _dot((qr * qd_ref[hh]).astype(BF16), state.astype(BF16))
            kv = lax.dot_general((kr * kd_ref[hh]).astype(BF16), v,
                                 (((0,), (0,)), ((), ())), preferred_element_type=F32)
            ns_ref[s, hh] = bd_ref[hh] * state + kv
            on = o * lax.rsqrt(jnp.mean(o * o, axis=-1, keepdims=True) + EPS)
            yb_ref[s, pl.ds(r0, tt), hh * dv:(hh + 1) * dv] = (on * _silu(z)).astype(BF16)

    def projections(hr):
        return (_dot(hr, wq_ref[...]), _dot(hr, wk_ref[...]),
                _dot(hr, wv_ref[...]), _dot(hr, wz_ref[...]))

    if has_state:
        ns_ref[...] = st_ref[...]
    else:
        ns_ref[...] = jnp.zeros(ns_ref.shape, F32)

    if mc <= ts:
        assert mc == tt
        for s in range(sb):
            def chunk(i, carry, s=s):
                r0 = pl.multiple_of(i * mc, mc)
                q2, k2, v2, z2 = projections(h_ref[s, pl.ds(r0, mc), :])
                rows = pl.ds(r0, mc)
                tile(s, r0, q2, k2, v2, z2, rq_ref[0, rows, :], rq_ref[1, rows, :],
                     rk_ref[0, rows, :], rk_ref[1, rows, :])
                return carry

            lax.fori_loop(0, ts // mc, chunk, 0, unroll=CHUNK_UNROLL)
    else:
        assert tt == ts
        hr = h_ref[...].reshape(sb * ts, h_ref.shape[2])
        for k, p in enumerate(projections(hr)):
            pbuf[k, :, 0:p.shape[1]] = p
        for s in range(sb):
            rows = slice(s * ts, (s + 1) * ts)
            tile(s, 0, pbuf[0, rows, 0:hpb * dk], pbuf[1, rows, 0:hpb * dk],
                 pbuf[2, rows, :], pbuf[3, rows, :],
                 rq_ref[0], rq_ref[1], rk_ref[0], rk_ref[1])


def _retention_tables(t, tt, pos0, chunk):
    dk = LANES
    f32 = np.float32
    inv = f32(ROPE_BASE) ** (-np.arange(0, dk, 2, dtype=f32) / f32(dk))
    ang = (pos0 + np.arange(t)).astype(f32)[:, None] * inv[None, :]
    cos, sin = np.cos(ang), np.sin(ang)
    rope_q = np.stack([np.concatenate([cos, cos], -1), np.concatenate([-sin, sin], -1)])
    rope_k = rope_q * f32(dk ** -0.5)
    lg = np.log(f32(1.0) - f32(2.0) ** (f32(-5.0) - np.arange(N_HEADS, dtype=f32)))
    idx = np.arange(tt, dtype=f32)
    visible = (np.arange(tt)[None, :] // chunk) <= (np.arange(tt)[:, None] // chunk)
    mask = np.exp(lg[:, None, None] * np.abs(idx[:, None] - idx[None, :])) * visible.astype(f32)
    q_dec = np.broadcast_to(np.exp(lg[:, None] * (idx + f32(1.0)))[..., None], (N_HEADS, tt, dk))
    k_dec = np.broadcast_to(np.exp(lg[:, None] * (f32(tt - 1.0) - idx))[..., None], (N_HEADS, tt, dk))
    b_dec = np.broadcast_to(np.exp(lg * f32(tt))[:, None, None], (N_HEADS, 1, 2 * dk))
    return tuple(jnp.asarray(a, dtype=F32) for a in (rope_q, rope_k, mask, q_dec, k_dec, b_dec))


def _ret_branch(h, w_in, state, pos0, sb, mc, cast_f32=None):
    b, t, d = h.shape
    dk, dv, hpb = LANES, 2 * LANES, 2
    nhp = N_HEADS // hpb
    tt = min(RET_TILE, t)
    chunk = min(REF_CHUNK, t)
    has_state = state is not None
    rope_q, rope_k, mask, q_dec, k_dec, b_dec = _retention_tables(t, tt, pos0, chunk)
    q_off = 0
    k_off = q_off + nhp
    v_off = 2 * N_HEADS * dk // (hpb * dv)
    z_off = v_off + nhp
    in_specs = [pl.BlockSpec((sb, t, d), lambda i, g: (i, 0, 0)),
                pl.BlockSpec((d, hpb * dk), lambda i, g: (0, q_off + g)),
                pl.BlockSpec((d, hpb * dk), lambda i, g: (0, k_off + g)),
                pl.BlockSpec((d, hpb * dv), lambda i, g: (0, v_off + g)),
                pl.BlockSpec((d, hpb * dv), lambda i, g: (0, z_off + g)),
                pl.BlockSpec((2, t, dk), lambda i, g: (0, 0, 0)),
                pl.BlockSpec((2, t, dk), lambda i, g: (0, 0, 0)),
                pl.BlockSpec((hpb, tt, tt), lambda i, g: (g, 0, 0)),
                pl.BlockSpec((hpb, tt, dk), lambda i, g: (g, 0, 0)),
                pl.BlockSpec((hpb, tt, dk), lambda i, g: (g, 0, 0)),
                pl.BlockSpec((hpb, 1, dv), lambda i, g: (g, 0, 0))]
    args = [h, w_in, w_in, w_in, w_in, rope_q, rope_k, mask, q_dec, k_dec, b_dec]
    if has_state:
        in_specs.append(pl.BlockSpec((sb, hpb, dk, dv), lambda i, g: (i, g, 0, 0)))
        args.append(state)
    out_specs = [pl.BlockSpec((sb, t, hpb * dv), lambda i, g: (i, 0, g)),
                 pl.BlockSpec((sb, hpb, dk, dv), lambda i, g: (i, g, 0, 0))]
    out_shape = [jax.ShapeDtypeStruct((b, t, N_HEADS * dv), BF16),
                 jax.ShapeDtypeStruct((b, N_HEADS, dk, dv), F32)]
    cast_blocks = 0
    if cast_f32 is not None:
        c_in, c_out, c_shape, cast_blocks = _cast_specs(cast_f32, 0, cast_f32.shape[2], (b // sb) * nhp,
                                                        lambda i, g: i * nhp + g)
        in_specs.append(c_in)
        args.append(cast_f32)
        out_specs.append(c_out)
        out_shape.append(c_shape)
    pbuf_rows = sb * t if mc > t else SUBLANES
    return pl.pallas_call(
        functools.partial(_ret_body, sb=sb, ts=t, mc=mc, tt=tt, has_state=has_state,
                          cast_blocks=cast_blocks),
        grid=(b // sb, nhp),
        in_specs=in_specs,
        out_specs=out_specs,
        out_shape=out_shape,
        scratch_shapes=[pltpu.VMEM((4, pbuf_rows, hpb * dv), F32)],
        compiler_params=pltpu.CompilerParams(dimension_semantics=("parallel", "parallel"),
                                             vmem_limit_bytes=RESIDENT_VMEM_LIMIT),
        name="ret_branch",
    )(*args)


def _merge_body(h_ref, ya_ref, yb_ref, wma_ref, wmb_ref, wa_ref, wb_ref, o_ref, *, mc):
    def chunk(i, carry):
        r0 = pl.multiple_of(i * mc, mc)
        rows = pl.ds(r0, mc)
        hr = h_ref[rows, :]
        ma = _sigmoid(_dot(hr, wma_ref[...])) * _dot(ya_ref[rows, :], wa_ref[0])
        mb = _sigmoid(_dot(hr, wmb_ref[...])) * _dot(yb_ref[rows, :], wb_ref[0])
        o_ref[rows, :] = (ma + mb).astype(BF16)
        return carry

    lax.fori_loop(0, h_ref.shape[0] // mc, chunk, 0, unroll=CHUNK_UNROLL)


def _merge(h2, ya2, yb2, w_in, w_branch, tm, mc):
    n, d = h2.shape
    nb = 512
    nblk = d // nb
    ma_off = (w_in.shape[1] - 2 * d) // nb
    mb_off = ma_off + nblk
    act = pl.BlockSpec((tm, d), lambda i, j: (i, 0))
    in_specs = [act, act, act,
                pl.BlockSpec((d, nb), lambda i, j: (0, ma_off + j)),
                pl.BlockSpec((d, nb), lambda i, j: (0, mb_off + j)),
                pl.BlockSpec((1, d, nb), lambda i, j: (0, 0, j)),
                pl.BlockSpec((1, d, nb), lambda i, j: (1, 0, j))]
    args = [h2, ya2, yb2, w_in, w_in, w_branch, w_branch]
    return pl.pallas_call(
        functools.partial(_merge_body, mc=mc),
        grid=(n // tm, nblk),
        in_specs=in_specs,
        out_specs=[pl.BlockSpec((tm, nb), lambda i, j: (i, j))],
        out_shape=[jax.ShapeDtypeStruct((n, d), BF16)],
        compiler_params=_params(("arbitrary", "arbitrary")),
        name="merge",
    )(*args)


def _merge_rows_body(*refs, mc, cast_blocks):
    it = iter(refs)
    h_ref, ya_ref, yb_ref, wma_ref, wmb_ref, wa_ref, wb_ref = (next(it) for _ in range(7))
    wf_ref = next(it) if cast_blocks else None
    o_ref = next(it)
    wr_ref = next(it) if cast_blocks else None
    if cast_blocks:
        _side_cast(wf_ref, wr_ref, cast_blocks)
    tm, d = h_ref.shape
    cw = COL_BLOCK
    for sub in range(tm // mc):
        rows = slice(sub * mc, (sub + 1) * mc)
        hr, yar, ybr = h_ref[rows, :], ya_ref[rows, :], yb_ref[rows, :]
        for gi in range(d // cw):
            cols = slice(gi * cw, (gi + 1) * cw)
            ma = _sigmoid(_dot(hr, wma_ref[:, cols])) * _dot(yar, wa_ref[0, :, cols])
            mb = _sigmoid(_dot(hr, wmb_ref[:, cols])) * _dot(ybr, wb_ref[0, :, cols])
            o_ref[rows, cols] = (ma + mb).astype(BF16)


def _merge_rows(h2, ya2, yb2, w_in, w_branch, tm, mc, cast_f32=None):
    n, d = h2.shape
    ma_blk = w_in.shape[1] // d - 2
    act = pl.BlockSpec((tm, d), lambda i, j: (i, 0))
    resident = pl.Buffered(1)
    in_specs = [act, act, act,
                pl.BlockSpec((d, d), lambda i, j: (0, ma_blk), pipeline_mode=resident),
                pl.BlockSpec((d, d), lambda i, j: (0, ma_blk + 1), pipeline_mode=resident),
                pl.BlockSpec((1, d, d), lambda i, j: (0, 0, 0), pipeline_mode=resident),
                pl.BlockSpec((1, d, d), lambda i, j: (1, 0, 0), pipeline_mode=resident)]
    args = [h2, ya2, yb2, w_in, w_in, w_branch, w_branch]
    out_specs = [act]
    out_shape = [jax.ShapeDtypeStruct((n, d), BF16)]
    cast_blocks = 0
    if cast_f32 is not None:
        c_in, c_out, c_shape, cast_blocks = _cast_specs(cast_f32, 0, cast_f32.shape[2], n // tm,
                                                        lambda i, j: i)
        in_specs.append(c_in)
        args.append(cast_f32)
        out_specs.append(c_out)
        out_shape.append(c_shape)
    return pl.pallas_call(
        functools.partial(_merge_rows_body, mc=mc, cast_blocks=cast_blocks),
        grid=(n // tm, 1),
        in_specs=in_specs,
        out_specs=out_specs,
        out_shape=out_shape,
        compiler_params=pltpu.CompilerParams(dimension_semantics=("parallel", "parallel"),
                                             vmem_limit_bytes=RESIDENT_VMEM_LIMIT),
        name="merge_rows",
    )(*args)


def _out_body(m_ref, w_ref, x_ref, mod_ref, g_ref, y_ref, obuf, *, sb, ts, mc):
    g = g_ref[...]

    def finish(o, x, gate):
        ms = jnp.mean(o * o, axis=-1, keepdims=True)
        return x + gate * (o * lax.rsqrt(ms + EPS) * g)

    if mc <= ts:
        d = w_ref.shape[1]
        cw = COL_BLOCK
        for s in range(sb):
            gain = mod_ref[s, 2:3, :] * g
            for c in range(ts // mc):
                rows = slice(c * mc, (c + 1) * mc)
                mr = m_ref[s, rows, :]
                ss = jnp.zeros((mc, 1), F32)
                for gi in range(d // cw):
                    cols = slice(gi * cw, (gi + 1) * cw)
                    o = _dot(mr, w_ref[:, cols])
                    obuf[rows, cols] = o
                    ss = ss + jnp.sum(o * o, axis=-1, keepdims=True)
                rs = lax.rsqrt(ss * (1.0 / d) + EPS)
                for gi in range(d // cw):
                    cols = slice(gi * cw, (gi + 1) * cw)
                    y_ref[s, rows, cols] = x_ref[s, rows, cols] + gain[:, cols] * (obuf[rows, cols] * rs)
    else:
        obuf[...] = _dot(m_ref[...].reshape(sb * ts, m_ref.shape[2]), w_ref[...])
        for s in range(sb):
            y_ref[s] = finish(obuf[s * ts:(s + 1) * ts, :], x_ref[s], mod_ref[s, 2:3, :])


def _out_proj(merged, w_out, x, mod3, g_post, sb, tr, mc):
    b, t, d = x.shape
    obuf_rows = sb * tr
    act = lambda i, j: (i, j, 0)
    return pl.pallas_call(
        functools.partial(_out_body, sb=sb, ts=tr, mc=mc),
        grid=(b // sb, t // tr),
        in_specs=[pl.BlockSpec((sb, tr, d), act),
                  pl.BlockSpec((d, d), lambda i, j: (0, 0)),
                  pl.BlockSpec((sb, tr, d), act),
                  pl.BlockSpec((sb, 3, d), lambda i, j: (i, 0, 0)),
                  pl.BlockSpec((1, d), lambda i, j: (0, 0))],
        out_specs=pl.BlockSpec((sb, tr, d), act),
        out_shape=jax.ShapeDtypeStruct((b, t, d), F32),
        scratch_shapes=[pltpu.VMEM((obuf_rows, d), F32)],
        compiler_params=pltpu.CompilerParams(dimension_semantics=("parallel", "parallel"),
                                             vmem_limit_bytes=RESIDENT_VMEM_LIMIT),
        name="out_proj",
    )(merged, w_out, x, mod3, g_post.reshape(1, d))


class _LayerWeights:
    def __init__(self, w_in, w_branch, w_out, conv):
        self.w_in, self.w_branch, self.w_out, self.conv = w_in, w_branch, w_out, conv
        self.rest = self.branch = self.out = None


def _mixer_layer(x, mod3, hist, state, pos0, g_pre, g_post, conv_w, conv_b, w):
    b, t, d = x.shape
    long_seq = t >= ROW_CHUNK
    sb = 1 if long_seq else b
    tr = 512 if long_seq else t
    mc = ROW_CHUNK if long_seq else b * t
    conv_cols = w.conv.shape[1]
    if long_seq:
        outs = _conv_rows(x, mod3, g_pre, w.conv, conv_w, conv_b, hist, tr, ROW_CHUNK,
                          cast_f32=w.w_in[None] if w.rest is None else None, cast_col0=conv_cols)
        h, ya, new_hist = outs[:3]
        if w.rest is None:
            w.rest = outs[3][0]
    else:
        if w.rest is None:
            w.rest = w.w_in[:, conv_cols:].astype(BF16)
        h, ya, new_hist = _conv_seqs(x, mod3, g_pre, w.conv, conv_w, conv_b, hist, sb)
    outs = _ret_branch(h, w.rest, state, pos0, sb, mc,
                       cast_f32=w.w_branch if long_seq and w.branch is None else None)
    yb, new_state = outs[:2]
    if w.branch is None:
        w.branch = outs[2] if long_seq else w.w_branch.astype(BF16)
    n = b * t
    acts = (h.reshape(n, d), ya.reshape(n, d), yb.reshape(n, d))
    if long_seq:
        outs = _merge_rows(*acts, w.rest, w.branch, tr, ROW_CHUNK,
                           cast_f32=w.w_out[None] if w.out is None else None)
    else:
        outs = _merge(*acts, w.rest, w.branch, n, ROW_CHUNK)
    if w.out is None:
        w.out = outs[1][0] if long_seq else w.w_out.astype(BF16)
    y = _out_proj(outs[0].reshape(b, t, d), w.out, x, mod3, g_post, sb, 2 * tr if long_seq else tr, mc)
    return y, new_hist, new_state


def kernel(x_prompt, x_sample, c_prompt, c_sample, state_conv, state_ret, ada_w, ada_b, norm_pre,
           norm_post, w_in, conv_w, conv_b, w_branch, w_out):
    depth = w_in.shape[0]
    d = x_prompt.shape[2]
    hp, hs = x_prompt, x_sample
    conv_p, ret_p, conv_s, ret_s = [], [], [], []
    for l in range(depth):
        mod_p, mod_s, w_conv = _modulation(c_prompt, c_sample, ada_w[l], ada_b[l], w_in[l], 4 * d)
        shared = (norm_pre[l], norm_post[l], conv_w[l], conv_b[l],
                  _LayerWeights(w_in[l], w_branch[l], w_out[l], w_conv))
        hp, cp, rp = _mixer_layer(hp, mod_p.reshape(-1, 3, d), None, None, 0, *shared)
        hs, cs, rs = _mixer_layer(hs, mod_s.reshape(-1, 3, d), state_conv[l], state_ret[l], PAST_LEN,
                                  *shared)
        conv_p.append(cp)
        ret_p.append(rp)
        conv_s.append(cs)
        ret_s.append(rs)
    return (hp, hs, jnp.stack(conv_p), jnp.stack(ret_p), jnp.stack(conv_s), jnp.stack(ret_s))
```

```python
import functools

import jax
import jax.numpy as jnp
import numpy as np
from jax import lax
from jax.experimental import pallas as pl
from jax.experimental.pallas import tpu as pltpu

F32 = jnp.float32
BF16 = jnp.bfloat16

EPS = 1e-6
ROPE_BASE = 10000.0
REF_CHUNK = 64
N_HEADS = 8
CONV_WIDTH = 3

LANES = 128
SUBLANES = 8
COL_BLOCK = 256
ROW_CHUNK = 256
RET_TILE = 256
CHUNK_UNROLL = True
PAST_LEN = 1024
VMEM_LIMIT = 56 * 1024 * 1024
RESIDENT_VMEM_LIMIT = 62 * 1024 * 1024


def _dot(a, b):
    return jnp.dot(a, b, preferred_element_type=F32)


def _sigmoid(x):
    return 1.0 / (1.0 + jnp.exp(-x))


def _silu(x):
    return x * _sigmoid(x)


def _params(semantics):
    return pltpu.CompilerParams(dimension_semantics=semantics, vmem_limit_bytes=VMEM_LIMIT)


def _cast_specs(w_f32, col0, cols, n_steps, step_of):
    m, d, _ = w_f32.shape
    per = cols // COL_BLOCK
    assert cols % COL_BLOCK == 0 and col0 % COL_BLOCK == 0 and m * per <= n_steps

    def block(*ids):
        return jnp.minimum(step_of(*ids), m * per - 1)

    in_spec = pl.BlockSpec((1, d, COL_BLOCK),
                           lambda *ids: (block(*ids) // per, 0, col0 // COL_BLOCK + block(*ids) % per))
    out_spec = pl.BlockSpec((1, d, COL_BLOCK), lambda *ids: (block(*ids) // per, 0, block(*ids) % per))
    return in_spec, out_spec, jax.ShapeDtypeStruct((m, d, cols), BF16), m * per


def _side_cast(wf_ref, wr_ref, n_blocks):
    step = pl.program_id(0) * pl.num_programs(1) + pl.program_id(1)

    @pl.when(step < n_blocks)
    def _():
        wr_ref[...] = wf_ref[...].astype(BF16)


def _mod_body(cp_ref, cs_ref, w_ref, b_ref, wf_ref, op_ref, os_ref, wr_ref):
    wr_ref[...] = wf_ref[...].astype(BF16)
    w = w_ref[...].astype(BF16)
    op_ref[...] = _dot(_silu(cp_ref[...]).astype(BF16), w) + b_ref[...]
    os_ref[...] = _dot(_silu(cs_ref[...]).astype(BF16), w) + b_ref[...]


def _modulation(c_a, c_b, ada_w, ada_b, w_f32, cast_cols):
    d = c_a.shape[1]
    n_out = ada_w.shape[1]
    nb = 768
    steps = n_out // nb
    cb = cast_cols // steps
    assert cast_cols % steps == 0 and cb % LANES == 0
    cond = lambda c: pl.BlockSpec((c.shape[0], d), lambda j: (0, 0))
    mod = lambda c: pl.BlockSpec((c.shape[0], nb), lambda j: (0, j))
    return pl.pallas_call(
        _mod_body,
        grid=(steps,),
        in_specs=[cond(c_a), cond(c_b),
                  pl.BlockSpec((d, nb), lambda j: (0, j)),
                  pl.BlockSpec((1, nb), lambda j: (0, j)),
                  pl.BlockSpec((d, cb), lambda j: (0, j))],
        out_specs=[mod(c_a), mod(c_b), pl.BlockSpec((d, cb), lambda j: (0, j))],
        out_shape=[jax.ShapeDtypeStruct((c_a.shape[0], n_out), F32),
                   jax.ShapeDtypeStruct((c_b.shape[0], n_out), F32),
                   jax.ShapeDtypeStruct((d, cast_cols), BF16)],
        compiler_params=_params(("arbitrary",)),
        name="modulation",
    )(c_a, c_b, ada_w, ada_b.reshape(1, n_out), w_f32)


def _prenorm(x, g_pre, scale1, shift):
    ms = jnp.mean(x * x, axis=-1, keepdims=True)
    return (x * lax.rsqrt(ms + EPS) * g_pre * scale1 + shift).astype(BF16)


def _conv_epilogue(ubuf, ec, taps, gb, gc, gu, zc):
    w0, w1, w2, cb = taps
    top = SUBLANES
    u = gc * gu
    ubuf[top:top + ec, :] = u
    f0 = ubuf[top - 2:top - 2 + ec, :]
    f1 = ubuf[top - 1:top - 1 + ec, :]
    conv = cb + w0 * f0 + w1 * f1 + w2 * u
    ubuf[top - 2:top, :] = ubuf[top + ec - 2:top + ec, :]
    return (gb * conv * _silu(zc)).astype(BF16)


def _conv_rows_body(*refs, mc, has_hist, cast_blocks):
    it = iter(refs)
    x_ref, mod_ref, g_ref, wb_ref, wc_ref, wu_ref, wz_ref, cw_ref, cb_ref = (next(it) for _ in range(9))
    hist_ref = next(it) if has_hist else None
    wf_ref = next(it) if cast_blocks else None
    h_ref, ya_ref, nh_ref = next(it), next(it), next(it)
    wr_ref = next(it) if cast_blocks else None
    ubuf = next(it)
    if cast_blocks:
        _side_cast(wf_ref, wr_ref, cast_blocks)
    tr, d = x_ref.shape[1], x_ref.shape[2]
    cw = ubuf.shape[2]
    ng = d // cw
    top = SUBLANES
    r = pl.program_id(1)

    @pl.when(r == 0)
    def _():
        for gi in range(ng):
            cols = slice(gi * cw, (gi + 1) * cw)
            ubuf[gi, top - 2:top, :] = hist_ref[0, :, cols] if has_hist else jnp.zeros((2, cw), F32)

    g_pre = g_ref[...]
    shift = mod_ref[0, 0:1, :]
    scale1 = 1.0 + mod_ref[0, 1:2, :]
    for sub in range(tr // mc):
        rows = slice(sub * mc, (sub + 1) * mc)
        h_ref[0, rows, :] = _prenorm(x_ref[0, rows, :], g_pre, scale1, shift)
        hr = h_ref[0, rows, :]
        for gi in range(ng):
            cols = slice(gi * cw, (gi + 1) * cw)
            taps = (cw_ref[0:1, cols], cw_ref[1:2, cols], cw_ref[2:3, cols], cb_ref[:, cols])
            ya_ref[0, rows, cols] = _conv_epilogue(
                ubuf.at[gi], mc, taps, gc=_dot(hr, wc_ref[:, cols]), gu=_dot(hr, wu_ref[:, cols]),
                zc=_dot(hr, wz_ref[:, cols]), gb=_dot(hr, wb_ref[:, cols]))

    @pl.when(r == pl.num_programs(1) - 1)
    def _():
        for gi in range(ng):
            nh_ref[0, :, gi * cw:(gi + 1) * cw] = ubuf[gi, top - 2:top, :]


def _conv_rows(x, mod3, g_pre, w_conv, conv_w, conv_b, hist, tr, mc, cast_f32=None, cast_col0=0):
    b, t, d = x.shape
    cw = COL_BLOCK
    nr = t // tr
    has_hist = hist is not None
    row_block = lambda i, r: (i, r, 0)
    per_seq = lambda i, r: (i, 0, 0)
    whole = lambda i, r: (0, 0)
    in_specs = [pl.BlockSpec((1, tr, d), row_block),
                pl.BlockSpec((1, 3, d), per_seq),
                pl.BlockSpec((1, d), whole)]
    in_specs += [pl.BlockSpec((d, d), functools.partial(lambda i, r, k: (0, k), k=k),
                              pipeline_mode=pl.Buffered(1)) for k in range(4)]
    in_specs += [pl.BlockSpec((CONV_WIDTH, d), whole), pl.BlockSpec((1, d), whole)]
    args = [x, mod3, g_pre.reshape(1, d), w_conv, w_conv, w_conv, w_conv, conv_w, conv_b.reshape(1, d)]
    if has_hist:
        in_specs.append(pl.BlockSpec((1, CONV_WIDTH - 1, d), per_seq))
        args.append(hist)
    out_specs = [pl.BlockSpec((1, tr, d), row_block),
                 pl.BlockSpec((1, tr, d), row_block),
                 pl.BlockSpec((1, CONV_WIDTH - 1, d), per_seq)]
    out_shape = [jax.ShapeDtypeStruct((b, t, d), BF16),
                 jax.ShapeDtypeStruct((b, t, d), BF16),
                 jax.ShapeDtypeStruct((b, CONV_WIDTH - 1, d), F32)]
    cast_blocks = 0
    if cast_f32 is not None:
        c_in, c_out, c_shape, cast_blocks = _cast_specs(
            cast_f32, cast_col0, cast_f32.shape[2] - cast_col0, b * nr, lambda i, r: i * nr + r)
        in_specs.append(c_in)
        args.append(cast_f32)
        out_specs.append(c_out)
        out_shape.append(c_shape)
    return pl.pallas_call(
        functools.partial(_conv_rows_body, mc=mc, has_hist=has_hist, cast_blocks=cast_blocks),
        grid=(b, nr),
        in_specs=in_specs,
        out_specs=out_specs,
        out_shape=out_shape,
        scratch_shapes=[pltpu.VMEM((d // cw, mc + SUBLANES, cw), F32)],
        compiler_params=pltpu.CompilerParams(dimension_semantics=("arbitrary", "arbitrary"),
                                             vmem_limit_bytes=RESIDENT_VMEM_LIMIT),
        name="conv_rows",
    )(*args)


def _conv_seqs_body(*refs, sb, ts, has_hist):
    if has_hist:
        (x_ref, mod_ref, g_ref, wb_ref, wc_ref, wu_ref, wz_ref, cw_ref, cb_ref, hist_ref,
         h_ref, ya_ref, nh_ref, ubuf, pbuf) = refs
    else:
        (x_ref, mod_ref, g_ref, wb_ref, wc_ref, wu_ref, wz_ref, cw_ref, cb_ref,
         h_ref, ya_ref, nh_ref, ubuf, pbuf) = refs

    @pl.when(pl.program_id(1) == 0)
    def _():
        h_ref[...] = _prenorm(x_ref[...], g_ref[...], 1.0 + mod_ref[:, 1:2, :], mod_ref[:, 0:1, :])

    taps = (cw_ref[0:1, :], cw_ref[1:2, :], cw_ref[2:3, :], cb_ref[...])
    top = SUBLANES
    hr = h_ref[...].reshape(sb * ts, h_ref.shape[2])
    for k, w_ref in enumerate((wb_ref, wc_ref, wu_ref, wz_ref)):
        pbuf[k] = _dot(hr, w_ref[...])
    for s in range(sb):
        ubuf[top - 2:top, :] = hist_ref[s] if has_hist else jnp.zeros((2, ubuf.shape[1]), F32)
        rows = slice(s * ts, (s + 1) * ts)
        ya_ref[s] = _conv_epilogue(ubuf, ts, taps, pbuf[0, rows, :], pbuf[1, rows, :],
                                   pbuf[2, rows, :], pbuf[3, rows, :])
        nh_ref[s] = ubuf[top - 2:top, :]


def _conv_seqs(x, mod3, g_pre, w_in, conv_w, conv_b, hist, sb):
    b, t, d = x.shape
    cw = COL_BLOCK
    ng = d // cw
    has_hist = hist is not None
    seq_block = lambda i, g: (i, 0, 0)
    in_specs = [pl.BlockSpec((sb, t, d), seq_block),
                pl.BlockSpec((sb, 3, d), seq_block),
                pl.BlockSpec((1, d), lambda i, g: (0, 0))]
    in_specs += [pl.BlockSpec((d, cw), functools.partial(lambda i, g, k: (0, k * ng + g), k=k))
                 for k in range(4)]
    in_specs += [pl.BlockSpec((CONV_WIDTH, cw), lambda i, g: (0, g)),
                 pl.BlockSpec((1, cw), lambda i, g: (0, g))]
    args = [x, mod3, g_pre.reshape(1, d), w_in, w_in, w_in, w_in, conv_w, conv_b.reshape(1, d)]
    if has_hist:
        in_specs.append(pl.BlockSpec((sb, CONV_WIDTH - 1, cw), lambda i, g: (i, 0, g)))
        args.append(hist)
    return pl.pallas_call(
        functools.partial(_conv_seqs_body, sb=sb, ts=t, has_hist=has_hist),
        grid=(b // sb, ng),
        in_specs=in_specs,
        out_specs=[pl.BlockSpec((sb, t, d), seq_block),
                   pl.BlockSpec((sb, t, cw), lambda i, g: (i, 0, g)),
                   pl.BlockSpec((sb, CONV_WIDTH - 1, cw), lambda i, g: (i, 0, g))],
        out_shape=[jax.ShapeDtypeStruct((b, t, d), BF16),
                   jax.ShapeDtypeStruct((b, t, d), BF16),
                   jax.ShapeDtypeStruct((b, CONV_WIDTH - 1, d), F32)],
        scratch_shapes=[pltpu.VMEM((t + SUBLANES, cw), F32),
                        pltpu.VMEM((4, sb * t, cw), F32)],
        compiler_params=_params(("arbitrary", "arbitrary")),
        name="conv_seqs",
    )(*args)


def _ret_body(*refs, sb, ts, mc, tt, has_state, cast_blocks):
    it = iter(refs)
    (h_ref, wq_ref, wk_ref, wv_ref, wz_ref, rq_ref, rk_ref, mask_ref, qd_ref, kd_ref,
     bd_ref) = (next(it) for _ in range(11))
    st_ref = next(it) if has_state else None
    wf_ref = next(it) if cast_blocks else None
    yb_ref, ns_ref = next(it), next(it)
    wr_ref = next(it) if cast_blocks else None
    pbuf = next(it)
    if cast_blocks:
        _side_cast(wf_ref, wr_ref, cast_blocks)
    dk = LANES
    dv = 2 * LANES
    hpb = 2

    def tile(s, r0, q2, k2, v2, z2, cq, sq, ck, sk):
        for hh in range(hpb):
            q = q2[:, hh * dk:(hh + 1) * dk]
            k = k2[:, hh * dk:(hh + 1) * dk]
            v = v2[:, hh * dv:(hh + 1) * dv].astype(BF16)
            z = z2[:, hh * dv:(hh + 1) * dv]
            qr = q * cq + pltpu.roll(q, dk // 2, axis=1) * sq
            kr = k * ck + pltpu.roll(k, dk // 2, axis=1) * sk
            state = ns_ref[s, hh]
            scores = lax.dot_general(qr.astype(BF16), kr.astype(BF16),
                                     (((1,), (1,)), ((), ())), preferred_element_type=F32)
            p = (scores * mask_ref[hh]).astype(BF16)
            o = _dot(p, v) + _dot((qr * qd_ref[hh]).astype(BF16), state.astype(BF16))
            kv = lax.dot_general((kr * kd_ref[hh]).astype(BF16), v,
                                 (((0,), (0,)), ((), ())), preferred_element_type=F32)
            ns_ref[s, hh] = bd_ref[hh] * state + kv
            on = o * lax.rsqrt(jnp.mean(o * o, axis=-1, keepdims=True) + EPS)
            yb_ref[s, pl.ds(r0, tt), hh * dv:(hh + 1) * dv] = (on * _silu(z)).astype(BF16)

    def projections(hr):
        return (_dot(hr, wq_ref[...]), _dot(hr, wk_ref[...]),
                _dot(hr, wv_ref[...]), _dot(hr, wz_ref[...]))

    if has_state:
        ns_ref[...] = st_ref[...]
    else:
        ns_ref[...] = jnp.zeros(ns_ref.shape, F32)

    if mc <= ts:
        assert mc == tt
        for s in range(sb):
            def chunk(i, carry, s=s):
                r0 = pl.multiple_of(i * mc, mc)
                q2, k2, v2, z2 = projections(h_ref[s, pl.ds(r0, mc), :])
                rows = pl.ds(r0, mc)
                tile(s, r0, q2, k2, v2, z2, rq_ref[0, rows, :], rq_ref[1, rows, :],
                     rk_ref[0, rows, :], rk_ref[1, rows, :])
                return carry

            lax.fori_loop(0, ts // mc, chunk, 0, unroll=CHUNK_UNROLL)
    else:
        assert tt == ts
        hr = h_ref[...].reshape(sb * ts, h_ref.shape[2])
        for k, p in enumerate(projections(hr)):
            pbuf[k, :, 0:p.shape[1]] = p
        for s in range(sb):
            rows = slice(s * ts, (s + 1) * ts)
            tile(s, 0, pbuf[0, rows, 0:hpb * dk], pbuf[1, rows, 0:hpb * dk],
                 pbuf[2, rows, :], pbuf[3, rows, :],
                 rq_ref[0], rq_ref[1], rk_ref[0], rk_ref[1])


def _retention_tables(t, tt, pos0, chunk):
    dk = LANES
    f32 = np.float32
    inv = f32(ROPE_BASE) ** (-np.arange(0, dk, 2, dtype=f32) / f32(dk))
    ang = (pos0 + np.arange(t)).astype(f32)[:, None] * inv[None, :]
    cos, sin = np.cos(ang), np.sin(ang)
    rope_q = np.stack([np.concatenate([cos, cos], -1), np.concatenate([-sin, sin], -1)])
    rope_k = rope_q * f32(dk ** -0.5)
    lg = np.log(f32(1.0) - f32(2.0) ** (f32(-5.0) - np.arange(N_HEADS, dtype=f32)))
    idx = np.arange(tt, dtype=f32)
    visible = (np.arange(tt)[None, :] // chunk) <= (np.arange(tt)[:, None] // chunk)
    mask = np.exp(lg[:, None, None] * np.abs(idx[:, None] - idx[None, :])) * visible.astype(f32)
    q_dec = np.broadcast_to(np.exp(lg[:, None] * (idx + f32(1.0)))[..., None], (N_HEADS, tt, dk))
    k_dec = np.broadcast_to(np.exp(lg[:, None] * (f32(tt - 1.0) - idx))[..., None], (N_HEADS, tt, dk))
    b_dec = np.broadcast_to(np.exp(lg * f32(tt))[:, None, None], (N_HEADS, 1, 2 * dk))
    return tuple(jnp.asarray(a, dtype=F32) for a in (rope_q, rope_k, mask, q_dec, k_dec, b_dec))


def _ret_branch(h, w_in, state, pos0, sb, mc, cast_f32=None):
    b, t, d = h.shape
    dk, dv, hpb = LANES, 2 * LANES, 2
    nhp = N_HEADS // hpb
    tt = min(RET_TILE, t)
    chunk = min(REF_CHUNK, t)
    has_state = state is not None
    rope_q, rope_k, mask, q_dec, k_dec, b_dec = _retention_tables(t, tt, pos0, chunk)
    q_off = 0
    k_off = q_off + nhp
    v_off = 2 * N_HEADS * dk // (hpb * dv)
    z_off = v_off + nhp
    in_specs = [pl.BlockSpec((sb, t, d), lambda i, g: (i, 0, 0)),
                pl.BlockSpec((d, hpb * dk), lambda i, g: (0, q_off + g)),
                pl.BlockSpec((d, hpb * dk), lambda i, g: (0, k_off + g)),
                pl.BlockSpec((d, hpb * dv), lambda i, g: (0, v_off + g)),
                pl.BlockSpec((d, hpb * dv), lambda i, g: (0, z_off + g)),
                pl.BlockSpec((2, t, dk), lambda i, g: (0, 0, 0)),
                pl.BlockSpec((2, t, dk), lambda i, g: (0, 0, 0)),
                pl.BlockSpec((hpb, tt, tt), lambda i, g: (g, 0, 0)),
                pl.BlockSpec((hpb, tt, dk), lambda i, g: (g, 0, 0)),
                pl.BlockSpec((hpb, tt, dk), lambda i, g: (g, 0, 0)),
                pl.BlockSpec((hpb, 1, dv), lambda i, g: (g, 0, 0))]
    args = [h, w_in, w_in, w_in, w_in, rope_q, rope_k, mask, q_dec, k_dec, b_dec]
    if has_state:
        in_specs.append(pl.BlockSpec((sb, hpb, dk, dv), lambda i, g: (i, g, 0, 0)))
        args.append(state)
    out_specs = [pl.BlockSpec((sb, t, hpb * dv), lambda i, g: (i, 0, g)),
                 pl.BlockSpec((sb, hpb, dk, dv), lambda i, g: (i, g, 0, 0))]
    out_shape = [jax.ShapeDtypeStruct((b, t, N_HEADS * dv), BF16),
                 jax.ShapeDtypeStruct((b, N_HEADS, dk, dv), F32)]
    cast_blocks = 0
    if cast_f32 is not None:
        c_in, c_out, c_shape, cast_blocks = _cast_specs(cast_f32, 0, cast_f32.shape[2], (b // sb) * nhp,
                                                        lambda i, g: i * nhp + g)
        in_specs.append(c_in)
        args.append(cast_f32)
        out_specs.append(c_out)
        out_shape.append(c_shape)
    pbuf_rows = sb * t if mc > t else SUBLANES
    return pl.pallas_call(
        functools.partial(_ret_body, sb=sb, ts=t, mc=mc, tt=tt, has_state=has_state,
                          cast_blocks=cast_blocks),
        grid=(b // sb, nhp),
        in_specs=in_specs,
        out_specs=out_specs,
        out_shape=out_shape,
        scratch_shapes=[pltpu.VMEM((4, pbuf_rows, hpb * dv), F32)],
        compiler_params=pltpu.CompilerParams(dimension_semantics=("arbitrary", "arbitrary"),
                                             vmem_limit_bytes=RESIDENT_VMEM_LIMIT),
        name="ret_branch",
    )(*args)


def _merge_body(h_ref, ya_ref, yb_ref, wma_ref, wmb_ref, wa_ref, wb_ref, o_ref, *, mc):
    def chunk(i, carry):
        r0 = pl.multiple_of(i * mc, mc)
        rows = pl.ds(r0, mc)
        hr = h_ref[rows, :]
        ma = _sigmoid(_dot(hr, wma_ref[...])) * _dot(ya_ref[rows, :], wa_ref[0])
        mb = _sigmoid(_dot(hr, wmb_ref[...])) * _dot(yb_ref[rows, :], wb_ref[0])
        o_ref[rows, :] = (ma + mb).astype(BF16)
        return carry

    lax.fori_loop(0, h_ref.shape[0] // mc, chunk, 0, unroll=CHUNK_UNROLL)


def _merge(h2, ya2, yb2, w_in, w_branch, tm, mc):
    n, d = h2.shape
    nb = 512
    nblk = d // nb
    ma_off = (w_in.shape[1] - 2 * d) // nb
    mb_off = ma_off + nblk
    act = pl.BlockSpec((tm, d), lambda i, j: (i, 0))
    in_specs = [act, act, act,
                pl.BlockSpec((d, nb), lambda i, j: (0, ma_off + j)),
                pl.BlockSpec((d, nb), lambda i, j: (0, mb_off + j)),
                pl.BlockSpec((1, d, nb), lambda i, j: (0, 0, j)),
                pl.BlockSpec((1, d, nb), lambda i, j: (1, 0, j))]
    args = [h2, ya2, yb2, w_in, w_in, w_branch, w_branch]
    return pl.pallas_call(
        functools.partial(_merge_body, mc=mc),
        grid=(n // tm, nblk),
        in_specs=in_specs,
        out_specs=[pl.BlockSpec((tm, nb), lambda i, j: (i, j))],
        out_shape=[jax.ShapeDtypeStruct((n, d), BF16)],
        compiler_params=_params(("arbitrary", "arbitrary")),
        name="merge",
    )(*args)


def _merge_rows_body(*refs, mc, cast_blocks):
    it = iter(refs)
    h_ref, ya_ref, yb_ref, wma_ref, wmb_ref, wa_ref, wb_ref = (next(it) for _ in range(7))
    wf_ref = next(it) if cast_blocks else None
    o_ref = next(it)
    wr_ref = next(it) if cast_blocks else None
    if cast_blocks:
        _side_cast(wf_ref, wr_ref, cast_blocks)
    tm, d = h_ref.shape
    cw = COL_BLOCK
    for sub in range(tm // mc):
        rows = slice(sub * mc, (sub + 1) * mc)
        hr, yar, ybr = h_ref[rows, :], ya_ref[rows, :], yb_ref[rows, :]
        for gi in range(d // cw):
            cols = slice(gi * cw, (gi + 1) * cw)
            ma = _sigmoid(_dot(hr, wma_ref[:, cols])) * _dot(yar, wa_ref[0, :, cols])
            mb = _sigmoid(_dot(hr, wmb_ref[:, cols])) * _dot(ybr, wb_ref[0, :, cols])
            o_ref[rows, cols] = (ma + mb).astype(BF16)


def _merge_rows(h2, ya2, yb2, w_in, w_branch, tm, mc, cast_f32=None):
    n, d = h2.shape
    ma_blk = w_in.shape[1] // d - 2
    act = pl.BlockSpec((tm, d), lambda i, j: (i, 0))
    resident = pl.Buffered(1)
    in_specs = [act, act, act,
                pl.BlockSpec((d, d), lambda i, j: (0, ma_blk), pipeline_mode=resident),
                pl.BlockSpec((d, d), lambda i, j: (0, ma_blk + 1), pipeline_mode=resident),
                pl.BlockSpec((1, d, d), lambda i, j: (0, 0, 0), pipeline_mode=resident),
                pl.BlockSpec((1, d, d), lambda i, j: (1, 0, 0), pipeline_mode=resident)]
    args = [h2, ya2, yb2, w_in, w_in, w_branch, w_branch]
    out_specs = [act]
    out_shape = [jax.ShapeDtypeStruct((n, d), BF16)]
    cast_blocks = 0
    if cast_f32 is not None:
        c_in, c_out, c_shape, cast_blocks = _cast_specs(cast_f32, 0, cast_f32.shape[2], n // tm,
                                                        lambda i, j: i)
        in_specs.append(c_in)
        args.append(cast_f32)
        out_specs.append(c_out)
        out_shape.append(c_shape)
    return pl.pallas_call(
        functools.partial(_merge_rows_body, mc=mc, cast_blocks=cast_blocks),
        grid=(n // tm, 1),
        in_specs=in_specs,
        out_specs=out_specs,
        out_shape=out_shape,
        compiler_params=pltpu.CompilerParams(dimension_semantics=("arbitrary", "arbitrary"),
                                             vmem_limit_bytes=RESIDENT_VMEM_LIMIT),
        name="merge_rows",
    )(*args)


def _out_body(m_ref, w_ref, x_ref, mod_ref, g_ref, y_ref, obuf, *, sb, ts, mc):
    g = g_ref[...]

    def finish(o, x, gate):
        ms = jnp.mean(o * o, axis=-1, keepdims=True)
        return x + gate * (o * lax.rsqrt(ms + EPS) * g)

    if mc <= ts:
        d = w_ref.shape[1]
        cw = COL_BLOCK
        for s in range(sb):
            gain = mod_ref[s, 2:3, :] * g
            for c in range(ts // mc):
                rows = slice(c * mc, (c + 1) * mc)
                mr = m_ref[s, rows, :]
                ss = jnp.zeros((mc, 1), F32)
                for gi in range(d // cw):
                    cols = slice(gi * cw, (gi + 1) * cw)
                    o = _dot(mr, w_ref[:, cols])
                    obuf[rows, cols] = o
                    ss = ss + jnp.sum(o * o, axis=-1, keepdims=True)
                rs = lax.rsqrt(ss * (1.0 / d) + EPS)
                for gi in range(d // cw):
                    cols = slice(gi * cw, (gi + 1) * cw)
                    y_ref[s, rows, cols] = x_ref[s, rows, cols] + gain[:, cols] * (obuf[rows, cols] * rs)
    else:
        obuf[...] = _dot(m_ref[...].reshape(sb * ts, m_ref.shape[2]), w_ref[...])
        for s in range(sb):
            y_ref[s] = finish(obuf[s * ts:(s + 1) * ts, :], x_ref[s], mod_ref[s, 2:3, :])


def _out_proj(merged, w_out, x, mod3, g_post, sb, tr, mc):
    b, t, d = x.shape
    obuf_rows = sb * tr
    act = lambda i, j: (i, j, 0)
    return pl.pallas_call(
        functools.partial(_out_body, sb=sb, ts=tr, mc=mc),
        grid=(b // sb, t // tr),
        in_specs=[pl.BlockSpec((sb, tr, d), act),
                  pl.BlockSpec((d, d), lambda i, j: (0, 0)),
                  pl.BlockSpec((sb, tr, d), act),
                  pl.BlockSpec((sb, 3, d), lambda i, j: (i, 0, 0)),
                  pl.BlockSpec((1, d), lambda i, j: (0, 0))],
        out_specs=pl.BlockSpec((sb, tr, d), act),
        out_shape=jax.ShapeDtypeStruct((b, t, d), F32),
        scratch_shapes=[pltpu.VMEM((obuf_rows, d), F32)],
        compiler_params=pltpu.CompilerParams(dimension_semantics=("arbitrary", "arbitrary"),
                                             vmem_limit_bytes=RESIDENT_VMEM_LIMIT),
        name="out_proj",
    )(merged, w_out, x, mod3, g_post.reshape(1, d))


class _LayerWeights:
    def __init__(self, w_in, w_branch, w_out, conv):
        self.w_in, self.w_branch, self.w_out, self.conv = w_in, w_branch, w_out, conv
        self.rest = self.branch = self.out = None


def _mixer_layer(x, mod3, hist, state, pos0, g_pre, g_post, conv_w, conv_b, w):
    b, t, d = x.shape
    long_seq = t >= ROW_CHUNK
    sb = 1 if long_seq else b
    tr = 512 if long_seq else t
    mc = ROW_CHUNK if long_seq else b * t
    conv_cols = w.conv.shape[1]
    if long_seq:
        outs = _conv_rows(x, mod3, g_pre, w.conv, conv_w, conv_b, hist, tr, ROW_CHUNK,
                          cast_f32=w.w_in[None] if w.rest is None else None, cast_col0=conv_cols)
        h, ya, new_hist = outs[:3]
        if w.rest is None:
            w.rest = outs[3][0]
    else:
        if w.rest is None:
            w.rest = w.w_in[:, conv_cols:].astype(BF16)
        h, ya, new_hist = _conv_seqs(x, mod3, g_pre, w.conv, conv_w, conv_b, hist, sb)
    outs = _ret_branch(h, w.rest, state, pos0, sb, mc,
                       cast_f32=w.w_branch if long_seq and w.branch is None else None)
    yb, new_state = outs[:2]
    if w.branch is None:
        w.branch = outs[2] if long_seq else w.w_branch.astype(BF16)
    n = b * t
    acts = (h.reshape(n, d), ya.reshape(n, d), yb.reshape(n, d))
    if long_seq:
        outs = _merge_rows(*acts, w.rest, w.branch, tr, tr,
                           cast_f32=w.w_out[None] if w.out is None else None)
    else:
        outs = _merge(*acts, w.rest, w.branch, n, ROW_CHUNK)
    if w.out is None:
        w.out = outs[1][0] if long_seq else w.w_out.astype(BF16)
    y = _out_proj(outs[0].reshape(b, t, d), w.out, x, mod3, g_post, sb, 2 * tr if long_seq else tr, mc)
    return y, new_hist, new_state


def kernel(x_prompt, x_sample, c_prompt, c_sample, state_conv, state_ret, ada_w, ada_b, norm_pre,
           norm_post, w_in, conv_w, conv_b, w_branch, w_out):
    depth = w_in.shape[0]
    d = x_prompt.shape[2]
    hp, hs = x_prompt, x_sample
    conv_p, ret_p, conv_s, ret_s = [], [], [], []
    for l in range(depth):
        mod_p, mod_s, w_conv = _modulation(c_prompt, c_sample, ada_w[l], ada_b[l], w_in[l], 4 * d)
        shared = (norm_pre[l], norm_post[l], conv_w[l], conv_b[l],
                  _LayerWeights(w_in[l], w_branch[l], w_out[l], w_conv))
        hp, cp, rp = _mixer_layer(hp, mod_p.reshape(-1, 3, d), None, None, 0, *shared)
        hs, cs, rs = _mixer_layer(hs, mod_s.reshape(-1, 3, d), state_conv[l], state_ret[l], PAST_LEN,
                                  *shared)
        conv_p.append(cp)
        ret_p.append(rp)
        conv_s.append(cs)
        ret_s.append(rs)
    return (hp, hs, jnp.stack(conv_p), jnp.stack(ret_p), jnp.stack(conv_s), jnp.stack(ret_s))
```

```python
import functools

import jax
import jax.numpy as jnp
import numpy as np
from jax import lax
from jax.experimental import pallas as pl
from jax.experimental.pallas import tpu as pltpu

F32 = jnp.float32
BF16 = jnp.bfloat16

EPS = 1e-6
ROPE_BASE = 10000.0
REF_CHUNK = 64
N_HEADS = 8
CONV_WIDTH = 3

LANES = 128
SUBLANES = 8
COL_BLOCK = 256
ROW_CHUNK = 256
RET_TILE = 256
CHUNK_UNROLL = True
PAST_LEN = 1024
VMEM_LIMIT = 56 * 1024 * 1024
RESIDENT_VMEM_LIMIT = 62 * 1024 * 1024


def _dot(a, b):
    return jnp.dot(a, b, preferred_element_type=F32)


def _sigmoid(x):
    return 1.0 / (1.0 + jnp.exp(-x))


def _silu(x):
    return x * _sigmoid(x)


def _params(semantics):
    return pltpu.CompilerParams(dimension_semantics=semantics, vmem_limit_bytes=VMEM_LIMIT)


def _cast_specs(w_f32, col0, cols, n_steps, step_of):
    m, d, _ = w_f32.shape
    per = cols // COL_BLOCK
    assert cols % COL_BLOCK == 0 and col0 % COL_BLOCK == 0 and m * per <= n_steps

    def block(*ids):
        return jnp.minimum(step_of(*ids), m * per - 1)

    in_spec = pl.BlockSpec((1, d, COL_BLOCK),
                           lambda *ids: (block(*ids) // per, 0, col0 // COL_BLOCK + block(*ids) % per))
    out_spec = pl.BlockSpec((1, d, COL_BLOCK), lambda *ids: (block(*ids) // per, 0, block(*ids) % per))
    return in_spec, out_spec, jax.ShapeDtypeStruct((m, d, cols), BF16), m * per


def _side_cast(wf_ref, wr_ref, n_blocks):
    step = pl.program_id(0) * pl.num_programs(1) + pl.program_id(1)

    @pl.when(step < n_blocks)
    def _():
        wr_ref[...] = wf_ref[...].astype(BF16)


def _mod_body(cp_ref, cs_ref, w_ref, b_ref, wf_ref, op_ref, os_ref, wr_ref):
    wr_ref[...] = wf_ref[...].astype(BF16)
    w = w_ref[...].astype(BF16)
    op_ref[...] = _dot(_silu(cp_ref[...]).astype(BF16), w) + b_ref[...]
    os_ref[...] = _dot(_silu(cs_ref[...]).astype(BF16), w) + b_ref[...]


def _modulation(c_a, c_b, ada_w, ada_b, w_f32, cast_cols):
    d = c_a.shape[1]
    n_out = ada_w.shape[1]
    nb = 768
    steps = n_out // nb
    cb = cast_cols // steps
    assert cast_cols % steps == 0 and cb % LANES == 0
    cond = lambda c: pl.BlockSpec((c.shape[0], d), lambda j: (0, 0))
    mod = lambda c: pl.BlockSpec((c.shape[0], nb), lambda j: (0, j))
    return pl.pallas_call(
        _mod_body,
        grid=(steps,),
        in_specs=[cond(c_a), cond(c_b),
                  pl.BlockSpec((d, nb), lambda j: (0, j)),
                  pl.BlockSpec((1, nb), lambda j: (0, j)),
                  pl.BlockSpec((d, cb), lambda j: (0, j))],
        out_specs=[mod(c_a), mod(c_b), pl.BlockSpec((d, cb), lambda j: (0, j))],
        out_shape=[jax.ShapeDtypeStruct((c_a.shape[0], n_out), F32),
                   jax.ShapeDtypeStruct((c_b.shape[0], n_out), F32),
                   jax.ShapeDtypeStruct((d, cast_cols), BF16)],
        compiler_params=_params(("arbitrary",)),
        name="modulation",
    )(c_a, c_b, ada_w, ada_b.reshape(1, n_out), w_f32)


def _prenorm(x, g_pre, scale1, shift):
    ms = jnp.mean(x * x, axis=-1, keepdims=True)
    return (x * lax.rsqrt(ms + EPS) * g_pre * scale1 + shift).astype(BF16)


def _conv_epilogue(ubuf, ec, taps, gb, gc, gu, zc):
    w0, w1, w2, cb = taps
    top = SUBLANES
    u = gc * gu
    ubuf[top:top + ec, :] = u
    f0 = ubuf[top - 2:top - 2 + ec, :]
    f1 = ubuf[top - 1:top - 1 + ec, :]
    conv = cb + w0 * f0 + w1 * f1 + w2 * u
    ubuf[top - 2:top, :] = ubuf[top + ec - 2:top + ec, :]
    return (gb * conv * _silu(zc)).astype(BF16)


def _conv_rows_body(*refs, mc, has_hist, cast_blocks):
    it = iter(refs)
    x_ref, mod_ref, g_ref, wb_ref, wc_ref, wu_ref, wz_ref, cw_ref, cb_ref = (next(it) for _ in range(9))
    hist_ref = next(it) if has_hist else None
    wf_ref = next(it) if cast_blocks else None
    h_ref, ya_ref, nh_ref = next(it), next(it), next(it)
    wr_ref = next(it) if cast_blocks else None
    ubuf = next(it)
    if cast_blocks:
        _side_cast(wf_ref, wr_ref, cast_blocks)
    tr, d = x_ref.shape[1], x_ref.shape[2]
    cw = ubuf.shape[2]
    ng = d // cw
    top = SUBLANES
    r = pl.program_id(1)

    @pl.when(r == 0)
    def _():
        for gi in range(ng):
            cols = slice(gi * cw, (gi + 1) * cw)
            ubuf[gi, top - 2:top, :] = hist_ref[0, :, cols] if has_hist else jnp.zeros((2, cw), F32)

    g_pre = g_ref[...]
    shift = mod_ref[0, 0:1, :]
    scale1 = 1.0 + mod_ref[0, 1:2, :]
    for sub in range(tr // mc):
        rows = slice(sub * mc, (sub + 1) * mc)
        h_ref[0, rows, :] = _prenorm(x_ref[0, rows, :], g_pre, scale1, shift)
        hr = h_ref[0, rows, :]
        for gi in range(ng):
            cols = slice(gi * cw, (gi + 1) * cw)
            taps = (cw_ref[0:1, cols], cw_ref[1:2, cols], cw_ref[2:3, cols], cb_ref[:, cols])
            ya_ref[0, rows, cols] = _conv_epilogue(
                ubuf.at[gi], mc, taps, gc=_dot(hr, wc_ref[:, cols]), gu=_dot(hr, wu_ref[:, cols]),
                zc=_dot(hr, wz_ref[:, cols]), gb=_dot(hr, wb_ref[:, cols]))

    @pl.when(r == pl.num_programs(1) - 1)
    def _():
        for gi in range(ng):
            nh_ref[0, :, gi * cw:(gi + 1) * cw] = ubuf[gi, top - 2:top, :]


def _conv_rows(x, mod3, g_pre, w_conv, conv_w, conv_b, hist, tr, mc, cast_f32=None, cast_col0=0):
    b, t, d = x.shape
    cw = COL_BLOCK
    nr = t // tr
    has_hist = hist is not None
    row_block = lambda i, r: (i, r, 0)
    per_seq = lambda i, r: (i, 0, 0)
    whole = lambda i, r: (0, 0)
    in_specs = [pl.BlockSpec((1, tr, d), row_block),
                pl.BlockSpec((1, 3, d), per_seq),
                pl.BlockSpec((1, d), whole)]
    in_specs += [pl.BlockSpec((d, d), functools.partial(lambda i, r, k: (0, k), k=k),
                              pipeline_mode=pl.Buffered(1)) for k in range(4)]
    in_specs += [pl.BlockSpec((CONV_WIDTH, d), whole), pl.BlockSpec((1, d), whole)]
    args = [x, mod3, g_pre.reshape(1, d), w_conv, w_conv, w_conv, w_conv, conv_w, conv_b.reshape(1, d)]
    if has_hist:
        in_specs.append(pl.BlockSpec((1, CONV_WIDTH - 1, d), per_seq))
        args.append(hist)
    out_specs = [pl.BlockSpec((1, tr, d), row_block),
                 pl.BlockSpec((1, tr, d), row_block),
                 pl.BlockSpec((1, CONV_WIDTH - 1, d), per_seq)]
    out_shape = [jax.ShapeDtypeStruct((b, t, d), BF16),
                 jax.ShapeDtypeStruct((b, t, d), BF16),
                 jax.ShapeDtypeStruct((b, CONV_WIDTH - 1, d), F32)]
    cast_blocks = 0
    if cast_f32 is not None:
        c_in, c_out, c_shape, cast_blocks = _cast_specs(
            cast_f32, cast_col0, cast_f32.shape[2] - cast_col0, b * nr, lambda i, r: i * nr + r)
        in_specs.append(c_in)
        args.append(cast_f32)
        out_specs.append(c_out)
        out_shape.append(c_shape)
    return pl.pallas_call(
        functools.partial(_conv_rows_body, mc=mc, has_hist=has_hist, cast_blocks=cast_blocks),
        grid=(b, nr),
        in_specs=in_specs,
        out_specs=out_specs,
        out_shape=out_shape,
        scratch_shapes=[pltpu.VMEM((d // cw, mc + SUBLANES, cw), F32)],
        compiler_params=pltpu.CompilerParams(dimension_semantics=("arbitrary", "arbitrary"),
                                             vmem_limit_bytes=RESIDENT_VMEM_LIMIT),
        name="conv_rows",
    )(*args)


def _conv_seqs_body(*refs, sb, ts, has_hist):
    if has_hist:
        (x_ref, mod_ref, g_ref, wb_ref, wc_ref, wu_ref, wz_ref, cw_ref, cb_ref, hist_ref,
         h_ref, ya_ref, nh_ref, ubuf, pbuf) = refs
    else:
        (x_ref, mod_ref, g_ref, wb_ref, wc_ref, wu_ref, wz_ref, cw_ref, cb_ref,
         h_ref, ya_ref, nh_ref, ubuf, pbuf) = refs

    @pl.when(pl.program_id(1) == 0)
    def _():
        h_ref[...] = _prenorm(x_ref[...], g_ref[...], 1.0 + mod_ref[:, 1:2, :], mod_ref[:, 0:1, :])

    taps = (cw_ref[0:1, :], cw_ref[1:2, :], cw_ref[2:3, :], cb_ref[...])
    top = SUBLANES
    hr = h_ref[...].reshape(sb * ts, h_ref.shape[2])
    for k, w_ref in enumerate((wb_ref, wc_ref, wu_ref, wz_ref)):
        pbuf[k] = _dot(hr, w_ref[...])
    for s in range(sb):
        ubuf[top - 2:top, :] = hist_ref[s] if has_hist else jnp.zeros((2, ubuf.shape[1]), F32)
        rows = slice(s * ts, (s + 1) * ts)
        ya_ref[s] = _conv_epilogue(ubuf, ts, taps, pbuf[0, rows, :], pbuf[1, rows, :],
                                   pbuf[2, rows, :], pbuf[3, rows, :])
        nh_ref[s] = ubuf[top - 2:top, :]


def _conv_seqs(x, mod3, g_pre, w_in, conv_w, conv_b, hist, sb):
    b, t, d = x.shape
    cw = COL_BLOCK
    ng = d // cw
    has_hist = hist is not None
    seq_block = lambda i, g: (i, 0, 0)
    in_specs = [pl.BlockSpec((sb, t, d), seq_block),
                pl.BlockSpec((sb, 3, d), seq_block),
                pl.BlockSpec((1, d), lambda i, g: (0, 0))]
    in_specs += [pl.BlockSpec((d, cw), functools.partial(lambda i, g, k: (0, k * ng + g), k=k))
                 for k in range(4)]
    in_specs += [pl.BlockSpec((CONV_WIDTH, cw), lambda i, g: (0, g)),
                 pl.BlockSpec((1, cw), lambda i, g: (0, g))]
    args = [x, mod3, g_pre.reshape(1, d), w_in, w_in, w_in, w_in, conv_w, conv_b.reshape(1, d)]
    if has_hist:
        in_specs.append(pl.BlockSpec((sb, CONV_WIDTH - 1, cw), lambda i, g: (i, 0, g)))
        args.append(hist)
    return pl.pallas_call(
        functools.partial(_conv_seqs_body, sb=sb, ts=t, has_hist=has_hist),
        grid=(b // sb, ng),
        in_specs=in_specs,
        out_specs=[pl.BlockSpec((sb, t, d), seq_block),
                   pl.BlockSpec((sb, t, cw), lambda i, g: (i, 0, g)),
                   pl.BlockSpec((sb, CONV_WIDTH - 1, cw), lambda i, g: (i, 0, g))],
        out_shape=[jax.ShapeDtypeStruct((b, t, d), BF16),
                   jax.ShapeDtypeStruct((b, t, d), BF16),
                   jax.ShapeDtypeStruct((b, CONV_WIDTH - 1, d), F32)],
        scratch_shapes=[pltpu.VMEM((t + SUBLANES, cw), F32),
                        pltpu.VMEM((4, sb * t, cw), F32)],
        compiler_params=_params(("arbitrary", "arbitrary")),
        name="conv_seqs",
    )(*args)


def _ret_body(*refs, sb, ts, mc, tt, has_state, cast_blocks):
    it = iter(refs)
    (h_ref, wq_ref, wk_ref, wv_ref, wz_ref, rq_ref, rk_ref, mask_ref, qd_ref, kd_ref,
     bd_ref) = (next(it) for _ in range(11))
    st_ref = next(it) if has_state else None
    wf_ref = next(it) if cast_blocks else None
    yb_ref, ns_ref = next(it), next(it)
    wr_ref = next(it) if cast_blocks else None
    pbuf = next(it)
    if cast_blocks:
        _side_cast(wf_ref, wr_ref, cast_blocks)
    dk = LANES
    dv = 2 * LANES
    hpb = 2

    def tile(s, r0, q2, k2, v2, z2, cq, sq, ck, sk):
        for hh in range(hpb):
            q = q2[:, hh * dk:(hh + 1) * dk]
            k = k2[:, hh * dk:(hh + 1) * dk]
            v = v2[:, hh * dv:(hh + 1) * dv].astype(BF16)
            z = z2[:, hh * dv:(hh + 1) * dv]
            qr = q * cq + pltpu.roll(q, dk // 2, axis=1) * sq
            kr = k * ck + pltpu.roll(k, dk // 2, axis=1) * sk
            state = ns_ref[s, hh]
            scores = lax.dot_general(qr.astype(BF16), kr.astype(BF16),
                                     (((1,), (1,)), ((), ())), preferred_element_type=F32)
            p = (scores * mask_ref[hh]).astype(BF16)
            o = _dot(p, v) + _dot((qr * qd_ref[hh]).astype(BF16), state.astype(BF16))
            kv = lax.dot_general((kr * kd_ref[hh]).astype(BF16), v,
                                 (((0,), (0,)), ((), ())), preferred_element_type=F32)
            ns_ref[s, hh] = bd_ref[hh] * state + kv
            on = o * lax.rsqrt(jnp.mean(o * o, axis=-1, keepdims=True) + EPS)
            yb_ref[s, pl.ds(r0, tt), hh * dv:(hh + 1) * dv] = (on * _silu(z)).astype(BF16)

    def projections(hr):
        return (_dot(hr, wq_ref[...]), _dot(hr, wk_ref[...]),
                _dot(hr, wv_ref[...]), _dot(hr, wz_ref[...]))

    if has_state:
        ns_ref[...] = st_ref[...]
    else:
        ns_ref[...] = jnp.zeros(ns_ref.shape, F32)

    if mc <= ts:
        assert mc == tt
        for s in range(sb):
            def chunk(i, carry, s=s):
                r0 = pl.multiple_of(i * mc, mc)
                q2, k2, v2, z2 = projections(h_ref[s, pl.ds(r0, mc), :])
                rows = pl.ds(r0, mc)
                tile(s, r0, q2, k2, v2, z2, rq_ref[0, rows, :], rq_ref[1, rows, :],
                     rk_ref[0, rows, :], rk_ref[1, rows, :])
                return carry

            lax.fori_loop(0, ts // mc, chunk, 0, unroll=CHUNK_UNROLL)
    else:
        assert tt == ts
        hr = h_ref[...].reshape(sb * ts, h_ref.shape[2])
        for k, p in enumerate(projections(hr)):
            pbuf[k, :, 0:p.shape[1]] = p
        for s in range(sb):
            rows = slice(s * ts, (s + 1) * ts)
            tile(s, 0, pbuf[0, rows, 0:hpb * dk], pbuf[1, rows, 0:hpb * dk],
                 pbuf[2, rows, :], pbuf[3, rows, :],
                 rq_ref[0], rq_ref[1], rk_ref[0], rk_ref[1])


def _retention_tables(t, tt, pos0, chunk):
    dk = LANES
    f32 = np.float32
    inv = f32(ROPE_BASE) ** (-np.arange(0, dk, 2, dtype=f32) / f32(dk))
    ang = (pos0 + np.arange(t)).astype(f32)[:, None] * inv[None, :]
    cos, sin = np.cos(ang), np.sin(ang)
    rope_q = np.stack([np.concatenate([cos, cos], -1), np.concatenate([-sin, sin], -1)])
    rope_k = rope_q * f32(dk ** -0.5)
    lg = np.log(f32(1.0) - f32(2.0) ** (f32(-5.0) - np.arange(N_HEADS, dtype=f32)))
    idx = np.arange(tt, dtype=f32)
    visible = (np.arange(tt)[None, :] // chunk) <= (np.arange(tt)[:, None] // chunk)
    mask = np.exp(lg[:, None, None] * np.abs(idx[:, None] - idx[None, :])) * visible.astype(f32)
    q_dec = np.broadcast_to(np.exp(lg[:, None] * (idx + f32(1.0)))[..., None], (N_HEADS, tt, dk))
    k_dec = np.broadcast_to(np.exp(lg[:, None] * (f32(tt - 1.0) - idx))[..., None], (N_HEADS, tt, dk))
    b_dec = np.broadcast_to(np.exp(lg * f32(tt))[:, None, None], (N_HEADS, 1, 2 * dk))
    return tuple(jnp.asarray(a, dtype=F32) for a in (rope_q, rope_k, mask, q_dec, k_dec, b_dec))


def _ret_branch(h, w_in, state, pos0, sb, mc, cast_f32=None):
    b, t, d = h.shape
    dk, dv, hpb = LANES, 2 * LANES, 2
    nhp = N_HEADS // hpb
    tt = min(RET_TILE, t)
    chunk = min(REF_CHUNK, t)
    has_state = state is not None
    rope_q, rope_k, mask, q_dec, k_dec, b_dec = _retention_tables(t, tt, pos0, chunk)
    q_off = 0
    k_off = q_off + nhp
    v_off = 2 * N_HEADS * dk // (hpb * dv)
    z_off = v_off + nhp
    in_specs = [pl.BlockSpec((sb, t, d), lambda i, g: (i, 0, 0)),
                pl.BlockSpec((d, hpb * dk), lambda i, g: (0, q_off + g)),
                pl.BlockSpec((d, hpb * dk), lambda i, g: (0, k_off + g)),
                pl.BlockSpec((d, hpb * dv), lambda i, g: (0, v_off + g)),
                pl.BlockSpec((d, hpb * dv), lambda i, g: (0, z_off + g)),
                pl.BlockSpec((2, t, dk), lambda i, g: (0, 0, 0)),
                pl.BlockSpec((2, t, dk), lambda i, g: (0, 0, 0)),
                pl.BlockSpec((hpb, tt, tt), lambda i, g: (g, 0, 0)),
                pl.BlockSpec((hpb, tt, dk), lambda i, g: (g, 0, 0)),
                pl.BlockSpec((hpb, tt, dk), lambda i, g: (g, 0, 0)),
                pl.BlockSpec((hpb, 1, dv), lambda i, g: (g, 0, 0))]
    args = [h, w_in, w_in, w_in, w_in, rope_q, rope_k, mask, q_dec, k_dec, b_dec]
    if has_state:
        in_specs.append(pl.BlockSpec((sb, hpb, dk, dv), lambda i, g: (i, g, 0, 0)))
        args.append(state)
    out_specs = [pl.BlockSpec((sb, t, hpb * dv), lambda i, g: (i, 0, g)),
                 pl.BlockSpec((sb, hpb, dk, dv), lambda i, g: (i, g, 0, 0))]
    out_shape = [jax.ShapeDtypeStruct((b, t, N_HEADS * dv), BF16),
                 jax.ShapeDtypeStruct((b, N_HEADS, dk, dv), F32)]
    cast_blocks = 0
    if cast_f32 is not None:
        c_in, c_out, c_shape, cast_blocks = _cast_specs(cast_f32, 0, cast_f32.shape[2], (b // sb) * nhp,
                                                        lambda i, g: i * nhp + g)
        in_specs.append(c_in)
        args.append(cast_f32)
        out_specs.append(c_out)
        out_shape.append(c_shape)
    pbuf_rows = sb * t if mc > t else SUBLANES
    return pl.pallas_call(
        functools.partial(_ret_body, sb=sb, ts=t, mc=mc, tt=tt, has_state=has_state,
                          cast_blocks=cast_blocks),
        grid=(b // sb, nhp),
        in_specs=in_specs,
        out_specs=out_specs,
        out_shape=out_shape,
        scratch_shapes=[pltpu.VMEM((4, pbuf_rows, hpb * dv), F32)],
        compiler_params=pltpu.CompilerParams(dimension_semantics=("arbitrary", "arbitrary"),
                                             vmem_limit_bytes=RESIDENT_VMEM_LIMIT),
        name="ret_branch",
    )(*args)


def _merge_body(h_ref, ya_ref, yb_ref, wma_ref, wmb_ref, wa_ref, wb_ref, o_ref, *, mc):
    def chunk(i, carry):
        r0 = pl.multiple_of(i * mc, mc)
        rows = pl.ds(r0, mc)
        hr = h_ref[rows, :]
        ma = _sigmoid(_dot(hr, wma_ref[...])) * _dot(ya_ref[rows, :], wa_ref[0])
        mb = _sigmoid(_dot(hr, wmb_ref[...])) * _dot(yb_ref[rows, :], wb_ref[0])
        o_ref[rows, :] = (ma + mb).astype(BF16)
        return carry

    lax.fori_loop(0, h_ref.shape[0] // mc, chunk, 0, unroll=CHUNK_UNROLL)


def _merge(h2, ya2, yb2, w_in, w_branch, tm, mc):
    n, d = h2.shape
    nb = 512
    nblk = d // nb
    ma_off = (w_in.shape[1] - 2 * d) // nb
    mb_off = ma_off + nblk
    act = pl.BlockSpec((tm, d), lambda i, j: (i, 0))
    in_specs = [act, act, act,
                pl.BlockSpec((d, nb), lambda i, j: (0, ma_off + j)),
                pl.BlockSpec((d, nb), lambda i, j: (0, mb_off + j)),
                pl.BlockSpec((1, d, nb), lambda i, j: (0, 0, j)),
                pl.BlockSpec((1, d, nb), lambda i, j: (1, 0, j))]
    args = [h2, ya2, yb2, w_in, w_in, w_branch, w_branch]
    return pl.pallas_call(
        functools.partial(_merge_body, mc=mc),
        grid=(n // tm, nblk),
        in_specs=in_specs,
        out_specs=[pl.BlockSpec((tm, nb), lambda i, j: (i, j))],
        out_shape=[jax.ShapeDtypeStruct((n, d), BF16)],
        compiler_params=_params(("arbitrary", "arbitrary")),
        name="merge",
    )(*args)


def _merge_rows_body(*refs, mc, cast_blocks):
    it = iter(refs)
    h_ref, ya_ref, yb_ref, wma_ref, wmb_ref, wa_ref, wb_ref = (next(it) for _ in range(7))
    wf_ref = next(it) if cast_blocks else None
    o_ref = next(it)
    wr_ref = next(it) if cast_blocks else None
    if cast_blocks:
        _side_cast(wf_ref, wr_ref, cast_blocks)
    tm, d = h_ref.shape
    cw = COL_BLOCK
    for sub in range(tm // mc):
        rows = slice(sub * mc, (sub + 1) * mc)
        hr, yar, ybr = h_ref[rows, :], ya_ref[rows, :], yb_ref[rows, :]
        for gi in range(d // cw):
            cols = slice(gi * cw, (gi + 1) * cw)
            ma = _sigmoid(_dot(hr, wma_ref[:, cols])) * _dot(yar, wa_ref[0, :, cols])
            mb = _sigmoid(_dot(hr, wmb_ref[:, cols])) * _dot(ybr, wb_ref[0, :, cols])
            o_ref[rows, cols] = (ma + mb).astype(BF16)


def _merge_rows(h2, ya2, yb2, w_in, w_branch, tm, mc, cast_f32=None):
    n, d = h2.shape
    ma_blk = w_in.shape[1] // d - 2
    act = pl.BlockSpec((tm, d), lambda i, j: (i, 0))
    resident = pl.Buffered(1)
    in_specs = [act, act, act,
                pl.BlockSpec((d, d), lambda i, j: (0, ma_blk), pipeline_mode=resident),
                pl.BlockSpec((d, d), lambda i, j: (0, ma_blk + 1), pipeline_mode=resident),
                pl.BlockSpec((1, d, d), lambda i, j: (0, 0, 0), pipeline_mode=resident),
                pl.BlockSpec((1, d, d), lambda i, j: (1, 0, 0), pipeline_mode=resident)]
    args = [h2, ya2, yb2, w_in, w_in, w_branch, w_branch]
    out_specs = [act]
    out_shape = [jax.ShapeDtypeStruct((n, d), BF16)]
    cast_blocks = 0
    if cast_f32 is not None:
        c_in, c_out, c_shape, cast_blocks = _cast_specs(cast_f32, 0, cast_f32.shape[2], n // tm,
                                                        lambda i, j: i)
        in_specs.append(c_in)
        args.append(cast_f32)
        out_specs.append(c_out)
        out_shape.append(c_shape)
    return pl.pallas_call(
        functools.partial(_merge_rows_body, mc=mc, cast_blocks=cast_blocks),
        grid=(n // tm, 1),
        in_specs=in_specs,
        out_specs=out_specs,
        out_shape=out_shape,
        compiler_params=pltpu.CompilerParams(dimension_semantics=("arbitrary", "arbitrary"),
                                             vmem_limit_bytes=RESIDENT_VMEM_LIMIT),
        name="merge_rows",
    )(*args)


def _out_body(m_ref, w_ref, x_ref, mod_ref, g_ref, y_ref, obuf, *, sb, ts, mc):
    g = g_ref[...]

    def finish(o, x, gate):
        ms = jnp.mean(o * o, axis=-1, keepdims=True)
        return x + gate * (o * lax.rsqrt(ms + EPS) * g)

    if mc <= ts:
        d = w_ref.shape[1]
        cw = COL_BLOCK
        for s in range(sb):
            gain = mod_ref[s, 2:3, :] * g
            for c in range(ts // mc):
                rows = slice(c * mc, (c + 1) * mc)
                mr = m_ref[s, rows, :]
                ss = jnp.zeros((mc, 1), F32)
                for gi in range(d // cw):
                    cols = slice(gi * cw, (gi + 1) * cw)
                    o = _dot(mr, w_ref[:, cols])
                    obuf[rows, cols] = o
                    ss = ss + jnp.sum(o * o, axis=-1, keepdims=True)
                rs = lax.rsqrt(ss * (1.0 / d) + EPS)
                for gi in range(d // cw):
                    cols = slice(gi * cw, (gi + 1) * cw)
                    y_ref[s, rows, cols] = x_ref[s, rows, cols] + gain[:, cols] * (obuf[rows, cols] * rs)
    else:
        obuf[...] = _dot(m_ref[...].reshape(sb * ts, m_ref.shape[2]), w_ref[...])
        for s in range(sb):
            y_ref[s] = finish(obuf[s * ts:(s + 1) * ts, :], x_ref[s], mod_ref[s, 2:3, :])


def _out_proj(merged, w_out, x, mod3, g_post, sb, tr, mc):
    b, t, d = x.shape
    obuf_rows = sb * tr
    act = lambda i, j: (i, j, 0)
    return pl.pallas_call(
        functools.partial(_out_body, sb=sb, ts=tr, mc=mc),
        grid=(b // sb, t // tr),
        in_specs=[pl.BlockSpec((sb, tr, d), act),
                  pl.BlockSpec((d, d), lambda i, j: (0, 0)),
                  pl.BlockSpec((sb, tr, d), act),
                  pl.BlockSpec((sb, 3, d), lambda i, j: (i, 0, 0)),
                  pl.BlockSpec((1, d), lambda i, j: (0, 0))],
        out_specs=pl.BlockSpec((sb, tr, d), act),
        out_shape=jax.ShapeDtypeStruct((b, t, d), F32),
        scratch_shapes=[pltpu.VMEM((obuf_rows, d), F32)],
        compiler_params=pltpu.CompilerParams(dimension_semantics=("arbitrary", "arbitrary"),
                                             vmem_limit_bytes=RESIDENT_VMEM_LIMIT),
        name="out_proj",
    )(merged, w_out, x, mod3, g_post.reshape(1, d))


class _LayerWeights:
    def __init__(self, w_in, w_branch, w_out, conv):
        self.w_in, self.w_branch, self.w_out, self.conv = w_in, w_branch, w_out, conv
        self.rest = self.branch = self.out = None


def _mixer_layer(x, mod3, hist, state, pos0, g_pre, g_post, conv_w, conv_b, w):
    b, t, d = x.shape
    long_seq = t >= ROW_CHUNK
    sb = 1 if long_seq else b
    tr = 512 if long_seq else t
    mc = ROW_CHUNK if long_seq else b * t
    conv_cols = w.conv.shape[1]
    if long_seq:
        outs = _conv_rows(x, mod3, g_pre, w.conv, conv_w, conv_b, hist, tr, tr,
                          cast_f32=w.w_in[None] if w.rest is None else None, cast_col0=conv_cols)
        h, ya, new_hist = outs[:3]
        if w.rest is None:
            w.rest = outs[3][0]
    else:
        if w.rest is None:
            w.rest = w.w_in[:, conv_cols:].astype(BF16)
        h, ya, new_hist = _conv_seqs(x, mod3, g_pre, w.conv, conv_w, conv_b, hist, sb)
    outs = _ret_branch(h, w.rest, state, pos0, sb, mc,
                       cast_f32=w.w_branch if long_seq and w.branch is None else None)
    yb, new_state = outs[:2]
    if w.branch is None:
        w.branch = outs[2] if long_seq else w.w_branch.astype(BF16)
    n = b * t
    acts = (h.reshape(n, d), ya.reshape(n, d), yb.reshape(n, d))
    if long_seq:
        outs = _merge_rows(*acts, w.rest, w.branch, tr, tr,
                           cast_f32=w.w_out[None] if w.out is None else None)
    else:
        outs = _merge(*acts, w.rest, w.branch, n, ROW_CHUNK)
    if w.out is None:
        w.out = outs[1][0] if long_seq else w.w_out.astype(BF16)
    y = _out_proj(outs[0].reshape(b, t, d), w.out, x, mod3, g_post, sb, 2 * tr if long_seq else tr, mc)
    return y, new_hist, new_state


def kernel(x_prompt, x_sample, c_prompt, c_sample, state_conv, state_ret, ada_w, ada_b, norm_pre,
           norm_post, w_in, conv_w, conv_b, w_branch, w_out):
    depth = w_in.shape[0]
    d = x_prompt.shape[2]
    hp, hs = x_prompt, x_sample
    conv_p, ret_p, conv_s, ret_s = [], [], [], []
    for l in range(depth):
        mod_p, mod_s, w_conv = _modulation(c_prompt, c_sample, ada_w[l], ada_b[l], w_in[l], 4 * d)
        shared = (norm_pre[l], norm_post[l], conv_w[l], conv_b[l],
                  _LayerWeights(w_in[l], w_branch[l], w_out[l], w_conv))
        hp, cp, rp = _mixer_layer(hp, mod_p.reshape(-1, 3, d), None, None, 0, *shared)
        hs, cs, rs = _mixer_layer(hs, mod_s.reshape(-1, 3, d), state_conv[l], state_ret[l], PAST_LEN,
                                  *shared)
        conv_p.append(cp)
        ret_p.append(rp)
        conv_s.append(cs)
        ret_s.append(rs)
    return (hp, hs, jnp.stack(conv_p), jnp.stack(ret_p), jnp.stack(conv_s), jnp.stack(ret_s))
```
